```python
import jax, jax.numpy as jnp
from jax import lax
import numpy as np

D_MODEL = 1024
BATCH = 1
SEQ = 16384
DEPTH = 1
DEC_BATCH = 32
DEC_SEQ = 4
PAST_LEN = 16384
PAGE_SIZE = 128

D_MIX = D_MODEL
W_A = D_MIX // 2
H_A = 4
DK_A = W_A // H_A
DV_A = DK_A
W_B = D_MIX - W_A
H_B = 8
HD_B = W_B // H_B
H_I = 8
D_I = 64
ROT_B = HD_B // 4
ROT_I = D_I // 4
ROPE_THETA = 500000.0
TOPK_MAX = 256
Q_BLOCK = 128
CHUNK = 64
D_FF = ((8 * D_MODEL // 3 + 255) // 256) * 256
CONV_W = 3
EPS = 1e-6
NEG_SCORE = -1e30
IN_SIZES = (W_A, W_A, W_A, W_A, W_B, W_B, W_B, H_I * D_I, D_I, H_I)
N_IN = 4 * W_A + 3 * W_B + H_I * D_I + D_I + H_I

kernel_name = 'hymba_hgrn2_dsa_convffn_step'

F32 = jnp.float32


def _rmsnorm(x, g):
    xf = x.astype(F32)
    y = xf * lax.rsqrt(jnp.mean(xf * xf, axis=-1, keepdims=True) + EPS)
    return (y * g.astype(F32)).astype(x.dtype)


def _rope_partial(x, pos, rot):
    half = rot // 2
    inv = jnp.power(ROPE_THETA, -jnp.arange(half, dtype=F32) * (2.0 / rot))
    ang = pos.astype(F32)[:, None] * inv[None, :]
    shape = (1, pos.shape[0]) + (1,) * (x.ndim - 3) + (half,)
    cos = jnp.cos(ang).reshape(shape)
    sin = jnp.sin(ang).reshape(shape)
    xf = x.astype(F32)
    x1, x2, rest = xf[..., :half], xf[..., half:rot], xf[..., rot:]
    return jnp.concatenate([x1 * cos - x2 * sin, x2 * cos + x1 * sin, rest], axis=-1).astype(x.dtype)


def _split_in(proj):
    offs = [int(o) for o in np.cumsum(IN_SIZES)[:-1]]
    return jnp.split(proj, offs, axis=-1)


def _hgrn2_chunked(q, k, v, log_f, s0):
    B, T, H, DK = q.shape
    DV = v.shape[-1]
    C = CHUNK if T % CHUNK == 0 else T
    n = T // C

    def chunks(a):
        return a.astype(F32).reshape(B, n, C, H, a.shape[-1]).transpose(1, 0, 3, 2, 4)

    causal = jnp.tril(jnp.ones((C, C), dtype=bool))[:, :, None]

    def step(S, inp):
        qc, kc, vc, gc = inp
        b = jnp.cumsum(gc, axis=2)
        diff = b[:, :, :, None, :] - b[:, :, None, :, :]
        decay = jnp.exp(jnp.where(causal, diff, -jnp.inf))
        attn = jnp.einsum('bhtk,bhtsk,bhsk->bhts', qc, decay, kc)
        o = (jnp.einsum('bhts,bhsv->bhtv', attn, vc)
             + jnp.einsum('bhtk,bhkv->bhtv', qc * jnp.exp(b), S))
        b_last = b[:, :, -1:, :]
        S = (jnp.exp(b_last[:, :, 0, :])[..., None] * S
             + jnp.einsum('bhsk,bhsv->bhkv', kc * jnp.exp(b_last - b), vc))
        return S, o

    S_fin, o = lax.scan(step, s0.astype(F32), (chunks(q), chunks(k), chunks(v), chunks(log_f)))
    o = o.transpose(1, 0, 3, 2, 4).reshape(B, T, H, DV)
    return o, S_fin


def _select_keys(q_idx, w_idx, k_idx, q_pos, n_sel):
    dots = jnp.einsum('btgd,bld->btgl', q_idx.astype(F32), k_idx.astype(F32))
    scores = jnp.einsum('btgl,btg->btl', jax.nn.relu(dots), w_idx.astype(F32))
    key_pos = jnp.arange(k_idx.shape[1], dtype=jnp.int32)
    visible = key_pos[None, None, :] <= q_pos[None, :, None]
    scores = jnp.where(visible, scores, NEG_SCORE)
    _, sel = lax.top_k(scores, n_sel)
    valid = sel <= q_pos[None, :, None]
    return sel, valid


def _gathered_attention(q, k_sel, v_sel, valid):
    s = jnp.einsum('bthd,btkhd->bthk', q.astype(F32), k_sel.astype(F32)) * (HD_B ** -0.5)
    s = jnp.where(valid[:, :, None, :], s, -jnp.inf)
    p = jax.nn.softmax(s, axis=-1)
    return jnp.einsum('bthk,btkhd->bthd', p, v_sel.astype(F32)).astype(q.dtype)


def _attend_prompt(q, k, v, q_idx, k_idx, w_idx):
    B, S = q.shape[:2]
    n_sel = min(TOPK_MAX, S // 4)
    bidx = jnp.arange(B)[:, None, None]

    def block(i):
        t0 = i * Q_BLOCK
        sl = lambda a: lax.dynamic_slice_in_dim(a, t0, Q_BLOCK, axis=1)
        q_pos = t0 + jnp.arange(Q_BLOCK, dtype=jnp.int32)
        sel, valid = _select_keys(sl(q_idx), sl(w_idx), k_idx, q_pos, n_sel)
        return _gathered_attention(sl(q), k[bidx, sel], v[bidx, sel], valid)

    out = lax.map(block, jnp.arange(S // Q_BLOCK))
    return out.transpose(1, 0, 2, 3, 4).reshape(B, S, H_B, HD_B)


def _attend_sample(q, k_new, v_new, q_idx, k_idx_new, w_idx, cache_k, cache_v, cache_kidx, page_table):
    B, T = q.shape[:2]
    past = page_table.shape[1] * PAGE_SIZE
    n_sel = min(TOPK_MAX, (past + T) // 4)
    k_idx_past = cache_kidx[page_table].reshape(B, past, D_I)
    k_idx_all = jnp.concatenate([k_idx_past.astype(k_idx_new.dtype), k_idx_new], axis=1)
    q_pos = past + jnp.arange(T, dtype=jnp.int32)
    sel, valid = _select_keys(q_idx, w_idx, k_idx_all, q_pos, n_sel)
    bidx = jnp.arange(B)[:, None, None]
    from_new = (sel >= past)[..., None, None]
    s_past = jnp.minimum(sel, past - 1)
    phys = page_table[bidx, s_past // PAGE_SIZE]
    off = s_past % PAGE_SIZE
    s_new = jnp.clip(sel - past, 0, T - 1)
    k_sel = jnp.where(from_new, k_new[bidx, s_new], cache_k[phys, off].astype(k_new.dtype))
    v_sel = jnp.where(from_new, v_new[bidx, s_new], cache_v[phys, off].astype(v_new.dtype))
    return _gathered_attention(q, k_sel, v_sel, valid)


def _layer(x, pos, attend, hgrn_s0, conv_s0, norm_mix_gain, w_in, lb, hgrn_norm_gain, w_out,
           norm_ffn_gain, w_up, conv_w, conv_b, w_down):
    B, T, _ = x.shape
    h = _rmsnorm(x, norm_mix_gain)
    qa, fa, ia, ga, qb, kb, vb, qi, ki, wi = _split_in(h @ w_in)
    qa = jax.nn.silu(qa).reshape(B, T, H_A, DK_A)
    fg = lb + (1.0 - lb) * jax.nn.sigmoid(fa.astype(F32))
    ka = (1.0 - fg).reshape(B, T, H_A, DK_A)
    log_f = jnp.log(fg).reshape(B, T, H_A, DK_A)
    oa, s_a = _hgrn2_chunked(qa, ka, ia.reshape(B, T, H_A, DV_A), log_f, hgrn_s0)
    oa = _rmsnorm(oa, hgrn_norm_gain) * jax.nn.silu(ga.astype(F32)).reshape(B, T, H_A, DV_A)
    qb = _rope_partial(qb.reshape(B, T, H_B, HD_B), pos, ROT_B)
    kb = _rope_partial(kb.reshape(B, T, H_B, HD_B), pos, ROT_B)
    vb = vb.reshape(B, T, H_B, HD_B)
    qi = _rope_partial(qi.reshape(B, T, H_I, D_I), pos, ROT_I)
    ki = _rope_partial(ki, pos, ROT_I)
    wi = wi * (H_I ** -0.5 * D_I ** -0.5)
    ob = attend(qb, kb, vb, qi, ki, wi)
    mixed = jnp.concatenate([oa.reshape(B, T, W_A).astype(x.dtype),
                             ob.reshape(B, T, W_B).astype(x.dtype)], axis=-1)
    x = x + mixed @ w_out
    h2 = _rmsnorm(x, norm_ffn_gain)
    u, v = jnp.split(h2 @ w_up, 2, axis=-1)
    u_pad = jnp.concatenate([conv_s0.astype(u.dtype), u], axis=1)
    c = conv_b
    for j in range(CONV_W):
        c = c + conv_w[j] * u_pad[:, j:j + T]
    x = x + (jax.nn.silu(c) * v) @ w_down
    return x, kb, vb, ki, s_a, u_pad[:, T:]


def setup_inputs(seed: int = 0) -> dict:
    key = jax.random.key(seed)
    ks = jax.random.split(key, 20)
    n_pages = PAST_LEN // PAGE_SIZE
    n_used = DEC_BATCH * n_pages
    n_pool = n_used + max(1, n_used // 4)
    nrm = lambda k, shape, scale: scale * jax.random.normal(k, shape, F32)
    page_table = jax.random.permutation(ks[7], n_pool)[:n_used].reshape(DEC_BATCH, n_pages).astype(jnp.int32)
    return {
        'x_prompt': nrm(ks[0], (BATCH, SEQ, D_MODEL), 1.0),
        'x_sample': nrm(ks[1], (DEC_BATCH, DEC_SEQ, D_MODEL), 1.0),
        'cache_k': nrm(ks[2], (DEPTH, n_pool, PAGE_SIZE, H_B, HD_B), 1.0),
        'cache_v': nrm(ks[3], (DEPTH, n_pool, PAGE_SIZE, H_B, HD_B), 1.0),
        'cache_kidx': nrm(ks[4], (DEPTH, n_pool, PAGE_SIZE, D_I), 1.0),
        'state_hgrn': nrm(ks[5], (DEPTH, DEC_BATCH, H_A, DK_A, DV_A), 0.5),
        'state_conv': nrm(ks[6], (DEPTH, DEC_BATCH, CONV_W - 1, D_FF), 1.0),
        'page_table': page_table,
        'norm_mix_gain': 1.0 + nrm(ks[8], (DEPTH, D_MODEL), 0.02),
        'w_in': nrm(ks[9], (DEPTH, D_MODEL, N_IN), D_MODEL ** -0.5),
        'lb_logits': nrm(ks[10], (DEPTH + 1, W_A), 0.1),
        'hgrn_norm_gain': 1.0 + nrm(ks[11], (DEPTH, DV_A), 0.02),
        'w_out': nrm(ks[12], (DEPTH, D_MIX, D_MODEL), D_MIX ** -0.5),
        'norm_ffn_gain': 1.0 + nrm(ks[13], (DEPTH, D_MODEL), 0.02),
        'w_up': nrm(ks[14], (DEPTH, D_MODEL, 2 * D_FF), D_MODEL ** -0.5),
        'conv_w': nrm(ks[15], (DEPTH, CONV_W, D_FF), CONV_W ** -0.5),
        'conv_b': nrm(ks[16], (DEPTH, D_FF), 0.01),
        'w_down': nrm(ks[17], (DEPTH, D_FF, D_MODEL), D_FF ** -0.5),
        'final_norm_gain': 1.0 + nrm(ks[18], (D_MODEL,), 0.02),
    }


def reference(x_prompt, x_sample, cache_k, cache_v, cache_kidx, state_hgrn, state_conv, page_table,
              norm_mix_gain, w_in, lb_logits, hgrn_norm_gain, w_out, norm_ffn_gain, w_up, conv_w,
              conv_b, w_down, final_norm_gain):
    B, S, _ = x_prompt.shape
    T = x_sample.shape[1]
    past = page_table.shape[1] * PAGE_SIZE
    pos_p = jnp.arange(S, dtype=jnp.int32)
    pos_s = past + jnp.arange(T, dtype=jnp.int32)
    lower_bounds = jnp.cumsum(jax.nn.softmax(lb_logits.astype(F32), axis=0), axis=0)
    s0_p = jnp.zeros((B, H_A, DK_A, DV_A), F32)
    c0_p = jnp.zeros((B, CONV_W - 1, D_FF), x_prompt.dtype)
    hp, hs = x_prompt, x_sample
    kp, vp, kip, sp, cp = [], [], [], [], []
    ksl, vsl, kis, ss, cs = [], [], [], [], []
    for l in range(DEPTH):
        wts = (norm_mix_gain[l], w_in[l], lower_bounds[l], hgrn_norm_gain[l], w_out[l],
               norm_ffn_gain[l], w_up[l], conv_w[l], conv_b[l], w_down[l])
        hp, k1, v1, ki1, s1, c1 = _layer(hp, pos_p, _attend_prompt, s0_p, c0_p, *wts)
        attend_s = lambda q, k, v, qi, ki, wi: _attend_sample(
            q, k, v, qi, ki, wi, cache_k[l], cache_v[l], cache_kidx[l], page_table)
        hs, k2, v2, ki2, s2, c2 = _layer(hs, pos_s, attend_s, state_hgrn[l], state_conv[l], *wts)
        kp.append(k1); vp.append(v1); kip.append(ki1); sp.append(s1.astype(state_hgrn.dtype)); cp.append(c1)
        ksl.append(k2); vsl.append(v2); kis.append(ki2); ss.append(s2.astype(state_hgrn.dtype)); cs.append(c2)
    y_prompt = _rmsnorm(hp, final_norm_gain)
    y_sample = _rmsnorm(hs, final_norm_gain)
    return (y_prompt, y_sample,
            jnp.stack(kp), jnp.stack(vp), jnp.stack(kip), jnp.stack(sp), jnp.stack(cp),
            jnp.stack(ksl), jnp.stack(vsl), jnp.stack(kis), jnp.stack(ss), jnp.stack(cs))
```

```python
import functools

import jax
import jax.numpy as jnp
from jax import lax
from jax.experimental import pallas as pl
from jax.experimental.pallas import tpu as pltpu

F32 = jnp.float32
BF16 = jnp.bfloat16
I32 = jnp.int32

H_A = 4
DK_A = 128
H_B = 8
HD_B = 64
H_I = 8
D_I = 64
ROT = 16
ROPE_THETA = 500000.0
TOPK_MAX = 256
PAGE_SIZE = 128
CONV_W = 3
EPS = 1e-6
NEG_SCORE = -1e30
MASK_VALUE = -1e30
GROUP = 512

LANES = 128
SUBLANES = 8
VMEM_LIMIT_BYTES = 58 * 1024 * 1024

HGRN_CHUNK = 64
HGRN_SUB = 16
BISECT_PASSES = 22
ATT_TQ = 128
ATT_TK = 256
KV_CHUNK_PAGES = 8


def _silu(x):
    return x * jax.nn.sigmoid(x)


def _rms(x, g):
    return x * lax.rsqrt(jnp.mean(x * x, axis=-1, keepdims=True) + EPS) * g


def _split_bf16(x):
    hi = x.astype(BF16)
    lo = (x - hi.astype(F32)).astype(BF16)
    return hi, lo


def _rope(x, c, sa, sb):
    w = x.shape[1]
    return x * c + pltpu.roll(x, ROT // 2, 1) * sa + pltpu.roll(x, w - ROT // 2, 1) * sb


def _proj_in_kernel(x_ref, g_ref, w_ref, lb_ref, c_ref, sa_ref, sb_ref,
                    qa_ref, ka_ref, lf_ref, ia_ref, ga_ref, qb_ref, kb_ref, vb_ref, qi_ref, kw_ref, h_ref):
    h_ref[...] = _rms(x_ref[...], g_ref[...]).astype(BF16)

    def grp(i, width=GROUP):
        return jnp.dot(h_ref[...], w_ref[:, i * GROUP:i * GROUP + width], preferred_element_type=F32)

    c = jnp.tile(c_ref[...], (1, GROUP // LANES))
    sa = jnp.tile(sa_ref[...], (1, GROUP // LANES))
    sb = jnp.tile(sb_ref[...], (1, GROUP // LANES))

    qa_ref[...] = _silu(grp(0))
    lb = lb_ref[...]
    fg = lb + (1.0 - lb) * jax.nn.sigmoid(grp(1))
    ka_ref[...] = 1.0 - fg
    lf_ref[...] = jnp.log(fg)
    ia_ref[...] = grp(2)
    ga_ref[...] = _silu(grp(3))
    qb_ref[...] = _rope(grp(4), c, sa, sb)
    kb_ref[...] = _rope(grp(5), c, sa, sb)
    vb_ref[...] = grp(6)
    qi_ref[...] = _rope(grp(7), c, sa, sb)
    kw = grp(8, LANES)
    lane = lax.broadcasted_iota(I32, kw.shape, 1)
    roped = _rope(kw, c_ref[...], sa_ref[...], sb_ref[...])
    kw_ref[...] = jnp.where(lane < D_I, roped, kw * (H_I ** -0.5 * D_I ** -0.5))


def _proj_in(x2d, gain, w_pad, lb, cos_t, sin_a, sin_b, tm):
    m, d = x2d.shape
    n_pad = w_pad.shape[1]
    row = lambda i: (i, 0)
    const = lambda i: (0, 0)
    big = pl.BlockSpec((tm, GROUP), row)
    out_shapes = [jax.ShapeDtypeStruct((m, GROUP), F32)] * 9 + [jax.ShapeDtypeStruct((m, LANES), F32)]
    return pl.pallas_call(
        _proj_in_kernel,
        grid=(m // tm,),
        in_specs=[
            pl.BlockSpec((tm, d), row),
            pl.BlockSpec((1, d), const),
            pl.BlockSpec((d, n_pad), const, pipeline_mode=pl.Buffered(1)),
            pl.BlockSpec((1, GROUP), const),
            pl.BlockSpec((tm, LANES), row),
            pl.BlockSpec((tm, LANES), row),
            pl.BlockSpec((tm, LANES), row),
        ],
        out_specs=[big] * 9 + [pl.BlockSpec((tm, LANES), row)],
        out_shape=out_shapes,
        scratch_shapes=[pltpu.VMEM((tm, d), BF16)],
        compiler_params=pltpu.CompilerParams(
            dimension_semantics=("arbitrary",), vmem_limit_bytes=VMEM_LIMIT_BYTES),
        name="proj_in",
    )(x2d, gain, w_pad, lb, cos_t, sin_a, sin_b)


def _shift_rows(x, d):
    if d == 0:
        return x
    return pltpu.roll(x, d, 0)


def _cumsum_rows(g):
    c = g.shape[0]
    row = lax.broadcasted_iota(I32, g.shape, 0)
    k = 1
    while k < c:
        g = g + jnp.where(row >= k, _shift_rows(g, k), 0.0)
        k *= 2
    return g


def _hgrn_kernel(q_ref, k_ref, lf_ref, v_ref, ga_ref, gain_ref, s0_ref, o_ref, sfin_ref, st_ref, *, chunk, sub):
    ci = pl.program_id(1)

    @pl.when(ci == 0)
    def _():
        st_ref[...] = s0_ref[0]

    nsb = chunk // sub
    row = lax.broadcasted_iota(I32, (chunk, DK_A), 0)
    row_in_sub = row & (sub - 1)
    gain = gain_ref[...]

    for h in range(H_A):
        sl = slice(h * DK_A, (h + 1) * DK_A)
        q = q_ref[0, :, sl]
        k = k_ref[0, :, sl]
        v = v_ref[0, :, sl]
        b = _cumsum_rows(lf_ref[0, :, sl])
        st = st_ref[h]

        refs = [jnp.zeros((1, DK_A), F32)] + [b[i * sub - 1:i * sub, :] for i in range(1, nsb)]
        ref_rows = jnp.concatenate([jnp.broadcast_to(r, (sub, DK_A)) for r in refs], axis=0)
        q_rel = q * jnp.exp(b - ref_rows)

        o = lax.dot_general((q * jnp.exp(b)).astype(BF16), st.astype(BF16),
                            (((1,), (1,)), ((), ())), preferred_element_type=F32)

        off_rows = [jnp.zeros((sub, DK_A), F32)]
        for i in range(1, nsb):
            n_prev = i * sub
            k_rel = (k[:n_prev] * jnp.exp(refs[i] - b[:n_prev])).astype(BF16)
            att = lax.dot_general(q_rel[n_prev:n_prev + sub].astype(BF16), k_rel,
                                  (((1,), (1,)), ((), ())), preferred_element_type=F32)
            off_rows.append(jnp.dot(att.astype(BF16), v[:n_prev].astype(BF16), preferred_element_type=F32))
        if nsb > 1:
            o = o + jnp.concatenate(off_rows, axis=0)

        for d in range(sub):
            valid = row_in_sub >= d
            e = jnp.exp(jnp.where(valid, b - _shift_rows(b, d), 0.0))
            w = jnp.sum(q * _shift_rows(k, d) * e, axis=1, keepdims=True)
            o = o + jnp.where(valid, w * _shift_rows(v, d), 0.0)

        b_last = b[chunk - 1:chunk, :]
        k_dec = (k * jnp.exp(b_last - b)).astype(BF16)
        st_ref[h] = st * jnp.exp(b_last) + lax.dot_general(
            v.astype(BF16), k_dec, (((0,), (0,)), ((), ())), preferred_element_type=F32)

        o_ref[0, :, sl] = _rms(o, gain) * ga_ref[0, :, sl]

    @pl.when(ci == pl.num_programs(1) - 1)
    def _():
        sfin_ref[0] = st_ref[...]


def _hgrn(q, k, lf, v, ga, gain, s0_t, chunk, sub):
    b, t, w = q.shape
    blk = pl.BlockSpec((1, chunk, w), lambda bi, ci: (bi, ci, 0))
    st_spec = pl.BlockSpec((1, H_A, DK_A, DK_A), lambda bi, ci: (bi, 0, 0, 0))
    return pl.pallas_call(
        functools.partial(_hgrn_kernel, chunk=chunk, sub=sub),
        grid=(b, t // chunk),
        in_specs=[blk, blk, blk, blk, blk, pl.BlockSpec((1, DK_A), lambda bi, ci: (0, 0)), st_spec],
        out_specs=[blk, st_spec],
        out_shape=[jax.ShapeDtypeStruct((b, t, w), F32), jax.ShapeDtypeStruct((b, H_A, DK_A, DK_A), F32)],
        scratch_shapes=[pltpu.VMEM((H_A, DK_A, DK_A), F32)],
        compiler_params=pltpu.CompilerParams(
            dimension_semantics=("arbitrary", "arbitrary"), vmem_limit_bytes=VMEM_LIMIT_BYTES),
        name="hgrn",
    )(q, k, lf, v, ga, gain, s0_t)


def _rep_sum(x):
    return jnp.broadcast_to(jnp.sum(x, axis=1, keepdims=True), x.shape)


def _rep_max(x):
    return jnp.broadcast_to(jnp.max(x, axis=1, keepdims=True), x.shape)


def _rep_min(x):
    return jnp.broadcast_to(jnp.min(x, axis=1, keepdims=True), x.shape)


def _any_true(flag):
    return jnp.max(flag) > 0.5


def _select_rows(sc_ref, nkb, n_hidden_tail, n_sel):
    _, r, w = sc_ref.shape
    shape = (r, w)
    neg = jnp.float32(NEG_SCORE)
    inf = jnp.float32(jnp.inf)
    kf = jnp.float32(n_sel)
    tail = n_hidden_tail.astype(F32)
    zeros = jnp.zeros(shape, F32)
    lane = lax.broadcasted_iota(I32, shape, 1)

    def count_ge(c):
        def body(j, acc):
            return acc + jnp.where(sc_ref[j] >= c, 1.0, 0.0)
        return _rep_sum(lax.fori_loop(0, nkb, body, zeros)) + jnp.where(neg >= c, tail, 0.0)

    def stats_body(j, carry):
        vmax, vmin_real, n_real, n_ge_neg, gmin = carry
        x = sc_ref[j]
        real = x > neg
        return (jnp.maximum(vmax, x),
                jnp.minimum(vmin_real, jnp.where(real, x, inf)),
                n_real + jnp.where(real, 1.0, 0.0),
                n_ge_neg + jnp.where(x >= neg, 1.0, 0.0),
                jnp.minimum(gmin, jnp.where(x > -inf, x, inf)))

    vmax, vmin_real, n_real, n_ge_neg, gmin = lax.fori_loop(
        0, nkb, stats_body, (zeros - inf, zeros + inf, zeros, zeros, zeros + inf))
    has_tail = tail > 0.5
    vmax = _rep_max(vmax)
    vmax = jnp.where(has_tail, jnp.maximum(vmax, neg), vmax)
    vmin_real = _rep_min(vmin_real)
    gmin = _rep_min(gmin)
    gmin = jnp.where(has_tail, jnp.minimum(gmin, neg), gmin)
    n_real = _rep_sum(n_real)
    n_ge_neg = _rep_sum(n_ge_neg) + tail
    c_max = count_ge(vmax)

    at_max = c_max >= kf
    few_real = jnp.logical_and(jnp.logical_not(at_max), n_real < kf)
    at_neg = jnp.logical_and(few_real, n_ge_neg >= kf)
    below_neg = jnp.logical_and(few_real, n_ge_neg < kf)

    done = jnp.where(jnp.logical_or(at_max, at_neg), 1.0, 0.0)
    thr = jnp.where(at_max, vmax, jnp.where(at_neg, neg, zeros))
    n_gt = jnp.where(at_neg, n_real, zeros)
    n_ge = jnp.where(at_max, c_max, jnp.where(at_neg, n_ge_neg, zeros))
    tie = jnp.where(jnp.logical_and(done > 0.5, n_ge > kf), 1.0, 0.0)
    lo = jnp.where(below_neg, gmin, vmin_real)
    hi = jnp.where(below_neg, neg, vmax)
    c_hi = jnp.where(below_neg, n_ge_neg, c_max)

    def bisect(st):
        done, thr, n_gt, tie, lo, hi, c_hi = st
        mid = lo + (hi - lo) * 0.5
        c = count_ge(mid)
        live = done < 0.5
        hit = jnp.logical_and(live, c == kf)
        up = jnp.logical_and(live, c > kf)
        dn = jnp.logical_and(live, c < kf)
        return (jnp.where(hit, 1.0, done), jnp.where(hit, mid, thr), n_gt, tie,
                jnp.where(up, mid, lo), jnp.where(dn, mid, hi), jnp.where(dn, c, c_hi))

    def snap(st):
        done, thr, n_gt, tie, lo, hi, c_hi = st

        def body(j, m):
            x = sc_ref[j]
            return jnp.maximum(m, jnp.where(x < hi, x, -inf))
        below = _rep_max(lax.fori_loop(0, nkb, body, zeros - inf))
        below = jnp.where(jnp.logical_and(tail > 0.5, neg < hi), jnp.maximum(below, neg), below)
        c = count_ge(below)
        live = done < 0.5
        fin = jnp.logical_and(live, c >= kf)
        mv = jnp.logical_and(live, c < kf)
        return (jnp.where(fin, 1.0, done), jnp.where(fin, below, thr), jnp.where(fin, c_hi, n_gt),
                jnp.where(fin, jnp.where(c > kf, 1.0, 0.0), tie),
                lo, jnp.where(mv, below, hi), jnp.where(mv, c, c_hi))

    st = (done, thr, n_gt, tie, lo, hi, c_hi)
    _, st = lax.while_loop(
        lambda ps: jnp.logical_and(ps[0] < BISECT_PASSES, _any_true(1.0 - ps[1][0])),
        lambda ps: (ps[0] + 1, bisect(ps[1])), (jnp.int32(0), st))
    st = lax.while_loop(lambda s: _any_true(1.0 - s[0]), lambda s: snap(bisect(s)), st)
    done, thr, n_gt, tie, lo, hi, c_hi = st

    need = kf - n_gt
    n_stored = nkb * w

    def count_eq_upto(jb):
        def body(j, acc):
            x = sc_ref[j]
            idx = j * w + lane
            return acc + jnp.where(jnp.logical_and(x == thr, idx <= jb), 1.0, 0.0)
        return _rep_sum(lax.fori_loop(0, nkb, body, zeros))

    def jstep(pj):
        p, (jlo, jhi) = pj
        jmid = jlo + lax.shift_right_arithmetic(jhi - jlo, 1)
        ok = count_eq_upto(jmid) >= need
        return p + 1, (jnp.where(ok, jlo, jmid), jnp.where(ok, jmid, jhi))

    n_jpass = jnp.where(_any_true(tie), 16, 0)
    jlo0 = jnp.full(shape, -1, I32)
    jhi0 = jnp.zeros(shape, I32) + n_stored
    _, (_, jhi) = lax.while_loop(lambda pj: pj[0] < n_jpass, jstep, (jnp.int32(0), (jlo0, jhi0)))
    jmax = jnp.where(tie > 0.5, jhi, jnp.int32(2 ** 30))
    return thr, jmax


def _attn_prompt_kernel(qi_ref, kw_ref, qb_ref, kit_ref, kt_ref, v_ref, o_ref,
                        sc_ref, wb_ref, qc_ref, qh_ref, m_ref, l_ref, acc_ref, *, n_keys, n_sel):
    tq, tk = ATT_TQ, ATT_TK
    i = pl.program_id(0)
    nkb = (i * tq + tq + tk - 1) // tk
    q_pos = i * tq + lax.broadcasted_iota(I32, (tq, tk), 0)
    lane = lax.broadcasted_iota(I32, (tq, tk), 1)

    kw = kw_ref[...]
    for h in range(H_I):
        hi, lo = _split_bf16(qi_ref[:, h * D_I:(h + 1) * D_I])
        qc_ref[h] = jnp.concatenate([hi, lo, hi], axis=1)
        wb_ref[h] = jnp.broadcast_to(kw[:, D_I + h:D_I + h + 1], (tq, tk))

    def score_body(j, _):
        kit = kit_ref[j]
        acc = jnp.zeros((tq, tk), F32)
        for h in range(H_I):
            d = jnp.dot(qc_ref[h], kit, preferred_element_type=F32)
            acc = acc + wb_ref[h] * jnp.maximum(d, 0.0)
        visible = j * tk + lane <= q_pos
        sc_ref[j] = jnp.where(visible, acc, NEG_SCORE)
        return 0

    lax.fori_loop(0, nkb, score_body, 0)

    tail = jnp.zeros((tq, tk), I32) + (n_keys - nkb * tk)
    thr, jmax = _select_rows(sc_ref, nkb, tail, n_sel)

    m_ref[...] = jnp.full(m_ref.shape, MASK_VALUE, F32)
    l_ref[...] = jnp.zeros(l_ref.shape, F32)
    acc_ref[...] = jnp.zeros(acc_ref.shape, F32)
    for h in range(H_B):
        qh_ref[h] = (qb_ref[:, h * HD_B:(h + 1) * HD_B] * (HD_B ** -0.5)).astype(BF16)
    first_half = lax.broadcasted_iota(I32, (tq, LANES), 1) < HD_B
    first_half_k = lax.broadcasted_iota(I32, (tk, LANES), 1) < HD_B

    def att_body(j, _):
        x = sc_ref[j]
        idx = j * tk + lane
        sel = jnp.logical_or(x > thr, jnp.logical_and(x == thr, idx <= jmax))
        sel = jnp.logical_and(sel, idx <= q_pos)
        bias = jnp.where(sel, 0.0, MASK_VALUE)
        rows = pl.ds(pl.multiple_of(j * tk, tk), tk)
        for pair in range(H_B // 2):
            ps = slice(pair * LANES, (pair + 1) * LANES)
            v_pair = v_ref[rows, ps]
            v_halves = (jnp.where(first_half_k, v_pair, jnp.zeros_like(v_pair)),
                        jnp.where(first_half_k, jnp.zeros_like(v_pair), v_pair))
            alphas = []
            pv = jnp.zeros((tq, LANES), F32)
            for half in range(2):
                h = 2 * pair + half
                s = jnp.dot(qh_ref[h], kt_ref[j, h * HD_B:(h + 1) * HD_B, :], preferred_element_type=F32) + bias
                m_old = m_ref[h]
                m_new = jnp.maximum(m_old, jnp.broadcast_to(jnp.max(s, axis=1, keepdims=True), m_old.shape))
                alpha = jnp.exp(m_old - m_new)
                p = jnp.exp(s - jnp.tile(m_new, (1, tk // LANES)))
                l_ref[h] = alpha * l_ref[h] + (p[:, :LANES] + p[:, LANES:])
                m_ref[h] = m_new
                pv = pv + jnp.dot(p.astype(BF16), v_halves[half], preferred_element_type=F32)
                alphas.append(alpha)
            acc_ref[:, ps] = jnp.where(first_half, alphas[0], alphas[1]) * acc_ref[:, ps] + pv
        return 0

    lax.fori_loop(0, nkb, att_body, 0)

    for pair in range(H_B // 2):
        ps = slice(pair * LANES, (pair + 1) * LANES)
        d0 = jnp.broadcast_to(jnp.sum(l_ref[2 * pair], axis=1, keepdims=True), (tq, LANES))
        d1 = jnp.broadcast_to(jnp.sum(l_ref[2 * pair + 1], axis=1, keepdims=True), (tq, LANES))
        o_ref[:, ps] = acc_ref[:, ps] / jnp.where(first_half, d0, d1)


def _attn_prompt(qi, kw, qb, kit_blocks, kt_blocks, v_bf, n_sel):
    s = qi.shape[0]
    nkb_total = s // ATT_TK
    row = lambda i: (i, 0)
    whole = pl.BlockSpec(memory_space=pltpu.VMEM)
    return pl.pallas_call(
        functools.partial(_attn_prompt_kernel, n_keys=s, n_sel=n_sel),
        grid=(s // ATT_TQ,),
        in_specs=[
            pl.BlockSpec((ATT_TQ, GROUP), row),
            pl.BlockSpec((ATT_TQ, LANES), row),
            pl.BlockSpec((ATT_TQ, GROUP), row),
            whole, whole, whole,
        ],
        out_specs=pl.BlockSpec((ATT_TQ, GROUP), row),
        out_shape=jax.ShapeDtypeStruct((s, GROUP), F32),
        scratch_shapes=[
            pltpu.VMEM((nkb_total, ATT_TQ, ATT_TK), F32),
            pltpu.VMEM((H_I, ATT_TQ, ATT_TK), F32),
            pltpu.VMEM((H_I, ATT_TQ, 3 * D_I), BF16),
            pltpu.VMEM((H_B, ATT_TQ, HD_B), BF16),
            pltpu.VMEM((H_B, ATT_TQ, LANES), F32),
            pltpu.VMEM((H_B, ATT_TQ, LANES), F32),
            pltpu.VMEM((ATT_TQ, GROUP), F32),
        ],
        compiler_params=pltpu.CompilerParams(
            dimension_semantics=("arbitrary",), vmem_limit_bytes=VMEM_LIMIT_BYTES),
        name="attn_prompt",
    )(qi, kw, qb, kit_blocks, kt_blocks, v_bf)


def _attn_sample_kernel(pt_ref, qi_ref, kw_ref, qb_ref, kin_ref, kn_ref, vn_ref,
                        ckidx_hbm, ck_hbm, cv_hbm, o_ref,
                        kid_buf, k_buf, v_buf, sc_ref, m_ref, l_ref, acc_ref, sem_i, sem_k, sem_v,
                        *, n_pages, t_new, n_sel):
    b = pl.program_id(0)
    r = SUBLANES
    nrow = H_B * r
    n_chunks = n_pages // KV_CHUNK_PAGES

    def kidx_copy(p):
        return pltpu.make_async_copy(ckidx_hbm.at[pt_ref[b, p]], kid_buf.at[p], sem_i)

    def kv_copies(c, slot, p):
        page = pt_ref[b, c * KV_CHUNK_PAGES + p]
        return (pltpu.make_async_copy(ck_hbm.at[page], k_buf.at[slot, p], sem_k.at[slot]),
                pltpu.make_async_copy(cv_hbm.at[page], v_buf.at[slot, p], sem_v.at[slot]))

    def start_chunk(c, slot):
        for p in range(KV_CHUNK_PAGES):
            ck, cv = kv_copies(c, slot, p)
            ck.start()
            cv.start()

    def wait_chunk(c, slot):
        for p in range(KV_CHUNK_PAGES):
            ck, cv = kv_copies(c, slot, p)
            ck.wait()
            cv.wait()

    def start_kidx(p, _):
        kidx_copy(p).start()
        return 0

    def wait_kidx(p, _):
        kidx_copy(p).wait()
        return 0

    lax.fori_loop(0, n_pages, start_kidx, 0)
    start_chunk(0, 0)

    kw = kw_ref[0]
    qi = qi_ref[0]
    q_hi, q_lo, w_rows = [], [], []
    for h in range(H_I):
        hi, lo = _split_bf16(qi[:, h * D_I:(h + 1) * D_I])
        q_hi.append(hi)
        q_lo.append(lo)
        w_rows.append(jnp.broadcast_to(kw[:, D_I + h:D_I + h + 1], (r, LANES)))
    q_hi = jnp.concatenate(q_hi, axis=0)
    q_hl = jnp.concatenate([q_hi, jnp.concatenate(q_lo, axis=0)], axis=0)
    w_rows = jnp.concatenate(w_rows, axis=0)
    nt = (((1,), (1,)), ((), ()))

    def page_scores(kpage):
        k_hi, k_lo = _split_bf16(kpage)
        d2 = lax.dot_general(q_hl, k_hi, nt, preferred_element_type=F32)
        d = d2[:H_I * r] + d2[H_I * r:] + lax.dot_general(q_hi, k_lo, nt, preferred_element_type=F32)
        wd = w_rows * jnp.maximum(d, 0.0)
        acc = wd[0:r]
        for h in range(1, H_I):
            acc = acc + wd[h * r:(h + 1) * r]
        return acc

    lax.fori_loop(0, n_pages, wait_kidx, 0)

    def score_body(p, _):
        sc_ref[p] = page_scores(kid_buf[p])
        return 0

    lax.fori_loop(0, n_pages, score_body, 0)

    lane = lax.broadcasted_iota(I32, (r, LANES), 1)
    t_row = jnp.minimum(lax.broadcasted_iota(I32, (r, LANES), 0), t_new - 1)
    new_sc = page_scores(kin_ref[0])
    new_vis = lane <= t_row
    sc_ref[n_pages] = jnp.where(lane >= t_new, -jnp.inf, jnp.where(new_vis, new_sc, NEG_SCORE))

    nkb = n_pages + 1
    thr, jmax = _select_rows(sc_ref, nkb, jnp.zeros((r, LANES), I32), n_sel)

    qb = qb_ref[0] * (HD_B ** -0.5)
    col_head = lax.shift_right_logical(lax.broadcasted_iota(I32, (r, GROUP), 1), HD_B.bit_length() - 1)
    q_bd = jnp.concatenate([jnp.where(col_head == h, qb, 0.0) for h in range(H_B)], axis=0).astype(BF16)

    m_ref[...] = jnp.full(m_ref.shape, MASK_VALUE, F32)
    l_ref[...] = jnp.zeros(l_ref.shape, F32)
    acc_ref[...] = jnp.zeros(acc_ref.shape, F32)

    def attend(blk, kpage, vpage, extra_mask):
        x = sc_ref[blk]
        idx = blk * LANES + lane
        sel = jnp.logical_or(x > thr, jnp.logical_and(x == thr, idx <= jmax))
        if extra_mask is not None:
            sel = jnp.logical_and(sel, extra_mask)
        bias = jnp.tile(jnp.where(sel, 0.0, MASK_VALUE), (H_B, 1))
        s = lax.dot_general(q_bd, kpage.astype(BF16), nt, preferred_element_type=F32) + bias
        m_old = m_ref[...]
        m_new = jnp.maximum(m_old, jnp.broadcast_to(jnp.max(s, axis=1, keepdims=True), m_old.shape))
        alpha = jnp.exp(m_old - m_new)
        p = jnp.exp(s - m_new)
        l_ref[...] = alpha * l_ref[...] + p
        pv = jnp.dot(p.astype(BF16), vpage.astype(BF16), preferred_element_type=F32)
        acc_ref[...] = jnp.tile(alpha, (1, GROUP // LANES)) * acc_ref[...] + pv
        m_ref[...] = m_new

    def chunk_body(c, _):
        slot = c % 2
        wait_chunk(c, slot)

        @pl.when(c + 1 < n_chunks)
        def _():
            start_chunk(c + 1, 1 - slot)

        for p in range(KV_CHUNK_PAGES):
            attend(c * KV_CHUNK_PAGES + p, k_buf[slot, p], v_buf[slot, p], None)
        return 0

    lax.fori_loop(0, n_chunks, chunk_body, 0)
    attend(n_pages, kn_ref[0], vn_ref[0], new_vis)

    out = jnp.zeros((r, GROUP), F32)
    for h in range(H_B):
        rs = slice(h * r, (h + 1) * r)
        denom = jnp.sum(l_ref[rs, :], axis=1, keepdims=True)
        out = out + jnp.where(col_head == h, acc_ref[rs, :] / denom, 0.0)
    o_ref[0] = out


def _attn_sample(page_table, qi, kw, qb, ki_new, k_new, v_new, cache_kidx, cache_k, cache_v, t_new, n_sel):
    nb, n_pages = page_table.shape
    r = SUBLANES
    nrow = H_B * r
    per_b = lambda b, pt: (b, 0, 0)
    anyspec = pl.BlockSpec(memory_space=pl.ANY)
    grid_spec = pltpu.PrefetchScalarGridSpec(
        num_scalar_prefetch=1,
        grid=(nb,),
        in_specs=[
            pl.BlockSpec((1, r, GROUP), per_b),
            pl.BlockSpec((1, r, LANES), per_b),
            pl.BlockSpec((1, r, GROUP), per_b),
            pl.BlockSpec((1, PAGE_SIZE, D_I), per_b),
            pl.BlockSpec((1, PAGE_SIZE, GROUP), per_b),
            pl.BlockSpec((1, PAGE_SIZE, GROUP), per_b),
            anyspec, anyspec, anyspec,
        ],
        out_specs=pl.BlockSpec((1, r, GROUP), per_b),
        scratch_shapes=[
            pltpu.VMEM((n_pages, PAGE_SIZE, D_I), F32),
            pltpu.VMEM((2, KV_CHUNK_PAGES, PAGE_SIZE, GROUP), F32),
            pltpu.VMEM((2, KV_CHUNK_PAGES, PAGE_SIZE, GROUP), F32),
            pltpu.VMEM((n_pages + 1, r, LANES), F32),
            pltpu.VMEM((nrow, LANES), F32),
            pltpu.VMEM((nrow, LANES), F32),
            pltpu.VMEM((nrow, GROUP), F32),
            pltpu.SemaphoreType.DMA(()),
            pltpu.SemaphoreType.DMA((2,)),
            pltpu.SemaphoreType.DMA((2,)),
        ],
    )
    return pl.pallas_call(
        functools.partial(_attn_sample_kernel, n_pages=n_pages, t_new=t_new, n_sel=n_sel),
        grid_spec=grid_spec,
        out_shape=jax.ShapeDtypeStruct((nb, r, GROUP), F32),
        compiler_params=pltpu.CompilerParams(
            dimension_semantics=("arbitrary",), vmem_limit_bytes=VMEM_LIMIT_BYTES),
        name="attn_sample",
    )(page_table, qi, kw, qb, ki_new, k_new, v_new, cache_kidx, cache_k, cache_v)


def _out_ffn_kernel(x_ref, oa_ref, ob_ref, p1_ref, p2_ref, wo_ref, g2_ref, wu_ref, cw_ref, cb_ref, wd_ref, gf_ref,
                    y_ref, u_ref, carry_ref, x1_ref, h2_ref, *, d_ff, ff_tile, seq_rows, carry_mode):
    tm = x_ref.shape[0]
    i = pl.program_id(0)
    mixed = jnp.concatenate([oa_ref[...], ob_ref[...]], axis=1).astype(BF16)
    x1 = x_ref[...] + jnp.dot(mixed, wo_ref[...], preferred_element_type=F32)
    h2_ref[...] = _rms(x1, g2_ref[...]).astype(BF16)
    x1_ref[...] = x1

    row = lax.broadcasted_iota(I32, (tm, ff_tile), 0)
    if carry_mode:
        @pl.when(i == 0)
        def _():
            carry_ref[...] = jnp.zeros(carry_ref.shape, F32)
        t_in_seq = row
    else:
        t_in_seq = row & (seq_rows - 1)

    for c in range(d_ff // ff_tile):
        cs = slice(c * ff_tile, (c + 1) * ff_tile)
        h2 = h2_ref[...]
        u = jnp.dot(h2, wu_ref[:, cs], preferred_element_type=F32)
        v = jnp.dot(h2, wu_ref[:, d_ff + c * ff_tile:d_ff + (c + 1) * ff_tile], preferred_element_type=F32)
        if carry_mode:
            prev = carry_ref[:, cs]
            p1 = jnp.broadcast_to(prev[SUBLANES - 1:SUBLANES], (tm, ff_tile))
            p2 = jnp.where(row == 0, jnp.broadcast_to(prev[SUBLANES - 2:SUBLANES - 1], (tm, ff_tile)), p1)
            carry_ref[:, cs] = u[tm - SUBLANES:]
            u_ref[:, cs] = u[tm - SUBLANES:]
        else:
            p1 = p1_ref[:, cs]
            p2 = p2_ref[:, cs]
            u_ref[:, cs] = u
        u1 = jnp.where(t_in_seq >= 1, pltpu.roll(u, 1, 0), p1)
        u2 = jnp.where(t_in_seq >= 2, pltpu.roll(u, 2, 0), p2)
        conv = cb_ref[:, cs] + cw_ref[0:1, cs] * u2 + cw_ref[1:2, cs] * u1 + cw_ref[2:3, cs] * u
        gate = (_silu(conv) * v).astype(BF16)
        x1_ref[...] += jnp.dot(gate, wd_ref[cs, :], preferred_element_type=F32)

    y_ref[...] = _rms(x1_ref[...], gf_ref[...])


def _out_ffn(x2d, oa, ob, p1, p2, w_out, g2, w_up, conv_w, conv_b, w_down, gf, tm, seq_rows, carry_mode):
    m, d = x2d.shape
    d_ff = w_down.shape[0]
    ff_tile = 256
    row = lambda i: (i, 0)
    const = lambda i: (0, 0)
    if carry_mode:
        prev_spec = pl.BlockSpec((SUBLANES, d_ff), const)
        u_spec = pl.BlockSpec((SUBLANES, d_ff), const)
        u_shape = jax.ShapeDtypeStruct((SUBLANES, d_ff), F32)
    else:
        prev_spec = pl.BlockSpec((tm, d_ff), row)
        u_spec = pl.BlockSpec((tm, d_ff), row)
        u_shape = jax.ShapeDtypeStruct((m, d_ff), F32)
    return pl.pallas_call(
        functools.partial(_out_ffn_kernel, d_ff=d_ff, ff_tile=ff_tile, seq_rows=seq_rows, carry_mode=carry_mode),
        grid=(m // tm,),
        in_specs=[
            pl.BlockSpec((tm, d), row),
            pl.BlockSpec((tm, GROUP), row),
            pl.BlockSpec((tm, GROUP), row),
            prev_spec, prev_spec,
            pl.BlockSpec(w_out.shape, const, pipeline_mode=pl.Buffered(1)),
            pl.BlockSpec((1, d), const),
            pl.BlockSpec(w_up.shape, const, pipeline_mode=pl.Buffered(1)),
            pl.BlockSpec(conv_w.shape, const),
            pl.BlockSpec((1, d_ff), const),
            pl.BlockSpec(w_down.shape, const, pipeline_mode=pl.Buffered(1)),
            pl.BlockSpec((1, d), const),
        ],
        out_specs=[pl.BlockSpec((tm, d), row), u_spec],
        out_shape=[jax.ShapeDtypeStruct((m, d), F32), u_shape],
        scratch_shapes=[pltpu.VMEM((SUBLANES, d_ff), F32), pltpu.VMEM((tm, d), F32), pltpu.VMEM((tm, d), BF16)],
        compiler_params=pltpu.CompilerParams(
            dimension_semantics=("arbitrary",), vmem_limit_bytes=VMEM_LIMIT_BYTES),
        name="out_ffn",
    )(x2d, oa, ob, p1, p2, w_out, g2, w_up, conv_w, conv_b, w_down, gf)


def _rope_tables(pos):
    half = ROT // 2
    inv = jnp.power(ROPE_THETA, -jnp.arange(half, dtype=F32) * (2.0 / ROT))
    ang = pos.astype(F32)[:, None] * inv[None, :]
    cos, sin = jnp.cos(ang), jnp.sin(ang)
    n = pos.shape[0]
    ones = jnp.ones((n, HD_B - ROT), F32)
    zeros = jnp.zeros((n, HD_B - ROT), F32)
    zh = jnp.zeros((n, half), F32)
    c = jnp.concatenate([cos, cos, ones], axis=1)
    sa = jnp.concatenate([zh, sin, zeros], axis=1)
    sb = jnp.concatenate([-sin, zh, zeros], axis=1)
    rep = LANES // HD_B
    return jnp.tile(c, (1, rep)), jnp.tile(sa, (1, rep)), jnp.tile(sb, (1, rep))


def kernel(x_prompt, x_sample, cache_k, cache_v, cache_kidx, state_hgrn, state_conv, page_table,
           norm_mix_gain, w_in, lb_logits, hgrn_norm_gain, w_out, norm_ffn_gain, w_up, conv_w,
           conv_b, w_down, final_norm_gain):
    bp, s, d = x_prompt.shape
    nb, t_new, _ = x_sample.shape
    depth = w_in.shape[0]
    assert bp == 1 and depth == 1
    n_pages = page_table.shape[1]
    past = n_pages * PAGE_SIZE
    d_ff = w_down.shape[1]
    assert s % ATT_TK == 0 and s % HGRN_CHUNK == 0 and n_pages % KV_CHUNK_PAGES == 0
    assert t_new <= SUBLANES and CONV_W - 1 <= t_new

    lower_bounds = jnp.cumsum(jax.nn.softmax(lb_logits.astype(F32), axis=0), axis=0)
    lb = lower_bounds[0][None, :]
    n_in = w_in.shape[2]
    n_pad = 8 * GROUP + LANES
    w_in_bf = jnp.pad(w_in[0], ((0, 0), (0, n_pad - n_in))).astype(BF16)
    w_out_bf = w_out[0].astype(BF16)
    w_up_bf = w_up[0].astype(BF16)
    w_down_bf = w_down[0].astype(BF16)
    g_mix = norm_mix_gain[0][None, :]
    g_ffn = norm_ffn_gain[0][None, :]
    g_fin = final_norm_gain[None, :]
    g_hgrn = hgrn_norm_gain[0][None, :]
    conv_b2 = conv_b[0][None, :]

    xp = x_prompt.reshape(s, d)
    cp, sap, sbp = _rope_tables(jnp.arange(s, dtype=I32))
    qa, ka, lf, ia, ga, qb, kb, vb, qi, kw = _proj_in(xp, g_mix, w_in_bf, lb, cp, sap, sbp, tm=256)

    s0 = jnp.zeros((1, H_A, DK_A, DK_A), F32)
    r3 = lambda a: a.reshape(1, s, GROUP)
    oa_p, st_p = _hgrn(r3(qa), r3(ka), r3(lf), r3(ia), r3(ga), g_hgrn, s0, HGRN_CHUNK, HGRN_SUB)

    nkb = s // ATT_TK
    ki_hi, ki_lo = _split_bf16(kw[:, :D_I])
    kit = jnp.concatenate([ki_hi, ki_hi, ki_lo], axis=1)
    kit_blocks = kit.reshape(nkb, ATT_TK, 3 * D_I).transpose(0, 2, 1)
    kt_blocks = kb.astype(BF16).reshape(nkb, ATT_TK, GROUP).transpose(0, 2, 1)
    ob_p = _attn_prompt(qi, kw, qb, kit_blocks, kt_blocks, vb.astype(BF16), min(TOPK_MAX, s // 4))

    zero_prev = jnp.zeros((SUBLANES, d_ff), F32)
    y_p, u_tail = _out_ffn(xp, oa_p.reshape(s, GROUP), ob_p, zero_prev, zero_prev, w_out_bf, g_ffn, w_up_bf,
                           conv_w[0], conv_b2, w_down_bf, g_fin, tm=256, seq_rows=s, carry_mode=True)

    ms = nb * t_new
    xs = x_sample.reshape(ms, d)
    pos_s = jnp.tile(past + jnp.arange(t_new, dtype=I32), nb)
    cs, sas, sbs = _rope_tables(pos_s)
    qa2, ka2, lf2, ia2, ga2, qb2, kb2, vb2, qi2, kw2 = _proj_in(xs, g_mix, w_in_bf, lb, cs, sas, sbs, tm=ms)

    pad_front = SUBLANES - t_new
    fp = lambda a: jnp.pad(a.reshape(nb, t_new, GROUP), ((0, 0), (pad_front, 0), (0, 0)))
    s0_s = jnp.swapaxes(state_hgrn[0], -1, -2)
    oa_s, st_s = _hgrn(fp(qa2), fp(ka2), fp(lf2), fp(ia2), fp(ga2), g_hgrn, s0_s, SUBLANES, SUBLANES)
    oa_s = oa_s[:, pad_front:, :].reshape(ms, GROUP)

    def rp(a):
        a = a.reshape(nb, t_new, a.shape[-1])
        return jnp.concatenate([a, jnp.broadcast_to(a[:, -1:], (nb, SUBLANES - t_new, a.shape[-1]))], axis=1)
    page_pad = lambda a: jnp.pad(a.reshape(nb, t_new, a.shape[-1]), ((0, 0), (0, PAGE_SIZE - t_new), (0, 0)))
    n_pool = cache_k.shape[1]
    ob_s = _attn_sample(
        page_table, rp(qi2), rp(kw2), rp(qb2), page_pad(kw2[:, :D_I]), page_pad(kb2), page_pad(vb2),
        cache_kidx[0], cache_k[0].reshape(n_pool, PAGE_SIZE, GROUP), cache_v[0].reshape(n_pool, PAGE_SIZE, GROUP),
        t_new, min(TOPK_MAX, (past + t_new) // 4))
    ob_s = ob_s[:, :t_new, :].reshape(ms, GROUP)

    sc0 = state_conv[0]
    zrow = jnp.zeros((nb, t_new - 1, d_ff), F32)
    p1 = jnp.concatenate([sc0[:, 1:2], zrow], axis=1).reshape(ms, d_ff)
    p2 = jnp.concatenate([sc0[:, 0:2], zrow[:, 1:]], axis=1).reshape(ms, d_ff)
    y_s, u_s = _out_ffn(xs, oa_s, ob_s, p1, p2, w_out_bf, g_ffn, w_up_bf, conv_w[0], conv_b2, w_down_bf, g_fin,
                        tm=ms, seq_rows=t_new, carry_mode=False)

    return (
        y_p.reshape(1, s, d),
        y_s.reshape(nb, t_new, d),
        kb.reshape(1, 1, s, H_B, HD_B),
        vb.reshape(1, 1, s, H_B, HD_B),
        kw[:, :D_I].reshape(1, 1, s, D_I),
        jnp.swapaxes(st_p, -1, -2).reshape(1, 1, H_A, DK_A, DK_A),
        u_tail[SUBLANES - (CONV_W - 1):].reshape(1, 1, CONV_W - 1, d_ff),
        kb2.reshape(1, nb, t_new, H_B, HD_B),
        vb2.reshape(1, nb, t_new, H_B, HD_B),
        kw2[:, :D_I].reshape(1, nb, t_new, D_I),
        jnp.swapaxes(st_s, -1, -2).reshape(1, nb, H_A, DK_A, DK_A),
        u_s.reshape(nb, t_new, d_ff)[:, t_new - (CONV_W - 1):].reshape(1, nb, CONV_W - 1, d_ff),
    )
```

```python
import functools

import jax
import jax.numpy as jnp
from jax import lax
from jax.experimental import pallas as pl
from jax.experimental.pallas import tpu as pltpu

F32 = jnp.float32
BF16 = jnp.bfloat16
I32 = jnp.int32

H_A = 4
DK_A = 128
H_B = 8
HD_B = 64
H_I = 8
D_I = 64
ROT = 16
ROPE_THETA = 500000.0
TOPK_MAX = 256
PAGE_SIZE = 128
CONV_W = 3
EPS = 1e-6
NEG_SCORE = -1e30
MASK_VALUE = -1e30
GROUP = 512
LOG2_E = 1.4426950408889634

LANES = 128
SUBLANES = 8
VMEM_LIMIT_BYTES = 60 * 1024 * 1024

HGRN_CHUNK = 64
HGRN_SUB = 16
INTERP_ROUNDS = 5
BISECT_PASSES = 14
ATT_TQ = 128
ATT_TK = 512
KV_CHUNK_PAGES = 8


def _silu(x):
    return x * jax.nn.sigmoid(x)


def _rms(x, g):
    return x * lax.rsqrt(jnp.mean(x * x, axis=-1, keepdims=True) + EPS) * g


def _split_bf16(x):
    hi = x.astype(BF16)
    lo = (x - hi.astype(F32)).astype(BF16)
    return hi, lo


def _rope(x, c, sa, sb):
    w = x.shape[1]
    return x * c + pltpu.roll(x, ROT // 2, 1) * sa + pltpu.roll(x, w - ROT // 2, 1) * sb


def _proj_in_kernel(x_ref, g_ref, w_ref, lb_ref, c_ref, sa_ref, sb_ref,
                    qa_ref, ka_ref, lf_ref, ia_ref, ga_ref, qb_ref, kb_ref, vb_ref, qi_ref, kw_ref, h_ref):
    h_ref[...] = _rms(x_ref[...], g_ref[...]).astype(BF16)

    def grp(i, width=GROUP):
        return jnp.dot(h_ref[...], w_ref[:, i * GROUP:i * GROUP + width], preferred_element_type=F32)

    c = jnp.tile(c_ref[...], (1, GROUP // LANES))
    sa = jnp.tile(sa_ref[...], (1, GROUP // LANES))
    sb = jnp.tile(sb_ref[...], (1, GROUP // LANES))

    qa_ref[...] = _silu(grp(0))
    lb = lb_ref[...]
    fg = lb + (1.0 - lb) * jax.nn.sigmoid(grp(1))
    ka_ref[...] = 1.0 - fg
    lf_ref[...] = jnp.log(fg)
    ia_ref[...] = grp(2)
    ga_ref[...] = _silu(grp(3))
    qb_ref[...] = _rope(grp(4), c, sa, sb)
    kb_ref[...] = _rope(grp(5), c, sa, sb)
    vb_ref[...] = grp(6)
    qi_ref[...] = _rope(grp(7), c, sa, sb)
    kw = grp(8, LANES)
    lane = lax.broadcasted_iota(I32, kw.shape, 1)
    roped = _rope(kw, c_ref[...], sa_ref[...], sb_ref[...])
    kw_ref[...] = jnp.where(lane < D_I, roped, kw * (H_I ** -0.5 * D_I ** -0.5))


def _proj_in(x2d, gain, w_pad, lb, cos_t, sin_a, sin_b, tm):
    m, d = x2d.shape
    n_pad = w_pad.shape[1]
    row = lambda i: (i, 0)
    const = lambda i: (0, 0)
    big = pl.BlockSpec((tm, GROUP), row)
    out_shapes = [jax.ShapeDtypeStruct((m, GROUP), F32)] * 9 + [jax.ShapeDtypeStruct((m, LANES), F32)]
    return pl.pallas_call(
        _proj_in_kernel,
        grid=(m // tm,),
        in_specs=[
            pl.BlockSpec((tm, d), row),
            pl.BlockSpec((1, d), const),
            pl.BlockSpec((d, n_pad), const, pipeline_mode=pl.Buffered(1)),
            pl.BlockSpec((1, GROUP), const),
            pl.BlockSpec((tm, LANES), row),
            pl.BlockSpec((tm, LANES), row),
            pl.BlockSpec((tm, LANES), row),
        ],
        out_specs=[big] * 9 + [pl.BlockSpec((tm, LANES), row)],
        out_shape=out_shapes,
        scratch_shapes=[pltpu.VMEM((tm, d), BF16)],
        compiler_params=pltpu.CompilerParams(
            dimension_semantics=("arbitrary",), vmem_limit_bytes=VMEM_LIMIT_BYTES),
        name="proj_in",
    )(x2d, gain, w_pad, lb, cos_t, sin_a, sin_b)


def _shift_rows(x, d):
    if d == 0:
        return x
    return pltpu.roll(x, d, 0)


def _cumsum_rows(g):
    c = g.shape[0]
    row = lax.broadcasted_iota(I32, g.shape, 0)
    k = 1
    while k < c:
        g = g + jnp.where(row >= k, _shift_rows(g, k), 0.0)
        k *= 2
    return g


def _hgrn_kernel(q_ref, k_ref, lf_ref, v_ref, ga_ref, gain_ref, s0_ref, o_ref, sfin_ref, st_ref, *, chunk, sub):
    ci = pl.program_id(1)

    @pl.when(ci == 0)
    def _():
        st_ref[...] = s0_ref[0]

    nsb = chunk // sub
    row = lax.broadcasted_iota(I32, (chunk, DK_A), 0)
    row_in_sub = row & (sub - 1)
    gain = gain_ref[...]

    for h in range(H_A):
        sl = slice(h * DK_A, (h + 1) * DK_A)
        q = q_ref[0, :, sl]
        k = k_ref[0, :, sl]
        v = v_ref[0, :, sl]
        b = _cumsum_rows(lf_ref[0, :, sl])
        st = st_ref[h]

        refs = [jnp.zeros((1, DK_A), F32)] + [b[i * sub - 1:i * sub, :] for i in range(1, nsb)]
        ref_rows = jnp.concatenate([jnp.broadcast_to(r, (sub, DK_A)) for r in refs], axis=0)
        q_rel = q * jnp.exp(b - ref_rows)

        o = lax.dot_general((q * jnp.exp(b)).astype(BF16), st.astype(BF16),
                            (((1,), (1,)), ((), ())), preferred_element_type=F32)

        off_rows = [jnp.zeros((sub, DK_A), F32)]
        for i in range(1, nsb):
            n_prev = i * sub
            k_rel = (k[:n_prev] * jnp.exp(refs[i] - b[:n_prev])).astype(BF16)
            att = lax.dot_general(q_rel[n_prev:n_prev + sub].astype(BF16), k_rel,
                                  (((1,), (1,)), ((), ())), preferred_element_type=F32)
            off_rows.append(jnp.dot(att.astype(BF16), v[:n_prev].astype(BF16), preferred_element_type=F32))
        if nsb > 1:
            o = o + jnp.concatenate(off_rows, axis=0)

        for d in range(sub):
            valid = row_in_sub >= d
            e = jnp.exp(jnp.where(valid, b - _shift_rows(b, d), 0.0))
            w = jnp.sum(q * _shift_rows(k, d) * e, axis=1, keepdims=True)
            o = o + jnp.where(valid, w * _shift_rows(v, d), 0.0)

        b_last = b[chunk - 1:chunk, :]
        k_dec = (k * jnp.exp(b_last - b)).astype(BF16)
        st_ref[h] = st * jnp.exp(b_last) + lax.dot_general(
            v.astype(BF16), k_dec, (((0,), (0,)), ((), ())), preferred_element_type=F32)

        o_ref[0, :, sl] = _rms(o, gain) * ga_ref[0, :, sl]

    @pl.when(ci == pl.num_programs(1) - 1)
    def _():
        sfin_ref[0] = st_ref[...]


def _hgrn(q, k, lf, v, ga, gain, s0_t, chunk, sub):
    b, t, w = q.shape
    blk = pl.BlockSpec((1, chunk, w), lambda bi, ci: (bi, ci, 0))
    st_spec = pl.BlockSpec((1, H_A, DK_A, DK_A), lambda bi, ci: (bi, 0, 0, 0))
    return pl.pallas_call(
        functools.partial(_hgrn_kernel, chunk=chunk, sub=sub),
        grid=(b, t // chunk),
        in_specs=[blk, blk, blk, blk, blk, pl.BlockSpec((1, DK_A), lambda bi, ci: (0, 0)), st_spec],
        out_specs=[blk, st_spec],
        out_shape=[jax.ShapeDtypeStruct((b, t, w), F32), jax.ShapeDtypeStruct((b, H_A, DK_A, DK_A), F32)],
        scratch_shapes=[pltpu.VMEM((H_A, DK_A, DK_A), F32)],
        compiler_params=pltpu.CompilerParams(
            dimension_semantics=("arbitrary", "arbitrary"), vmem_limit_bytes=VMEM_LIMIT_BYTES),
        name="hgrn",
    )(q, k, lf, v, ga, gain, s0_t)


def _rep_sum(x):
    return jnp.broadcast_to(jnp.sum(x, axis=1, keepdims=True), x.shape)


def _rep_max(x):
    return jnp.broadcast_to(jnp.max(x, axis=1, keepdims=True), x.shape)


def _rep_min(x):
    return jnp.broadcast_to(jnp.min(x, axis=1, keepdims=True), x.shape)


def _any_true(flag):
    return jnp.max(flag) > 0.5


def _lane_tiles(x):
    return [x[:, u * LANES:(u + 1) * LANES] for u in range(x.shape[1] // LANES)]


def _select_rows(sc_ref, nkb, n_hidden_tail, n_sel):
    _, r, w = sc_ref.shape
    shape = (r, LANES)
    neg = jnp.float32(NEG_SCORE)
    inf = jnp.float32(jnp.inf)
    kf = jnp.float32(n_sel)
    tail = n_hidden_tail.astype(F32)
    zeros = jnp.zeros(shape, F32)
    lane = lax.broadcasted_iota(I32, shape, 1)

    def count_ge(c):
        def body(j, acc):
            for xu in _lane_tiles(sc_ref[j]):
                acc = acc + jnp.where(xu >= c, 1.0, 0.0)
            return acc
        return _rep_sum(lax.fori_loop(0, nkb, body, zeros)) + jnp.where(neg >= c, tail, 0.0)

    def stats_body(j, carry):
        vmax, vmin_real, n_real = carry
        for x in _lane_tiles(sc_ref[j]):
            real = x > neg
            vmax = jnp.maximum(vmax, x)
            vmin_real = jnp.minimum(vmin_real, jnp.where(real, x, inf))
            n_real = n_real + jnp.where(real, 1.0, 0.0)
        return vmax, vmin_real, n_real

    def hidden_stats_body(j, carry):
        n_ge_neg, gmin = carry
        for x in _lane_tiles(sc_ref[j]):
            n_ge_neg = n_ge_neg + jnp.where(x >= neg, 1.0, 0.0)
            gmin = jnp.minimum(gmin, jnp.where(x > -inf, x, inf))
        return n_ge_neg, gmin

    vmax, vmin_real, n_real = lax.fori_loop(0, nkb, stats_body, (zeros - inf, zeros + inf, zeros))
    few = _any_true(jnp.where(_rep_sum(n_real) < kf, 1.0, 0.0))
    n_ge_neg, gmin = lax.fori_loop(0, jnp.where(few, nkb, 0), hidden_stats_body, (zeros, zeros + inf))
    has_tail = tail > 0.5
    vmax = _rep_max(vmax)
    vmax = jnp.where(has_tail, jnp.maximum(vmax, neg), vmax)
    vmin_real = _rep_min(vmin_real)
    gmin = _rep_min(gmin)
    gmin = jnp.where(has_tail, jnp.minimum(gmin, neg), gmin)
    n_real = _rep_sum(n_real)
    n_ge_neg = _rep_sum(n_ge_neg) + tail
    c_max = count_ge(vmax)

    at_max = c_max >= kf
    few_real = jnp.logical_and(jnp.logical_not(at_max), n_real < kf)
    at_neg = jnp.logical_and(few_real, n_ge_neg >= kf)
    below_neg = jnp.logical_and(few_real, n_ge_neg < kf)

    done = jnp.where(jnp.logical_or(at_max, at_neg), 1.0, 0.0)
    thr = jnp.where(at_max, vmax, jnp.where(at_neg, neg, zeros))
    n_gt = jnp.where(at_neg, n_real, zeros)
    n_ge = jnp.where(at_max, c_max, jnp.where(at_neg, n_ge_neg, zeros))
    tie = jnp.where(jnp.logical_and(done > 0.5, n_ge > kf), 1.0, 0.0)
    lo = jnp.where(below_neg, gmin, vmin_real)
    hi = jnp.where(below_neg, neg, vmax)
    c_hi = jnp.where(below_neg, n_ge_neg, c_max)
    c_lo = jnp.where(below_neg, zeros, n_real)

    def probe(st, frac):
        done, thr, n_gt, tie, lo, hi, c_hi, c_lo = st
        mid = lo + (hi - lo) * frac
        c = count_ge(mid)
        live = done < 0.5
        hit = jnp.logical_and(live, c == kf)
        up = jnp.logical_and(live, c > kf)
        dn = jnp.logical_and(live, c < kf)
        return (jnp.where(hit, 1.0, done), jnp.where(hit, mid, thr), n_gt, tie,
                jnp.where(up, mid, lo), jnp.where(dn, mid, hi), jnp.where(dn, c, c_hi), jnp.where(up, c, c_lo))

    def bisect(st):
        return probe(st, 0.5)

    def interpolate(st):
        c_hi, c_lo = st[6], st[7]
        log_lo = jnp.log(jnp.maximum(c_lo, 1.0))
        log_hi = jnp.log(jnp.maximum(c_hi, 0.5))
        frac = (log_lo - jnp.log(kf)) / jnp.maximum(log_lo - log_hi, 1e-6)
        frac = jnp.where(c_lo > 0.5, jnp.clip(frac, 0.03, 0.97), 0.5)
        return probe(st, frac)

    def snap(st):
        done, thr, n_gt, tie, lo, hi, c_hi, c_lo = st

        def body(j, m):
            for xu in _lane_tiles(sc_ref[j]):
                m = jnp.maximum(m, jnp.where(xu < hi, xu, -inf))
            return m
        below = _rep_max(lax.fori_loop(0, nkb, body, zeros - inf))
        below = jnp.where(jnp.logical_and(tail > 0.5, neg < hi), jnp.maximum(below, neg), below)
        c = count_ge(below)
        live = done < 0.5
        fin = jnp.logical_and(live, c >= kf)
        mv = jnp.logical_and(live, c < kf)
        return (jnp.where(fin, 1.0, done), jnp.where(fin, below, thr), jnp.where(fin, c_hi, n_gt),
                jnp.where(fin, jnp.where(c > kf, 1.0, 0.0), tie),
                lo, jnp.where(mv, below, hi), jnp.where(mv, c, c_hi), c_lo)

    def not_done(st):
        return _any_true(1.0 - st[0])

    st = (done, thr, n_gt, tie, lo, hi, c_hi, c_lo)
    _, st = lax.while_loop(
        lambda ps: jnp.logical_and(ps[0] < INTERP_ROUNDS, not_done(ps[1])),
        lambda ps: (ps[0] + 1, bisect(interpolate(ps[1]))), (jnp.int32(0), st))
    _, st = lax.while_loop(
        lambda ps: jnp.logical_and(ps[0] < BISECT_PASSES, not_done(ps[1])),
        lambda ps: (ps[0] + 1, bisect(ps[1])), (jnp.int32(0), st))
    st = lax.while_loop(not_done, lambda s: snap(bisect(s)), st)
    done, thr, n_gt, tie, lo, hi, c_hi, c_lo = st

    need = kf - n_gt
    n_stored = nkb * w

    def count_eq_upto(jb):
        def body(j, acc):
            for u, xu in enumerate(_lane_tiles(sc_ref[j])):
                idx = j * w + u * LANES + lane
                acc = acc + jnp.where(jnp.logical_and(xu == thr, idx <= jb), 1.0, 0.0)
            return acc
        return _rep_sum(lax.fori_loop(0, nkb, body, zeros))

    def jstep(pj):
        p, (jlo, jhi) = pj
        jmid = jlo + lax.shift_right_arithmetic(jhi - jlo, 1)
        ok = count_eq_upto(jmid) >= need
        return p + 1, (jnp.where(ok, jlo, jmid), jnp.where(ok, jmid, jhi))

    n_jpass = jnp.where(_any_true(tie), 16, 0)
    jlo0 = jnp.full(shape, -1, I32)
    jhi0 = jnp.zeros(shape, I32) + n_stored
    _, (_, jhi) = lax.while_loop(lambda pj: pj[0] < n_jpass, jstep, (jnp.int32(0), (jlo0, jhi0)))
    jmax = jnp.where(tie > 0.5, jhi, jnp.int32(2 ** 30))
    return thr, jmax


def _attn_prompt_kernel(qi_ref, kw_ref, qb_ref, kit_ref, kt_ref, v_ref, o_ref,
                        sc_ref, wb_ref, qc_ref, qh_ref, sa_ref, sb_ref, mxa_ref, mxb_ref, ba_ref, bb_ref, p_ref,
                        m_ref, l_ref, acc_ref,
                        *, n_keys, n_sel):
    tq, tk = ATT_TQ, ATT_TK
    tiles = [slice(u * LANES, (u + 1) * LANES) for u in range(tk // LANES)]
    i = pl.program_id(0)
    nkb = (i * tq + tq + tk - 1) // tk
    last_blk = n_keys // tk - 1
    q_pos = i * tq + lax.broadcasted_iota(I32, (tq, LANES), 0)
    lane = lax.broadcasted_iota(I32, (tq, LANES), 1)

    @pl.when(i == 0)
    def _():
        sc_ref[...] = jnp.full(sc_ref.shape, NEG_SCORE, F32)

    kw = kw_ref[...]
    for h in range(H_I):
        hi, lo = _split_bf16(qi_ref[:, h * D_I:(h + 1) * D_I])
        qc_ref[h] = jnp.concatenate([hi, lo, hi], axis=1)
        wb_ref[h] = jnp.broadcast_to(kw[:, D_I + h:D_I + h + 1], (tq, LANES))

    def score_body(j, _):
        kit = kit_ref[j]
        for h in range(H_I):
            d = jnp.dot(qc_ref[h], kit, preferred_element_type=F32)
            wbh = wb_ref[h]
            for u, us in enumerate(tiles):
                val = wbh * jnp.maximum(d[:, us], 0.0)
                if h > 0:
                    val = sc_ref[j, :, us] + val
                if h == H_I - 1:
                    val = jnp.where(j * tk + u * LANES + lane <= q_pos, val, NEG_SCORE)
                sc_ref[j, :, us] = val
        return 0

    lax.fori_loop(0, nkb, score_body, 0)

    tail = jnp.zeros((tq, LANES), I32) + (n_keys - nkb * tk)
    thr, jmax = _select_rows(sc_ref, nkb, tail, n_sel)

    m_ref[...] = jnp.full(m_ref.shape, MASK_VALUE, F32)
    l_ref[...] = jnp.zeros(l_ref.shape, F32)
    acc_ref[...] = jnp.zeros(acc_ref.shape, F32)
    for h in range(H_B):
        qh_ref[h] = (qb_ref[:, h * HD_B:(h + 1) * HD_B] * (HD_B ** -0.5 * LOG2_E)).astype(BF16)
    first_half = lane < HD_B

    def selection_bias(blk, b_ref):
        for u, us in enumerate(tiles):
            xu = sc_ref[blk, :, us]
            idx = blk * tk + u * LANES + lane
            sel = jnp.logical_or(xu > thr, jnp.logical_and(xu == thr, idx <= jmax))
            sel = jnp.logical_and(sel, idx <= q_pos)
            b_ref[:, us] = jnp.where(sel, 0.0, MASK_VALUE)

    def masked_logits(blk, h, s_ref, mx_ref, b_ref):
        s = jnp.dot(qh_ref[h], kt_ref[blk, h * HD_B:(h + 1) * HD_B, :], preferred_element_type=F32) + b_ref[...]
        s_ref[h] = s
        mx = s[:, tiles[0]]
        for us in tiles[1:]:
            mx = jnp.maximum(mx, s[:, us])
        mx_ref[h] = mx

    def reduce_block(blk, s_ref, mx_ref, nxt, b_ref):
        rows = pl.ds(pl.multiple_of(blk * tk, tk), tk)
        for pair in range(H_B // 2):
            ps = slice(pair * LANES, (pair + 1) * LANES)
            v_pair = v_ref[rows, ps]
            for half in range(2):
                h = 2 * pair + half
                m_old = m_ref[h]
                m_new = jnp.maximum(m_old, _rep_max(mx_ref[h]))
                alpha = jnp.exp2(m_old - m_new)
                acc_ref[h] = alpha * acc_ref[h]
                l_new = alpha * l_ref[h]
                for us in tiles:
                    pt = jnp.exp2(s_ref[h, :, us] - m_new)
                    l_new = l_new + pt
                    p_ref[h, :, us] = pt.astype(BF16)
                l_ref[h] = l_new
                m_ref[h] = m_new
                masked_logits(nxt, h, s_ref, mx_ref, b_ref)
                acc_ref[h] += jnp.dot(p_ref[h], v_pair, preferred_element_type=F32)

    selection_bias(0, ba_ref)
    selection_bias(1, bb_ref)
    for h in range(H_B):
        masked_logits(0, h, sa_ref, mxa_ref, ba_ref)
        masked_logits(1, h, sb_ref, mxb_ref, bb_ref)

    def pair_body(jj, _):
        a = 2 * jj
        nxt_a = jnp.minimum(a + 2, last_blk)
        nxt_b = jnp.minimum(a + 3, last_blk)
        selection_bias(nxt_a, ba_ref)
        selection_bias(nxt_b, bb_ref)
        reduce_block(a, sa_ref, mxa_ref, nxt_a, ba_ref)
        reduce_block(a + 1, sb_ref, mxb_ref, nxt_b, bb_ref)
        return 0

    lax.fori_loop(0, (nkb + 1) // 2, pair_body, 0)

    for pair in range(H_B // 2):
        ps = slice(pair * LANES, (pair + 1) * LANES)
        d0 = jnp.broadcast_to(jnp.sum(l_ref[2 * pair], axis=1, keepdims=True), (tq, LANES))
        d1 = jnp.broadcast_to(jnp.sum(l_ref[2 * pair + 1], axis=1, keepdims=True), (tq, LANES))
        o_ref[:, ps] = jnp.where(first_half, acc_ref[2 * pair] / d0, acc_ref[2 * pair + 1] / d1)


def _attn_prompt(qi, kw, qb, kit_blocks, kt_blocks, v_bf, n_sel):
    s = qi.shape[0]
    nkb_total = s // ATT_TK
    row = lambda i: (i, 0)
    whole = pl.BlockSpec(memory_space=pltpu.VMEM)
    return pl.pallas_call(
        functools.partial(_attn_prompt_kernel, n_keys=s, n_sel=n_sel),
        grid=(s // ATT_TQ,),
        in_specs=[
            pl.BlockSpec((ATT_TQ, GROUP), row),
            pl.BlockSpec((ATT_TQ, LANES), row),
            pl.BlockSpec((ATT_TQ, GROUP), row),
            whole, whole, whole,
        ],
        out_specs=pl.BlockSpec((ATT_TQ, GROUP), row),
        out_shape=jax.ShapeDtypeStruct((s, GROUP), F32),
        scratch_shapes=[
            pltpu.VMEM((nkb_total, ATT_TQ, ATT_TK), F32),
            pltpu.VMEM((H_I, ATT_TQ, LANES), F32),
            pltpu.VMEM((H_I, ATT_TQ, 3 * D_I), BF16),
            pltpu.VMEM((H_B, ATT_TQ, HD_B), BF16),
            pltpu.VMEM((H_B, ATT_TQ, ATT_TK), F32),
            pltpu.VMEM((H_B, ATT_TQ, ATT_TK), F32),
            pltpu.VMEM((H_B, ATT_TQ, LANES), F32),
            pltpu.VMEM((H_B, ATT_TQ, LANES), F32),
            pltpu.VMEM((ATT_TQ, ATT_TK), F32),
            pltpu.VMEM((ATT_TQ, ATT_TK), F32),
            pltpu.VMEM((H_B, ATT_TQ, ATT_TK), BF16),
            pltpu.VMEM((H_B, ATT_TQ, LANES), F32),
            pltpu.VMEM((H_B, ATT_TQ, LANES), F32),
            pltpu.VMEM((H_B, ATT_TQ, LANES), F32),
        ],
        compiler_params=pltpu.CompilerParams(
            dimension_semantics=("arbitrary",), vmem_limit_bytes=VMEM_LIMIT_BYTES),
        name="attn_prompt",
    )(qi, kw, qb, kit_blocks, kt_blocks, v_bf)


def _attn_sample_kernel(pt_ref, qi_ref, kw_ref, qb_ref, kin_ref, kn_ref, vn_ref,
                        ckidx_hbm, ck_hbm, cv_hbm, o_ref,
                        kid_buf, k_buf, v_buf, sc_ref, m_ref, l_ref, acc_ref, sem_i, sem_k, sem_v,
                        *, n_pages, t_new, n_sel):
    b = pl.program_id(0)
    r = SUBLANES
    nrow = H_B * r
    n_chunks = n_pages // KV_CHUNK_PAGES

    def kidx_copy(p):
        return pltpu.make_async_copy(ckidx_hbm.at[pt_ref[b, p]], kid_buf.at[p], sem_i)

    def kv_copies(c, slot, p):
        page = pt_ref[b, c * KV_CHUNK_PAGES + p]
        return (pltpu.make_async_copy(ck_hbm.at[page], k_buf.at[slot, p], sem_k.at[slot]),
                pltpu.make_async_copy(cv_hbm.at[page], v_buf.at[slot, p], sem_v.at[slot]))

    def start_chunk(c, slot):
        for p in range(KV_CHUNK_PAGES):
            ck, cv = kv_copies(c, slot, p)
            ck.start()
            cv.start()

    def wait_chunk(c, slot):
        for p in range(KV_CHUNK_PAGES):
            ck, cv = kv_copies(c, slot, p)
            ck.wait()
            cv.wait()

    def start_kidx(p, _):
        kidx_copy(p).start()
        return 0

    def wait_kidx(p, _):
        kidx_copy(p).wait()
        return 0

    lax.fori_loop(0, n_pages, start_kidx, 0)
    start_chunk(0, 0)

    kw = kw_ref[0]
    qi = qi_ref[0]
    q_hi, q_lo, w_rows = [], [], []
    for h in range(H_I):
        hi, lo = _split_bf16(qi[:, h * D_I:(h + 1) * D_I])
        q_hi.append(hi)
        q_lo.append(lo)
        w_rows.append(jnp.broadcast_to(kw[:, D_I + h:D_I + h + 1], (r, LANES)))
    q_hi = jnp.concatenate(q_hi, axis=0)
    q_hl = jnp.concatenate([q_hi, jnp.concatenate(q_lo, axis=0)], axis=0)
    w_rows = jnp.concatenate(w_rows, axis=0)
    nt = (((1,), (1,)), ((), ()))

    def key_scores(keys):
        k_hi, k_lo = _split_bf16(keys)
        d2 = lax.dot_general(q_hl, k_hi, nt, preferred_element_type=F32)
        d = d2[:H_I * r] + d2[H_I * r:] + lax.dot_general(q_hi, k_lo, nt, preferred_element_type=F32)
        out = []
        for u in range(keys.shape[0] // LANES):
            us = slice(u * LANES, (u + 1) * LANES)
            wd = w_rows * jnp.maximum(d[:, us], 0.0)
            acc = wd[0:r]
            for h in range(1, H_I):
                acc = acc + wd[h * r:(h + 1) * r]
            out.append(acc)
        return out

    lax.fori_loop(0, n_pages, wait_kidx, 0)

    def score_body(c, _):
        first = c * KV_CHUNK_PAGES
        keys = kid_buf[pl.ds(first, KV_CHUNK_PAGES)].reshape(KV_CHUNK_PAGES * PAGE_SIZE, D_I)
        for u, tile in enumerate(key_scores(keys)):
            sc_ref[first + u] = tile
        return 0

    lax.fori_loop(0, n_chunks, score_body, 0)

    lane = lax.broadcasted_iota(I32, (r, LANES), 1)
    t_row = jnp.minimum(lax.broadcasted_iota(I32, (r, LANES), 0), t_new - 1)
    new_sc = key_scores(kin_ref[0])[0]
    new_vis = lane <= t_row
    sc_ref[n_pages] = jnp.where(lane >= t_new, -jnp.inf, jnp.where(new_vis, new_sc, NEG_SCORE))

    nkb = n_pages + 1
    thr, jmax = _select_rows(sc_ref, nkb, jnp.zeros((r, LANES), I32), n_sel)

    qb = qb_ref[0] * (HD_B ** -0.5)
    col_head = lax.shift_right_logical(lax.broadcasted_iota(I32, (r, GROUP), 1), HD_B.bit_length() - 1)
    q_bd = jnp.concatenate([jnp.where(col_head == h, qb, 0.0) for h in range(H_B)], axis=0).astype(BF16)

    m_ref[...] = jnp.full(m_ref.shape, MASK_VALUE, F32)
    l_ref[...] = jnp.zeros(l_ref.shape, F32)
    acc_ref[...] = jnp.zeros(acc_ref.shape, F32)

    def attend(first_blk, n_blk, kmat, vmat, extra_mask):
        bias = []
        for u in range(n_blk):
            x = sc_ref[first_blk + u]
            idx = (first_blk + u) * LANES + lane
            sel = jnp.logical_or(x > thr, jnp.logical_and(x == thr, idx <= jmax))
            if extra_mask is not None:
                sel = jnp.logical_and(sel, extra_mask)
            bias.append(jnp.where(sel, 0.0, MASK_VALUE))
        bias = jnp.tile(jnp.concatenate(bias, axis=1), (H_B, 1))
        s = lax.dot_general(q_bd, kmat.astype(BF16), nt, preferred_element_type=F32) + bias
        m_old = m_ref[...]
        m_new = jnp.maximum(m_old, jnp.broadcast_to(jnp.max(s, axis=1, keepdims=True), m_old.shape))
        alpha = jnp.exp(m_old - m_new)
        p = jnp.exp(s - jnp.tile(m_new, (1, n_blk)))
        p_sum = p[:, :LANES]
        for u in range(1, n_blk):
            p_sum = p_sum + p[:, u * LANES:(u + 1) * LANES]
        l_ref[...] = alpha * l_ref[...] + p_sum
        pv = jnp.dot(p.astype(BF16), vmat.astype(BF16), preferred_element_type=F32)
        acc_ref[...] = jnp.tile(alpha, (1, GROUP // LANES)) * acc_ref[...] + pv
        m_ref[...] = m_new

    def chunk_body(c, _):
        slot = c % 2
        wait_chunk(c, slot)

        @pl.when(c + 1 < n_chunks)
        def _():
            start_chunk(c + 1, 1 - slot)

        n_keys_chunk = KV_CHUNK_PAGES * PAGE_SIZE
        attend(c * KV_CHUNK_PAGES, KV_CHUNK_PAGES, k_buf[slot].reshape(n_keys_chunk, GROUP),
               v_buf[slot].reshape(n_keys_chunk, GROUP), None)
        return 0

    lax.fori_loop(0, n_chunks, chunk_body, 0)
    attend(n_pages, 1, kn_ref[0], vn_ref[0], new_vis)

    out = jnp.zeros((r, GROUP), F32)
    for h in range(H_B):
        rs = slice(h * r, (h + 1) * r)
        denom = jnp.sum(l_ref[rs, :], axis=1, keepdims=True)
        out = out + jnp.where(col_head == h, acc_ref[rs, :] / denom, 0.0)
    o_ref[0] = out


def _attn_sample(page_table, qi, kw, qb, ki_new, k_new, v_new, cache_kidx, cache_k, cache_v, t_new, n_sel):
    nb, n_pages = page_table.shape
    r = SUBLANES
    nrow = H_B * r
    per_b = lambda b, pt: (b, 0, 0)
    anyspec = pl.BlockSpec(memory_space=pl.ANY)
    grid_spec = pltpu.PrefetchScalarGridSpec(
        num_scalar_prefetch=1,
        grid=(nb,),
        in_specs=[
            pl.BlockSpec((1, r, GROUP), per_b),
            pl.BlockSpec((1, r, LANES), per_b),
            pl.BlockSpec((1, r, GROUP), per_b),
            pl.BlockSpec((1, PAGE_SIZE, D_I), per_b),
            pl.BlockSpec((1, PAGE_SIZE, GROUP), per_b),
            pl.BlockSpec((1, PAGE_SIZE, GROUP), per_b),
            anyspec, anyspec, anyspec,
        ],
        out_specs=pl.BlockSpec((1, r, GROUP), per_b),
        scratch_shapes=[
            pltpu.VMEM((n_pages, PAGE_SIZE, D_I), F32),
            pltpu.VMEM((2, KV_CHUNK_PAGES, PAGE_SIZE, GROUP), F32),
            pltpu.VMEM((2, KV_CHUNK_PAGES, PAGE_SIZE, GROUP), F32),
            pltpu.VMEM((n_pages + 1, r, LANES), F32),
            pltpu.VMEM((nrow, LANES), F32),
            pltpu.VMEM((nrow, LANES), F32),
            pltpu.VMEM((nrow, GROUP), F32),
            pltpu.SemaphoreType.DMA(()),
            pltpu.SemaphoreType.DMA((2,)),
            pltpu.SemaphoreType.DMA((2,)),
        ],
    )
    return pl.pallas_call(
        functools.partial(_attn_sample_kernel, n_pages=n_pages, t_new=t_new, n_sel=n_sel),
        grid_spec=grid_spec,
        out_shape=jax.ShapeDtypeStruct((nb, r, GROUP), F32),
        compiler_params=pltpu.CompilerParams(
            dimension_semantics=("arbitrary",), vmem_limit_bytes=VMEM_LIMIT_BYTES),
        name="attn_sample",
    )(page_table, qi, kw, qb, ki_new, k_new, v_new, cache_kidx, cache_k, cache_v)


def _out_ffn_kernel(x_ref, oa_ref, ob_ref, p1_ref, p2_ref, wo_ref, g2_ref, wu_ref, cw_ref, cb_ref, wd_ref, gf_ref,
                    y_ref, u_ref, carry_ref, x1_ref, h2_ref, *, d_ff, ff_tile, seq_rows, carry_mode):
    tm = x_ref.shape[0]
    i = pl.program_id(0)
    mixed = jnp.concatenate([oa_ref[...], ob_ref[...]], axis=1).astype(BF16)
    x1 = x_ref[...] + jnp.dot(mixed, wo_ref[...], preferred_element_type=F32)
    h2_ref[...] = _rms(x1, g2_ref[...]).astype(BF16)
    x1_ref[...] = x1

    row = lax.broadcasted_iota(I32, (tm, ff_tile), 0)
    if carry_mode:
        @pl.when(i == 0)
        def _():
            carry_ref[...] = jnp.zeros(carry_ref.shape, F32)
        t_in_seq = row
    else:
        t_in_seq = row & (seq_rows - 1)

    for c in range(d_ff // ff_tile):
        cs = slice(c * ff_tile, (c + 1) * ff_tile)
        h2 = h2_ref[...]
        u = jnp.dot(h2, wu_ref[:, cs], preferred_element_type=F32)
        v = jnp.dot(h2, wu_ref[:, d_ff + c * ff_tile:d_ff + (c + 1) * ff_tile], preferred_element_type=F32)
        if carry_mode:
            prev = carry_ref[:, cs]
            p1 = jnp.broadcast_to(prev[SUBLANES - 1:SUBLANES], (tm, ff_tile))
            p2 = jnp.where(row == 0, jnp.broadcast_to(prev[SUBLANES - 2:SUBLANES - 1], (tm, ff_tile)), p1)
            carry_ref[:, cs] = u[tm - SUBLANES:]
            u_ref[:, cs] = u[tm - SUBLANES:]
        else:
            p1 = p1_ref[:, cs]
            p2 = p2_ref[:, cs]
            u_ref[:, cs] = u
        u1 = jnp.where(t_in_seq >= 1, pltpu.roll(u, 1, 0), p1)
        u2 = jnp.where(t_in_seq >= 2, pltpu.roll(u, 2, 0), p2)
        conv = cb_ref[:, cs] + cw_ref[0:1, cs] * u2 + cw_ref[1:2, cs] * u1 + cw_ref[2:3, cs] * u
        gate = (_silu(conv) * v).astype(BF16)
        x1_ref[...] += jnp.dot(gate, wd_ref[cs, :], preferred_element_type=F32)

    y_ref[...] = _rms(x1_ref[...], gf_ref[...])


def _out_ffn(x2d, oa, ob, p1, p2, w_out, g2, w_up, conv_w, conv_b, w_down, gf, tm, seq_rows, carry_mode):
    m, d = x2d.shape
    d_ff = w_down.shape[0]
    ff_tile = 256
    row = lambda i: (i, 0)
    const = lambda i: (0, 0)
    if carry_mode:
        prev_spec = pl.BlockSpec((SUBLANES, d_ff), const)
        u_spec = pl.BlockSpec((SUBLANES, d_ff), const)
        u_shape = jax.ShapeDtypeStruct((SUBLANES, d_ff), F32)
    else:
        prev_spec = pl.BlockSpec((tm, d_ff), row)
        u_spec = pl.BlockSpec((tm, d_ff), row)
        u_shape = jax.ShapeDtypeStruct((m, d_ff), F32)
    return pl.pallas_call(
        functools.partial(_out_ffn_kernel, d_ff=d_ff, ff_tile=ff_tile, seq_rows=seq_rows, carry_mode=carry_mode),
        grid=(m // tm,),
        in_specs=[
            pl.BlockSpec((tm, d), row),
            pl.BlockSpec((tm, GROUP), row),
            pl.BlockSpec((tm, GROUP), row),
            prev_spec, prev_spec,
            pl.BlockSpec(w_out.shape, const, pipeline_mode=pl.Buffered(1)),
            pl.BlockSpec((1, d), const),
            pl.BlockSpec(w_up.shape, const, pipeline_mode=pl.Buffered(1)),
            pl.BlockSpec(conv_w.shape, const),
            pl.BlockSpec((1, d_ff), const),
            pl.BlockSpec(w_down.shape, const, pipeline_mode=pl.Buffered(1)),
            pl.BlockSpec((1, d), const),
        ],
        out_specs=[pl.BlockSpec((tm, d), row), u_spec],
        out_shape=[jax.ShapeDtypeStruct((m, d), F32), u_shape],
        scratch_shapes=[pltpu.VMEM((SUBLANES, d_ff), F32), pltpu.VMEM((tm, d), F32), pltpu.VMEM((tm, d), BF16)],
        compiler_params=pltpu.CompilerParams(
            dimension_semantics=("arbitrary",), vmem_limit_bytes=VMEM_LIMIT_BYTES),
        name="out_ffn",
    )(x2d, oa, ob, p1, p2, w_out, g2, w_up, conv_w, conv_b, w_down, gf)


def _rope_tables(pos):
    half = ROT // 2
    inv = jnp.power(ROPE_THETA, -jnp.arange(half, dtype=F32) * (2.0 / ROT))
    ang = pos.astype(F32)[:, None] * inv[None, :]
    cos, sin = jnp.cos(ang), jnp.sin(ang)
    n = pos.shape[0]
    ones = jnp.ones((n, HD_B - ROT), F32)
    zeros = jnp.zeros((n, HD_B - ROT), F32)
    zh = jnp.zeros((n, half), F32)
    c = jnp.concatenate([cos, cos, ones], axis=1)
    sa = jnp.concatenate([zh, sin, zeros], axis=1)
    sb = jnp.concatenate([-sin, zh, zeros], axis=1)
    rep = LANES // HD_B
    return jnp.tile(c, (1, rep)), jnp.tile(sa, (1, rep)), jnp.tile(sb, (1, rep))


def kernel(x_prompt, x_sample, cache_k, cache_v, cache_kidx, state_hgrn, state_conv, page_table,
           norm_mix_gain, w_in, lb_logits, hgrn_norm_gain, w_out, norm_ffn_gain, w_up, conv_w,
           conv_b, w_down, final_norm_gain):
    bp, s, d = x_prompt.shape
    nb, t_new, _ = x_sample.shape
    depth = w_in.shape[0]
    assert bp == 1 and depth == 1
    n_pages = page_table.shape[1]
    past = n_pages * PAGE_SIZE
    d_ff = w_down.shape[1]
    assert s % ATT_TK == 0 and s % HGRN_CHUNK == 0 and n_pages % KV_CHUNK_PAGES == 0
    assert t_new <= SUBLANES and CONV_W - 1 <= t_new

    lower_bounds = jnp.cumsum(jax.nn.softmax(lb_logits.astype(F32), axis=0), axis=0)
    lb = lower_bounds[0][None, :]
    n_in = w_in.shape[2]
    n_pad = 8 * GROUP + LANES
    w_in_bf = jnp.pad(w_in[0], ((0, 0), (0, n_pad - n_in))).astype(BF16)
    w_out_bf = w_out[0].astype(BF16)
    w_up_bf = w_up[0].astype(BF16)
    w_down_bf = w_down[0].astype(BF16)
    g_mix = norm_mix_gain[0][None, :]
    g_ffn = norm_ffn_gain[0][None, :]
    g_fin = final_norm_gain[None, :]
    g_hgrn = hgrn_norm_gain[0][None, :]
    conv_b2 = conv_b[0][None, :]

    xp = x_prompt.reshape(s, d)
    cp, sap, sbp = _rope_tables(jnp.arange(s, dtype=I32))
    qa, ka, lf, ia, ga, qb, kb, vb, qi, kw = _proj_in(xp, g_mix, w_in_bf, lb, cp, sap, sbp, tm=256)

    s0 = jnp.zeros((1, H_A, DK_A, DK_A), F32)
    r3 = lambda a: a.reshape(1, s, GROUP)
    oa_p, st_p = _hgrn(r3(qa), r3(ka), r3(lf), r3(ia), r3(ga), g_hgrn, s0, HGRN_CHUNK, HGRN_SUB)

    nkb = s // ATT_TK
    ki_hi, ki_lo = _split_bf16(kw[:, :D_I])
    kit = jnp.concatenate([ki_hi, ki_hi, ki_lo], axis=1)
    kit_blocks = kit.reshape(nkb, ATT_TK, 3 * D_I).transpose(0, 2, 1)
    kt_blocks = kb.astype(BF16).reshape(nkb, ATT_TK, GROUP).transpose(0, 2, 1)
    ob_p = _attn_prompt(qi, kw, qb, kit_blocks, kt_blocks, vb.astype(BF16), min(TOPK_MAX, s // 4))

    zero_prev = jnp.zeros((SUBLANES, d_ff), F32)
    y_p, u_tail = _out_ffn(xp, oa_p.reshape(s, GROUP), ob_p, zero_prev, zero_prev, w_out_bf, g_ffn, w_up_bf,
                           conv_w[0], conv_b2, w_down_bf, g_fin, tm=256, seq_rows=s, carry_mode=True)

    ms = nb * t_new
    xs = x_sample.reshape(ms, d)
    pos_s = jnp.tile(past + jnp.arange(t_new, dtype=I32), nb)
    cs, sas, sbs = _rope_tables(pos_s)
    qa2, ka2, lf2, ia2, ga2, qb2, kb2, vb2, qi2, kw2 = _proj_in(xs, g_mix, w_in_bf, lb, cs, sas, sbs, tm=ms)

    pad_front = SUBLANES - t_new
    fp = lambda a: jnp.pad(a.reshape(nb, t_new, GROUP), ((0, 0), (pad_front, 0), (0, 0)))
    s0_s = jnp.swapaxes(state_hgrn[0], -1, -2)
    oa_s, st_s = _hgrn(fp(qa2), fp(ka2), fp(lf2), fp(ia2), fp(ga2), g_hgrn, s0_s, SUBLANES, SUBLANES)
    oa_s = oa_s[:, pad_front:, :].reshape(ms, GROUP)

    def rp(a):
        a = a.reshape(nb, t_new, a.shape[-1])
        return jnp.concatenate([a, jnp.broadcast_to(a[:, -1:], (nb, SUBLANES - t_new, a.shape[-1]))], axis=1)
    page_pad = lambda a: jnp.pad(a.reshape(nb, t_new, a.shape[-1]), ((0, 0), (0, PAGE_SIZE - t_new), (0, 0)))
    n_pool = cache_k.shape[1]
    ob_s = _attn_sample(
        page_table, rp(qi2), rp(kw2), rp(qb2), page_pad(kw2[:, :D_I]), page_pad(kb2), page_pad(vb2),
        cache_kidx[0], cache_k[0].reshape(n_pool, PAGE_SIZE, GROUP), cache_v[0].reshape(n_pool, PAGE_SIZE, GROUP),
        t_new, min(TOPK_MAX, (past + t_new) // 4))
    ob_s = ob_s[:, :t_new, :].reshape(ms, GROUP)

    sc0 = state_conv[0]
    zrow = jnp.zeros((nb, t_new - 1, d_ff), F32)
    p1 = jnp.concatenate([sc0[:, 1:2], zrow], axis=1).reshape(ms, d_ff)
    p2 = jnp.concatenate([sc0[:, 0:2], zrow[:, 1:]], axis=1).reshape(ms, d_ff)
    y_s, u_s = _out_ffn(xs, oa_s, ob_s, p1, p2, w_out_bf, g_ffn, w_up_bf, conv_w[0], conv_b2, w_down_bf, g_fin,
                        tm=ms, seq_rows=t_new, carry_mode=False)

    return (
        y_p.reshape(1, s, d),
        y_s.reshape(nb, t_new, d),
        kb.reshape(1, 1, s, H_B, HD_B),
        vb.reshape(1, 1, s, H_B, HD_B),
        kw[:, :D_I].reshape(1, 1, s, D_I),
        jnp.swapaxes(st_p, -1, -2).reshape(1, 1, H_A, DK_A, DK_A),
        u_tail[SUBLANES - (CONV_W - 1):].reshape(1, 1, CONV_W - 1, d_ff),
        kb2.reshape(1, nb, t_new, H_B, HD_B),
        vb2.reshape(1, nb, t_new, H_B, HD_B),
        kw2[:, :D_I].reshape(1, nb, t_new, D_I),
        jnp.swapaxes(st_s, -1, -2).reshape(1, nb, H_A, DK_A, DK_A),
        u_s.reshape(nb, t_new, d_ff)[:, t_new - (CONV_W - 1):].reshape(1, nb, CONV_W - 1, d_ff),
    )
```

```python
import functools

import jax
import jax.numpy as jnp
from jax import lax
from jax.experimental import pallas as pl
from jax.experimental.pallas import tpu as pltpu

F32 = jnp.float32
BF16 = jnp.bfloat16
I32 = jnp.int32

H_A = 4
DK_A = 128
H_B = 8
HD_B = 64
H_I = 8
D_I = 64
ROT = 16
ROPE_THETA = 500000.0
TOPK_MAX = 256
PAGE_SIZE = 128
CONV_W = 3
EPS = 1e-6
NEG_SCORE = -1e30
MASK_VALUE = -1e30
GROUP = 512
LOG2_E = 1.4426950408889634

LANES = 128
SUBLANES = 8
VMEM_LIMIT_BYTES = 61 * 1024 * 1024

HGRN_CHUNK = 64
HGRN_SUB = 16
BISECT_BLIND_PASSES = 10
BISECT_TESTED_ROUNDS = 8
ATT_TQ = 128
ATT_TK = 512
KV_CHUNK_PAGES = 8


def _silu(x):
    return x * jax.nn.sigmoid(x)


def _rms(x, g):
    return x * lax.rsqrt(jnp.mean(x * x, axis=-1, keepdims=True) + EPS) * g


def _split_bf16(x):
    hi = x.astype(BF16)
    lo = (x - hi.astype(F32)).astype(BF16)
    return hi, lo


def _rope(x, c, sa, sb):
    w = x.shape[1]
    return x * c + pltpu.roll(x, ROT // 2, 1) * sa + pltpu.roll(x, w - ROT // 2, 1) * sb


def _proj_in_kernel(x_ref, g_ref, w_ref, lb_ref, c_ref, sa_ref, sb_ref,
                    qa_ref, ka_ref, lf_ref, ia_ref, ga_ref, qb_ref, kb_ref, vb_ref, qi_ref, kw_ref, h_ref):
    h_ref[...] = _rms(x_ref[...], g_ref[...]).astype(BF16)

    def grp(i, width=GROUP):
        return jnp.dot(h_ref[...], w_ref[:, i * GROUP:i * GROUP + width], preferred_element_type=F32)

    c = jnp.tile(c_ref[...], (1, GROUP // LANES))
    sa = jnp.tile(sa_ref[...], (1, GROUP // LANES))
    sb = jnp.tile(sb_ref[...], (1, GROUP // LANES))

    qa_ref[...] = _silu(grp(0))
    lb = lb_ref[...]
    fg = lb + (1.0 - lb) * jax.nn.sigmoid(grp(1))
    ka_ref[...] = 1.0 - fg
    lf_ref[...] = jnp.log(fg)
    ia_ref[...] = grp(2)
    ga_ref[...] = _silu(grp(3))
    qb_ref[...] = _rope(grp(4), c, sa, sb)
    kb_ref[...] = _rope(grp(5), c, sa, sb)
    vb_ref[...] = grp(6)
    qi_ref[...] = _rope(grp(7), c, sa, sb)
    kw = grp(8, LANES)
    lane = lax.broadcasted_iota(I32, kw.shape, 1)
    roped = _rope(kw, c_ref[...], sa_ref[...], sb_ref[...])
    kw_ref[...] = jnp.where(lane < D_I, roped, kw * (H_I ** -0.5 * D_I ** -0.5))


def _proj_in(x2d, gain, w_pad, lb, cos_t, sin_a, sin_b, tm):
    m, d = x2d.shape
    n_pad = w_pad.shape[1]
    row = lambda i: (i, 0)
    const = lambda i: (0, 0)
    big = pl.BlockSpec((tm, GROUP), row)
    out_shapes = [jax.ShapeDtypeStruct((m, GROUP), F32)] * 9 + [jax.ShapeDtypeStruct((m, LANES), F32)]
    return pl.pallas_call(
        _proj_in_kernel,
        grid=(m // tm,),
        in_specs=[
            pl.BlockSpec((tm, d), row),
            pl.BlockSpec((1, d), const),
            pl.BlockSpec((d, n_pad), const, pipeline_mode=pl.Buffered(1)),
            pl.BlockSpec((1, GROUP), const),
            pl.BlockSpec((tm, LANES), row),
            pl.BlockSpec((tm, LANES), row),
            pl.BlockSpec((tm, LANES), row),
        ],
        out_specs=[big] * 9 + [pl.BlockSpec((tm, LANES), row)],
        out_shape=out_shapes,
        scratch_shapes=[pltpu.VMEM((tm, d), BF16)],
        compiler_params=pltpu.CompilerParams(
            dimension_semantics=("arbitrary",), vmem_limit_bytes=VMEM_LIMIT_BYTES),
        name="proj_in",
    )(x2d, gain, w_pad, lb, cos_t, sin_a, sin_b)


def _shift_rows(x, d):
    if d == 0:
        return x
    return pltpu.roll(x, d, 0)


def _cumsum_rows(g):
    c = g.shape[0]
    row = lax.broadcasted_iota(I32, g.shape, 0)
    k = 1
    while k < c:
        g = g + jnp.where(row >= k, _shift_rows(g, k), 0.0)
        k *= 2
    return g


def _hgrn_kernel(q_ref, k_ref, lf_ref, v_ref, ga_ref, gain_ref, s0_ref, o_ref, sfin_ref, st_ref, *, chunk, sub):
    ci = pl.program_id(1)

    @pl.when(ci == 0)
    def _():
        st_ref[...] = s0_ref[0]

    nsb = chunk // sub
    row = lax.broadcasted_iota(I32, (chunk, DK_A), 0)
    row_in_sub = row & (sub - 1)
    gain = gain_ref[...]

    for h in range(H_A):
        sl = slice(h * DK_A, (h + 1) * DK_A)
        q = q_ref[0, :, sl]
        k = k_ref[0, :, sl]
        v = v_ref[0, :, sl]
        b = _cumsum_rows(lf_ref[0, :, sl])
        st = st_ref[h]

        refs = [jnp.zeros((1, DK_A), F32)] + [b[i * sub - 1:i * sub, :] for i in range(1, nsb)]
        ref_rows = jnp.concatenate([jnp.broadcast_to(r, (sub, DK_A)) for r in refs], axis=0)
        q_rel = q * jnp.exp(b - ref_rows)

        o = lax.dot_general((q * jnp.exp(b)).astype(BF16), st.astype(BF16),
                            (((1,), (1,)), ((), ())), preferred_element_type=F32)

        off_rows = [jnp.zeros((sub, DK_A), F32)]
        for i in range(1, nsb):
            n_prev = i * sub
            k_rel = (k[:n_prev] * jnp.exp(refs[i] - b[:n_prev])).astype(BF16)
            att = lax.dot_general(q_rel[n_prev:n_prev + sub].astype(BF16), k_rel,
                                  (((1,), (1,)), ((), ())), preferred_element_type=F32)
            off_rows.append(jnp.dot(att.astype(BF16), v[:n_prev].astype(BF16), preferred_element_type=F32))
        if nsb > 1:
            o = o + jnp.concatenate(off_rows, axis=0)

        for d in range(sub):
            valid = row_in_sub >= d
            e = jnp.exp(jnp.where(valid, b - _shift_rows(b, d), 0.0))
            w = jnp.sum(q * _shift_rows(k, d) * e, axis=1, keepdims=True)
            o = o + jnp.where(valid, w * _shift_rows(v, d), 0.0)

        b_last = b[chunk - 1:chunk, :]
        k_dec = (k * jnp.exp(b_last - b)).astype(BF16)
        st_ref[h] = st * jnp.exp(b_last) + lax.dot_general(
            v.astype(BF16), k_dec, (((0,), (0,)), ((), ())), preferred_element_type=F32)

        o_ref[0, :, sl] = _rms(o, gain) * ga_ref[0, :, sl]

    @pl.when(ci == pl.num_programs(1) - 1)
    def _():
        sfin_ref[0] = st_ref[...]


def _hgrn(q, k, lf, v, ga, gain, s0_t, chunk, sub):
    b, t, w = q.shape
    blk = pl.BlockSpec((1, chunk, w), lambda bi, ci: (bi, ci, 0))
    st_spec = pl.BlockSpec((1, H_A, DK_A, DK_A), lambda bi, ci: (bi, 0, 0, 0))
    return pl.pallas_call(
        functools.partial(_hgrn_kernel, chunk=chunk, sub=sub),
        grid=(b, t // chunk),
        in_specs=[blk, blk, blk, blk, blk, pl.BlockSpec((1, DK_A), lambda bi, ci: (0, 0)), st_spec],
        out_specs=[blk, st_spec],
        out_shape=[jax.ShapeDtypeStruct((b, t, w), F32), jax.ShapeDtypeStruct((b, H_A, DK_A, DK_A), F32)],
        scratch_shapes=[pltpu.VMEM((H_A, DK_A, DK_A), F32)],
        compiler_params=pltpu.CompilerParams(
            dimension_semantics=("arbitrary", "arbitrary"), vmem_limit_bytes=VMEM_LIMIT_BYTES),
        name="hgrn",
    )(q, k, lf, v, ga, gain, s0_t)


def _rep_sum(x):
    return jnp.broadcast_to(jnp.sum(x, axis=1, keepdims=True), x.shape)


def _rep_max(x):
    return jnp.broadcast_to(jnp.max(x, axis=1, keepdims=True), x.shape)


def _rep_min(x):
    return jnp.broadcast_to(jnp.min(x, axis=1, keepdims=True), x.shape)


def _any_true(flag):
    return jnp.max(flag) > 0.5


def _lane_tiles(x):
    return [x[:, u * LANES:(u + 1) * LANES] for u in range(x.shape[1] // LANES)]


def _select_rows(sc_ref, nkb, n_hidden_tail, n_sel):
    _, r, w = sc_ref.shape
    shape = (r, LANES)
    neg = jnp.float32(NEG_SCORE)
    inf = jnp.float32(jnp.inf)
    kf = jnp.float32(n_sel)
    tail = n_hidden_tail.astype(F32)
    zeros = jnp.zeros(shape, F32)
    lane = lax.broadcasted_iota(I32, shape, 1)

    def count_ge(c):
        def body(j, acc):
            for xu in _lane_tiles(sc_ref[j]):
                acc = acc + jnp.where(xu >= c, 1.0, 0.0)
            return acc
        return _rep_sum(lax.fori_loop(0, nkb, body, zeros)) + jnp.where(neg >= c, tail, 0.0)

    def stats_body(j, carry):
        vmax, vmin_real, n_real = carry
        for x in _lane_tiles(sc_ref[j]):
            real = x > neg
            vmax = jnp.maximum(vmax, x)
            vmin_real = jnp.minimum(vmin_real, jnp.where(real, x, inf))
            n_real = n_real + jnp.where(real, 1.0, 0.0)
        return vmax, vmin_real, n_real

    def hidden_stats_body(j, carry):
        n_ge_neg, gmin = carry
        for x in _lane_tiles(sc_ref[j]):
            n_ge_neg = n_ge_neg + jnp.where(x >= neg, 1.0, 0.0)
            gmin = jnp.minimum(gmin, jnp.where(x > -inf, x, inf))
        return n_ge_neg, gmin

    vmax, vmin_real, n_real = lax.fori_loop(0, nkb, stats_body, (zeros - inf, zeros + inf, zeros))
    few = _any_true(jnp.where(_rep_sum(n_real) < kf, 1.0, 0.0))
    n_ge_neg, gmin = lax.fori_loop(0, jnp.where(few, nkb, 0), hidden_stats_body, (zeros, zeros + inf))
    has_tail = tail > 0.5
    vmax = _rep_max(vmax)
    vmax = jnp.where(has_tail, jnp.maximum(vmax, neg), vmax)
    vmin_real = _rep_min(vmin_real)
    gmin = _rep_min(gmin)
    gmin = jnp.where(has_tail, jnp.minimum(gmin, neg), gmin)
    n_real = _rep_sum(n_real)
    n_ge_neg = _rep_sum(n_ge_neg) + tail
    c_max = count_ge(vmax)

    at_max = c_max >= kf
    few_real = jnp.logical_and(jnp.logical_not(at_max), n_real < kf)
    at_neg = jnp.logical_and(few_real, n_ge_neg >= kf)
    below_neg = jnp.logical_and(few_real, n_ge_neg < kf)

    done = jnp.where(jnp.logical_or(at_max, at_neg), 1.0, 0.0)
    thr = jnp.where(at_max, vmax, jnp.where(at_neg, neg, zeros))
    n_gt = jnp.where(at_neg, n_real, zeros)
    n_ge = jnp.where(at_max, c_max, jnp.where(at_neg, n_ge_neg, zeros))
    tie = jnp.where(jnp.logical_and(done > 0.5, n_ge > kf), 1.0, 0.0)
    lo = jnp.where(below_neg, gmin, vmin_real)
    hi = jnp.where(below_neg, neg, vmax)
    c_hi = jnp.where(below_neg, n_ge_neg, c_max)

    def bisect(st):
        done, thr, n_gt, tie, lo, hi, c_hi = st
        mid = lo + (hi - lo) * 0.5
        c = count_ge(mid)
        live = done < 0.5
        hit = jnp.logical_and(live, c == kf)
        up = jnp.logical_and(live, c > kf)
        dn = jnp.logical_and(live, c < kf)
        return (jnp.where(hit, 1.0, done), jnp.where(hit, mid, thr), n_gt, tie,
                jnp.where(up, mid, lo), jnp.where(dn, mid, hi), jnp.where(dn, c, c_hi))

    def snap(st):
        done, thr, n_gt, tie, lo, hi, c_hi = st

        def body(j, m):
            for xu in _lane_tiles(sc_ref[j]):
                m = jnp.maximum(m, jnp.where(xu < hi, xu, -inf))
            return m
        below = _rep_max(lax.fori_loop(0, nkb, body, zeros - inf))
        below = jnp.where(jnp.logical_and(tail > 0.5, neg < hi), jnp.maximum(below, neg), below)
        c = count_ge(below)
        live = done < 0.5
        fin = jnp.logical_and(live, c >= kf)
        mv = jnp.logical_and(live, c < kf)
        return (jnp.where(fin, 1.0, done), jnp.where(fin, below, thr), jnp.where(fin, c_hi, n_gt),
                jnp.where(fin, jnp.where(c > kf, 1.0, 0.0), tie),
                lo, jnp.where(mv, below, hi), jnp.where(mv, c, c_hi))

    def not_done(st):
        return _any_true(1.0 - st[0])

    st = (done, thr, n_gt, tie, lo, hi, c_hi)
    st = lax.fori_loop(0, BISECT_BLIND_PASSES, lambda _, s: bisect(s), st)
    _, st = lax.while_loop(
        lambda ps: jnp.logical_and(ps[0] < BISECT_TESTED_ROUNDS, not_done(ps[1])),
        lambda ps: (ps[0] + 1, bisect(bisect(ps[1]))), (jnp.int32(0), st))
    st = lax.while_loop(not_done, lambda s: snap(bisect(s)), st)
    done, thr, n_gt, tie, lo, hi, c_hi = st

    need = kf - n_gt
    n_stored = nkb * w

    def count_eq_upto(jb):
        def body(j, acc):
            for u, xu in enumerate(_lane_tiles(sc_ref[j])):
                idx = j * w + u * LANES + lane
                acc = acc + jnp.where(jnp.logical_and(xu == thr, idx <= jb), 1.0, 0.0)
            return acc
        return _rep_sum(lax.fori_loop(0, nkb, body, zeros))

    def jstep(pj):
        p, (jlo, jhi) = pj
        jmid = jlo + lax.shift_right_arithmetic(jhi - jlo, 1)
        ok = count_eq_upto(jmid) >= need
        return p + 1, (jnp.where(ok, jlo, jmid), jnp.where(ok, jmid, jhi))

    n_jpass = jnp.where(_any_true(tie), 16, 0)
    jlo0 = jnp.full(shape, -1, I32)
    jhi0 = jnp.zeros(shape, I32) + n_stored
    _, (_, jhi) = lax.while_loop(lambda pj: pj[0] < n_jpass, jstep, (jnp.int32(0), (jlo0, jhi0)))
    jmax = jnp.where(tie > 0.5, jhi, jnp.int32(2 ** 30))
    return thr, jmax


def _attn_prompt_kernel(qi_ref, kw_ref, qb_ref, kit_ref, kt_ref, v_ref, o_ref,
                        sc_ref, wb_ref, qc_ref, qh_ref, sa_ref, sb_ref, mxa_ref, mxb_ref, ba_ref, bb_ref,
                        pa_ref, pb_ref,
                        m_ref, l_ref, acc_ref,
                        *, n_keys, n_sel):
    tq, tk = ATT_TQ, ATT_TK
    tiles = [slice(u * LANES, (u + 1) * LANES) for u in range(tk // LANES)]
    i = pl.program_id(0)
    nkb = (i * tq + tq + tk - 1) // tk
    last_blk = n_keys // tk - 1
    q_pos = i * tq + lax.broadcasted_iota(I32, (tq, LANES), 0)
    lane = lax.broadcasted_iota(I32, (tq, LANES), 1)

    @pl.when(i == 0)
    def _():
        sc_ref[...] = jnp.full(sc_ref.shape, NEG_SCORE, F32)

    kw = kw_ref[...]
    for h in range(H_I):
        hi, lo = _split_bf16(qi_ref[:, h * D_I:(h + 1) * D_I])
        qc_ref[h] = jnp.concatenate([hi, lo, hi], axis=1)
        wb_ref[h] = jnp.broadcast_to(kw[:, D_I + h:D_I + h + 1], (tq, LANES))

    def score_body(j, _):
        kit = kit_ref[j]
        for h in range(H_I):
            d = jnp.dot(qc_ref[h], kit, preferred_element_type=F32)
            wbh = wb_ref[h]
            for u, us in enumerate(tiles):
                val = wbh * jnp.maximum(d[:, us], 0.0)
                if h > 0:
                    val = sc_ref[j, :, us] + val
                if h == H_I - 1:
                    val = jnp.where(j * tk + u * LANES + lane <= q_pos, val, NEG_SCORE)
                sc_ref[j, :, us] = val
        return 0

    lax.fori_loop(0, nkb, score_body, 0)

    tail = jnp.zeros((tq, LANES), I32) + (n_keys - nkb * tk)
    thr, jmax = _select_rows(sc_ref, nkb, tail, n_sel)

    m_ref[...] = jnp.full(m_ref.shape, MASK_VALUE, F32)
    l_ref[...] = jnp.zeros(l_ref.shape, F32)
    acc_ref[...] = jnp.zeros(acc_ref.shape, F32)
    for h in range(H_B):
        qh_ref[h] = (qb_ref[:, h * HD_B:(h + 1) * HD_B] * (HD_B ** -0.5 * LOG2_E)).astype(BF16)
    first_half = lane < HD_B

    def selection_bias(blk, b_ref):
        for u, us in enumerate(tiles):
            xu = sc_ref[blk, :, us]
            idx = blk * tk + u * LANES + lane
            sel = jnp.logical_or(xu > thr, jnp.logical_and(xu == thr, idx <= jmax))
            sel = jnp.logical_and(sel, idx <= q_pos)
            b_ref[:, us] = jnp.where(sel, 0.0, MASK_VALUE)

    def masked_logits(blk, h, s_ref, mx_ref, b_ref):
        s = jnp.dot(qh_ref[h], kt_ref[blk, h * HD_B:(h + 1) * HD_B, :], preferred_element_type=F32) + b_ref[...]
        s_ref[h] = s
        mx = s[:, tiles[0]]
        for us in tiles[1:]:
            mx = jnp.maximum(mx, s[:, us])
        mx_ref[h] = mx

    def reduce_block(blk, s_ref, mx_ref, nxt, b_ref, p_ref):
        rows = pl.ds(pl.multiple_of(blk * tk, tk), tk)
        for pair in range(H_B // 2):
            ps = slice(pair * LANES, (pair + 1) * LANES)
            v_pair = v_ref[rows, ps]
            for half in range(2):
                h = 2 * pair + half
                m_old = m_ref[h]
                m_new = jnp.maximum(m_old, _rep_max(mx_ref[h]))
                alpha = jnp.exp2(m_old - m_new)
                acc_ref[h] = alpha * acc_ref[h]
                l_new = alpha * l_ref[h]
                for us in tiles:
                    pt = jnp.exp2(s_ref[h, :, us] - m_new)
                    l_new = l_new + pt
                    p_ref[h, :, us] = pt.astype(BF16)
                l_ref[h] = l_new
                m_ref[h] = m_new
                masked_logits(nxt, h, s_ref, mx_ref, b_ref)
                acc_ref[h] += jnp.dot(p_ref[h], v_pair, preferred_element_type=F32)

    selection_bias(0, ba_ref)
    selection_bias(1, bb_ref)
    for h in range(H_B):
        masked_logits(0, h, sa_ref, mxa_ref, ba_ref)
        masked_logits(1, h, sb_ref, mxb_ref, bb_ref)

    def pair_body(jj, _):
        a = 2 * jj
        nxt_a = jnp.minimum(a + 2, last_blk)
        nxt_b = jnp.minimum(a + 3, last_blk)
        selection_bias(nxt_a, ba_ref)
        selection_bias(nxt_b, bb_ref)
        reduce_block(a, sa_ref, mxa_ref, nxt_a, ba_ref, pa_ref)
        reduce_block(a + 1, sb_ref, mxb_ref, nxt_b, bb_ref, pb_ref)
        return 0

    lax.fori_loop(0, (nkb + 1) // 2, pair_body, 0)

    for pair in range(H_B // 2):
        ps = slice(pair * LANES, (pair + 1) * LANES)
        d0 = jnp.broadcast_to(jnp.sum(l_ref[2 * pair], axis=1, keepdims=True), (tq, LANES))
        d1 = jnp.broadcast_to(jnp.sum(l_ref[2 * pair + 1], axis=1, keepdims=True), (tq, LANES))
        o_ref[:, ps] = jnp.where(first_half, acc_ref[2 * pair] / d0, acc_ref[2 * pair + 1] / d1)


def _attn_prompt(qi, kw, qb, kit_blocks, kt_blocks, v_bf, n_sel):
    s = qi.shape[0]
    nkb_total = s // ATT_TK
    row = lambda i: (i, 0)
    whole = pl.BlockSpec(memory_space=pltpu.VMEM)
    return pl.pallas_call(
        functools.partial(_attn_prompt_kernel, n_keys=s, n_sel=n_sel),
        grid=(s // ATT_TQ,),
        in_specs=[
            pl.BlockSpec((ATT_TQ, GROUP), row),
            pl.BlockSpec((ATT_TQ, LANES), row),
            pl.BlockSpec((ATT_TQ, GROUP), row),
            whole, whole, whole,
        ],
        out_specs=pl.BlockSpec((ATT_TQ, GROUP), row),
        out_shape=jax.ShapeDtypeStruct((s, GROUP), F32),
        scratch_shapes=[
            pltpu.VMEM((nkb_total, ATT_TQ, ATT_TK), F32),
            pltpu.VMEM((H_I, ATT_TQ, LANES), F32),
            pltpu.VMEM((H_I, ATT_TQ, 3 * D_I), BF16),
            pltpu.VMEM((H_B, ATT_TQ, HD_B), BF16),
            pltpu.VMEM((H_B, ATT_TQ, ATT_TK), F32),
            pltpu.VMEM((H_B, ATT_TQ, ATT_TK), F32),
            pltpu.VMEM((H_B, ATT_TQ, LANES), F32),
            pltpu.VMEM((H_B, ATT_TQ, LANES), F32),
            pltpu.VMEM((ATT_TQ, ATT_TK), F32),
            pltpu.VMEM((ATT_TQ, ATT_TK), F32),
            pltpu.VMEM((H_B, ATT_TQ, ATT_TK), BF16),
            pltpu.VMEM((H_B, ATT_TQ, ATT_TK), BF16),
            pltpu.VMEM((H_B, ATT_TQ, LANES), F32),
            pltpu.VMEM((H_B, ATT_TQ, LANES), F32),
            pltpu.VMEM((H_B, ATT_TQ, LANES), F32),
        ],
        compiler_params=pltpu.CompilerParams(
            dimension_semantics=("arbitrary",), vmem_limit_bytes=VMEM_LIMIT_BYTES),
        name="attn_prompt",
    )(qi, kw, qb, kit_blocks, kt_blocks, v_bf)


def _attn_sample_kernel(pt_ref, qi_ref, kw_ref, qb_ref, kin_ref, kn_ref, vn_ref,
                        ckidx_hbm, ck_hbm, cv_hbm, o_ref,
                        kid_buf, k_buf, v_buf, sc_ref, s_scr, p_scr, m_ref, l_ref, acc_ref, sem_i, sem_k, sem_v,
                        *, n_pages, t_new, n_sel):
    b = pl.program_id(0)
    r = SUBLANES
    nrow = H_B * r
    n_chunks = n_pages // KV_CHUNK_PAGES

    def kidx_copy(p):
        return pltpu.make_async_copy(ckidx_hbm.at[pt_ref[b, p]], kid_buf.at[p], sem_i)

    def kv_copies(c, slot, p):
        page = pt_ref[b, c * KV_CHUNK_PAGES + p]
        return (pltpu.make_async_copy(ck_hbm.at[page], k_buf.at[slot, p], sem_k.at[slot]),
                pltpu.make_async_copy(cv_hbm.at[page], v_buf.at[slot, p], sem_v.at[slot]))

    def start_chunk(c, slot):
        for p in range(KV_CHUNK_PAGES):
            ck, cv = kv_copies(c, slot, p)
            ck.start()
            cv.start()

    def wait_chunk(c, slot):
        for p in range(KV_CHUNK_PAGES):
            ck, cv = kv_copies(c, slot, p)
            ck.wait()
            cv.wait()

    def start_kidx(p, _):
        kidx_copy(p).start()
        return 0

    def wait_kidx(p, _):
        kidx_copy(p).wait()
        return 0

    lax.fori_loop(0, n_pages, start_kidx, 0)
    start_chunk(0, 0)

    kw = kw_ref[0]
    qi = qi_ref[0]
    q_hi, q_lo, w_rows = [], [], []
    for h in range(H_I):
        hi, lo = _split_bf16(qi[:, h * D_I:(h + 1) * D_I])
        q_hi.append(hi)
        q_lo.append(lo)
        w_rows.append(jnp.broadcast_to(kw[:, D_I + h:D_I + h + 1], (r, LANES)))
    q_hi = jnp.concatenate(q_hi, axis=0)
    q_hl = jnp.concatenate([q_hi, jnp.concatenate(q_lo, axis=0)], axis=0)
    w_rows = jnp.concatenate(w_rows, axis=0)
    nt = (((1,), (1,)), ((), ()))

    def key_scores(keys):
        k_hi, k_lo = _split_bf16(keys)
        d2 = lax.dot_general(q_hl, k_hi, nt, preferred_element_type=F32)
        d = d2[:H_I * r] + d2[H_I * r:] + lax.dot_general(q_hi, k_lo, nt, preferred_element_type=F32)
        out = []
        for u in range(keys.shape[0] // LANES):
            us = slice(u * LANES, (u + 1) * LANES)
            wd = w_rows * jnp.maximum(d[:, us], 0.0)
            acc = wd[0:r]
            for h in range(1, H_I):
                acc = acc + wd[h * r:(h + 1) * r]
            out.append(acc)
        return out

    lax.fori_loop(0, n_pages, wait_kidx, 0)

    def score_body(c, _):
        first = c * KV_CHUNK_PAGES
        keys = kid_buf[pl.ds(first, KV_CHUNK_PAGES)].reshape(KV_CHUNK_PAGES * PAGE_SIZE, D_I)
        for u, tile in enumerate(key_scores(keys)):
            sc_ref[c, :, u * LANES:(u + 1) * LANES] = tile
        return 0

    lax.fori_loop(0, n_chunks, score_body, 0)

    lane = lax.broadcasted_iota(I32, (r, LANES), 1)
    t_row = jnp.minimum(lax.broadcasted_iota(I32, (r, LANES), 0), t_new - 1)
    new_sc = key_scores(kin_ref[0])[0]
    new_vis = lane <= t_row
    sc_ref[n_chunks] = jnp.full((r, KV_CHUNK_PAGES * LANES), -jnp.inf, F32)
    sc_ref[n_chunks, :, 0:LANES] = jnp.where(lane >= t_new, -jnp.inf, jnp.where(new_vis, new_sc, NEG_SCORE))

    thr, jmax = _select_rows(sc_ref, n_chunks + 1, jnp.zeros((r, LANES), I32), n_sel)

    qb = qb_ref[0] * (HD_B ** -0.5)
    q_all = jnp.concatenate([qb[:, h * HD_B:(h + 1) * HD_B] for h in range(H_B)], axis=0).astype(BF16)
    cols = PAGE_SIZE * H_B
    row_head = lax.shift_right_logical(lax.broadcasted_iota(I32, (nrow, cols), 0), r.bit_length() - 1)
    col_head = lax.broadcasted_iota(I32, (nrow, cols), 1) & (H_B - 1)
    head_match = row_head == col_head
    rep = jnp.where(lax.shift_right_logical(lax.broadcasted_iota(I32, (LANES, cols), 1), H_B.bit_length() - 1)
                    == lax.broadcasted_iota(I32, (LANES, cols), 0), 1.0, 0.0).astype(BF16)

    m_ref[...] = jnp.full(m_ref.shape, MASK_VALUE, F32)
    l_ref[...] = jnp.zeros(l_ref.shape, F32)
    acc_ref[...] = jnp.zeros(acc_ref.shape, F32)

    def attend(blk, n_blk, kmat, vmat, extra_mask):
        mx = jnp.full((nrow, LANES), -jnp.inf, F32)
        for u in range(n_blk):
            x = sc_ref[blk, :, u * LANES:(u + 1) * LANES]
            idx = (blk * KV_CHUNK_PAGES + u) * LANES + lane
            sel = jnp.logical_or(x > thr, jnp.logical_and(x == thr, idx <= jmax))
            if extra_mask is not None:
                sel = jnp.logical_and(sel, extra_mask)
            sel_rep = jnp.dot(jnp.where(sel, 1.0, 0.0).astype(BF16), rep, preferred_element_type=F32)
            keep = jnp.logical_and(jnp.tile(sel_rep, (H_B, 1)) > 0.5, head_match)
            kb = kmat[u * cols:(u + 1) * cols].astype(BF16)
            s = lax.dot_general(q_all, kb, nt, preferred_element_type=F32) + jnp.where(keep, 0.0, MASK_VALUE)
            s_scr[:, u * cols:(u + 1) * cols] = s
            for c0 in range(0, cols, LANES):
                mx = jnp.maximum(mx, s[:, c0:c0 + LANES])
        m_old = m_ref[...]
        m_new = jnp.maximum(m_old, _rep_max(mx))
        alpha = jnp.exp(m_old - m_new)
        p_sum = jnp.zeros((nrow, LANES), F32)
        for c0 in range(0, n_blk * cols, LANES):
            pt = jnp.exp(s_scr[:, c0:c0 + LANES] - m_new)
            p_sum = p_sum + pt
            p_scr[:, c0:c0 + LANES] = pt.astype(BF16)
        l_ref[...] = alpha * l_ref[...] + p_sum
        m_ref[...] = m_new
        pv = jnp.dot(p_scr[:, :n_blk * cols], vmat.astype(BF16), preferred_element_type=F32)
        acc_ref[...] = alpha[:, :HD_B] * acc_ref[...] + pv

    def chunk_body(c, _):
        slot = c % 2
        wait_chunk(c, slot)

        @pl.when(c + 1 < n_chunks)
        def _():
            start_chunk(c + 1, 1 - slot)

        attend(c, KV_CHUNK_PAGES, k_buf[slot].reshape(KV_CHUNK_PAGES * cols, HD_B),
               v_buf[slot].reshape(KV_CHUNK_PAGES * cols, HD_B), None)
        return 0

    lax.fori_loop(0, n_chunks, chunk_body, 0)
    attend(n_chunks, 1, kn_ref[0].reshape(cols, HD_B), vn_ref[0].reshape(cols, HD_B), new_vis)

    denom = jnp.sum(l_ref[...], axis=1, keepdims=True)
    out = acc_ref[...] / denom
    o_ref[0] = jnp.concatenate([out[h * r:(h + 1) * r] for h in range(H_B)], axis=1)


def _attn_sample(page_table, qi, kw, qb, ki_new, k_new, v_new, cache_kidx, cache_k, cache_v, t_new, n_sel):
    nb, n_pages = page_table.shape
    r = SUBLANES
    nrow = H_B * r
    per_b = lambda b, pt: (b, 0, 0)
    per_b4 = lambda b, pt: (b, 0, 0, 0)
    anyspec = pl.BlockSpec(memory_space=pl.ANY)
    chunk_cols = KV_CHUNK_PAGES * PAGE_SIZE * H_B
    grid_spec = pltpu.PrefetchScalarGridSpec(
        num_scalar_prefetch=1,
        grid=(nb,),
        in_specs=[
            pl.BlockSpec((1, r, GROUP), per_b),
            pl.BlockSpec((1, r, LANES), per_b),
            pl.BlockSpec((1, r, GROUP), per_b),
            pl.BlockSpec((1, PAGE_SIZE, D_I), per_b),
            pl.BlockSpec((1, PAGE_SIZE, H_B, HD_B), per_b4),
            pl.BlockSpec((1, PAGE_SIZE, H_B, HD_B), per_b4),
            anyspec, anyspec, anyspec,
        ],
        out_specs=pl.BlockSpec((1, r, GROUP), per_b),
        scratch_shapes=[
            pltpu.VMEM((n_pages, PAGE_SIZE, D_I), F32),
            pltpu.VMEM((2, KV_CHUNK_PAGES, PAGE_SIZE, H_B, HD_B), F32),
            pltpu.VMEM((2, KV_CHUNK_PAGES, PAGE_SIZE, H_B, HD_B), F32),
            pltpu.VMEM((n_pages // KV_CHUNK_PAGES + 1, r, KV_CHUNK_PAGES * LANES), F32),
            pltpu.VMEM((nrow, chunk_cols), F32),
            pltpu.VMEM((nrow, chunk_cols), BF16),
            pltpu.VMEM((nrow, LANES), F32),
            pltpu.VMEM((nrow, LANES), F32),
            pltpu.VMEM((nrow, HD_B), F32),
            pltpu.SemaphoreType.DMA(()),
            pltpu.SemaphoreType.DMA((2,)),
            pltpu.SemaphoreType.DMA((2,)),
        ],
    )
    return pl.pallas_call(
        functools.partial(_attn_sample_kernel, n_pages=n_pages, t_new=t_new, n_sel=n_sel),
        grid_spec=grid_spec,
        out_shape=jax.ShapeDtypeStruct((nb, r, GROUP), F32),
        compiler_params=pltpu.CompilerParams(
            dimension_semantics=("arbitrary",), vmem_limit_bytes=VMEM_LIMIT_BYTES),
        name="attn_sample",
    )(page_table, qi, kw, qb, ki_new, k_new, v_new, cache_kidx, cache_k, cache_v)


def _out_ffn_kernel(x_ref, oa_ref, ob_ref, p1_ref, p2_ref, wo_ref, g2_ref, wu_ref, cw_ref, cb_ref, wd_ref, gf_ref,
                    y_ref, u_ref, carry_ref, x1_ref, h2_ref, *, d_ff, ff_tile, seq_rows, carry_mode):
    tm = x_ref.shape[0]
    i = pl.program_id(0)
    mixed = jnp.concatenate([oa_ref[...], ob_ref[...]], axis=1).astype(BF16)
    x1 = x_ref[...] + jnp.dot(mixed, wo_ref[...], preferred_element_type=F32)
    h2_ref[...] = _rms(x1, g2_ref[...]).astype(BF16)
    x1_ref[...] = x1

    row = lax.broadcasted_iota(I32, (tm, ff_tile), 0)
    if carry_mode:
        @pl.when(i == 0)
        def _():
            carry_ref[...] = jnp.zeros(carry_ref.shape, F32)
        t_in_seq = row
    else:
        t_in_seq = row & (seq_rows - 1)

    for c in range(d_ff // ff_tile):
        cs = slice(c * ff_tile, (c + 1) * ff_tile)
        h2 = h2_ref[...]
        u = jnp.dot(h2, wu_ref[:, cs], preferred_element_type=F32)
        v = jnp.dot(h2, wu_ref[:, d_ff + c * ff_tile:d_ff + (c + 1) * ff_tile], preferred_element_type=F32)
        if carry_mode:
            prev = carry_ref[:, cs]
            p1 = jnp.broadcast_to(prev[SUBLANES - 1:SUBLANES], (tm, ff_tile))
            p2 = jnp.where(row == 0, jnp.broadcast_to(prev[SUBLANES - 2:SUBLANES - 1], (tm, ff_tile)), p1)
            carry_ref[:, cs] = u[tm - SUBLANES:]
            u_ref[:, cs] = u[tm - SUBLANES:]
        else:
            p1 = p1_ref[:, cs]
            p2 = p2_ref[:, cs]
            u_ref[:, cs] = u
        u1 = jnp.where(t_in_seq >= 1, pltpu.roll(u, 1, 0), p1)
        u2 = jnp.where(t_in_seq >= 2, pltpu.roll(u, 2, 0), p2)
        conv = cb_ref[:, cs] + cw_ref[0:1, cs] * u2 + cw_ref[1:2, cs] * u1 + cw_ref[2:3, cs] * u
        gate = (_silu(conv) * v).astype(BF16)
        x1_ref[...] += jnp.dot(gate, wd_ref[cs, :], preferred_element_type=F32)

    y_ref[...] = _rms(x1_ref[...], gf_ref[...])


def _out_ffn(x2d, oa, ob, p1, p2, w_out, g2, w_up, conv_w, conv_b, w_down, gf, tm, seq_rows, carry_mode):
    m, d = x2d.shape
    d_ff = w_down.shape[0]
    ff_tile = 256
    row = lambda i: (i, 0)
    const = lambda i: (0, 0)
    if carry_mode:
        prev_spec = pl.BlockSpec((SUBLANES, d_ff), const)
        u_spec = pl.BlockSpec((SUBLANES, d_ff), const)
        u_shape = jax.ShapeDtypeStruct((SUBLANES, d_ff), F32)
    else:
        prev_spec = pl.BlockSpec((tm, d_ff), row)
        u_spec = pl.BlockSpec((tm, d_ff), row)
        u_shape = jax.ShapeDtypeStruct((m, d_ff), F32)
    return pl.pallas_call(
        functools.partial(_out_ffn_kernel, d_ff=d_ff, ff_tile=ff_tile, seq_rows=seq_rows, carry_mode=carry_mode),
        grid=(m // tm,),
        in_specs=[
            pl.BlockSpec((tm, d), row),
            pl.BlockSpec((tm, GROUP), row),
            pl.BlockSpec((tm, GROUP), row),
            prev_spec, prev_spec,
            pl.BlockSpec(w_out.shape, const, pipeline_mode=pl.Buffered(1)),
            pl.BlockSpec((1, d), const),
            pl.BlockSpec(w_up.shape, const, pipeline_mode=pl.Buffered(1)),
            pl.BlockSpec(conv_w.shape, const),
            pl.BlockSpec((1, d_ff), const),
            pl.BlockSpec(w_down.shape, const, pipeline_mode=pl.Buffered(1)),
            pl.BlockSpec((1, d), const),
        ],
        out_specs=[pl.BlockSpec((tm, d), row), u_spec],
        out_shape=[jax.ShapeDtypeStruct((m, d), F32), u_shape],
        scratch_shapes=[pltpu.VMEM((SUBLANES, d_ff), F32), pltpu.VMEM((tm, d), F32), pltpu.VMEM((tm, d), BF16)],
        compiler_params=pltpu.CompilerParams(
            dimension_semantics=("arbitrary",), vmem_limit_bytes=VMEM_LIMIT_BYTES),
        name="out_ffn",
    )(x2d, oa, ob, p1, p2, w_out, g2, w_up, conv_w, conv_b, w_down, gf)


def _rope_tables(pos):
    half = ROT // 2
    inv = jnp.power(ROPE_THETA, -jnp.arange(half, dtype=F32) * (2.0 / ROT))
    ang = pos.astype(F32)[:, None] * inv[None, :]
    cos, sin = jnp.cos(ang), jnp.sin(ang)
    n = pos.shape[0]
    ones = jnp.ones((n, HD_B - ROT), F32)
    zeros = jnp.zeros((n, HD_B - ROT), F32)
    zh = jnp.zeros((n, half), F32)
    c = jnp.concatenate([cos, cos, ones], axis=1)
    sa = jnp.concatenate([zh, sin, zeros], axis=1)
    sb = jnp.concatenate([-sin, zh, zeros], axis=1)
    rep = LANES // HD_B
    return jnp.tile(c, (1, rep)), jnp.tile(sa, (1, rep)), jnp.tile(sb, (1, rep))


def kernel(x_prompt, x_sample, cache_k, cache_v, cache_kidx, state_hgrn, state_conv, page_table,
           norm_mix_gain, w_in, lb_logits, hgrn_norm_gain, w_out, norm_ffn_gain, w_up, conv_w,
           conv_b, w_down, final_norm_gain):
    bp, s, d = x_prompt.shape
    nb, t_new, _ = x_sample.shape
    depth = w_in.shape[0]
    assert bp == 1 and depth == 1
    n_pages = page_table.shape[1]
    past = n_pages * PAGE_SIZE
    d_ff = w_down.shape[1]
    assert s % ATT_TK == 0 and s % HGRN_CHUNK == 0 and n_pages % KV_CHUNK_PAGES == 0
    assert t_new <= SUBLANES and CONV_W - 1 <= t_new

    lower_bounds = jnp.cumsum(jax.nn.softmax(lb_logits.astype(F32), axis=0), axis=0)
    lb = lower_bounds[0][None, :]
    n_in = w_in.shape[2]
    n_pad = 8 * GROUP + LANES
    w_in_bf = jnp.pad(w_in[0], ((0, 0), (0, n_pad - n_in))).astype(BF16)
    w_out_bf = w_out[0].astype(BF16)
    w_up_bf = w_up[0].astype(BF16)
    w_down_bf = w_down[0].astype(BF16)
    g_mix = norm_mix_gain[0][None, :]
    g_ffn = norm_ffn_gain[0][None, :]
    g_fin = final_norm_gain[None, :]
    g_hgrn = hgrn_norm_gain[0][None, :]
    conv_b2 = conv_b[0][None, :]

    xp = x_prompt.reshape(s, d)
    cp, sap, sbp = _rope_tables(jnp.arange(s, dtype=I32))
    qa, ka, lf, ia, ga, qb, kb, vb, qi, kw = _proj_in(xp, g_mix, w_in_bf, lb, cp, sap, sbp, tm=256)

    s0 = jnp.zeros((1, H_A, DK_A, DK_A), F32)
    r3 = lambda a: a.reshape(1, s, GROUP)
    oa_p, st_p = _hgrn(r3(qa), r3(ka), r3(lf), r3(ia), r3(ga), g_hgrn, s0, HGRN_CHUNK, HGRN_SUB)

    nkb = s // ATT_TK
    ki_hi, ki_lo = _split_bf16(kw[:, :D_I])
    kit = jnp.concatenate([ki_hi, ki_hi, ki_lo], axis=1)
    kit_blocks = kit.reshape(nkb, ATT_TK, 3 * D_I).transpose(0, 2, 1)
    kt_blocks = kb.astype(BF16).reshape(nkb, ATT_TK, GROUP).transpose(0, 2, 1)
    ob_p = _attn_prompt(qi, kw, qb, kit_blocks, kt_blocks, vb.astype(BF16), min(TOPK_MAX, s // 4))

    zero_prev = jnp.zeros((SUBLANES, d_ff), F32)
    y_p, u_tail = _out_ffn(xp, oa_p.reshape(s, GROUP), ob_p, zero_prev, zero_prev, w_out_bf, g_ffn, w_up_bf,
                           conv_w[0], conv_b2, w_down_bf, g_fin, tm=256, seq_rows=s, carry_mode=True)

    ms = nb * t_new
    xs = x_sample.reshape(ms, d)
    pos_s = jnp.tile(past + jnp.arange(t_new, dtype=I32), nb)
    cs, sas, sbs = _rope_tables(pos_s)
    qa2, ka2, lf2, ia2, ga2, qb2, kb2, vb2, qi2, kw2 = _proj_in(xs, g_mix, w_in_bf, lb, cs, sas, sbs, tm=ms)

    pad_front = SUBLANES - t_new
    fp = lambda a: jnp.pad(a.reshape(nb, t_new, GROUP), ((0, 0), (pad_front, 0), (0, 0)))
    s0_s = jnp.swapaxes(state_hgrn[0], -1, -2)
    oa_s, st_s = _hgrn(fp(qa2), fp(ka2), fp(lf2), fp(ia2), fp(ga2), g_hgrn, s0_s, SUBLANES, SUBLANES)
    oa_s = oa_s[:, pad_front:, :].reshape(ms, GROUP)

    def rp(a):
        a = a.reshape(nb, t_new, a.shape[-1])
        return jnp.concatenate([a, jnp.broadcast_to(a[:, -1:], (nb, SUBLANES - t_new, a.shape[-1]))], axis=1)
    page_pad = lambda a: jnp.pad(a.reshape(nb, t_new, a.shape[-1]), ((0, 0), (0, PAGE_SIZE - t_new), (0, 0)))
    heads = lambda a: page_pad(a).reshape(nb, PAGE_SIZE, H_B, HD_B)
    ob_s = _attn_sample(
        page_table, rp(qi2), rp(kw2), rp(qb2), page_pad(kw2[:, :D_I]), heads(kb2), heads(vb2),
        cache_kidx[0], cache_k[0], cache_v[0],
        t_new, min(TOPK_MAX, (past + t_new) // 4))
    ob_s = ob_s[:, :t_new, :].reshape(ms, GROUP)

    sc0 = state_conv[0]
    zrow = jnp.zeros((nb, t_new - 1, d_ff), F32)
    p1 = jnp.concatenate([sc0[:, 1:2], zrow], axis=1).reshape(ms, d_ff)
    p2 = jnp.concatenate([sc0[:, 0:2], zrow[:, 1:]], axis=1).reshape(ms, d_ff)
    y_s, u_s = _out_ffn(xs, oa_s, ob_s, p1, p2, w_out_bf, g_ffn, w_up_bf, conv_w[0], conv_b2, w_down_bf, g_fin,
                        tm=ms, seq_rows=t_new, carry_mode=False)

    return (
        y_p.reshape(1, s, d),
        y_s.reshape(nb, t_new, d),
        kb.reshape(1, 1, s, H_B, HD_B),
        vb.reshape(1, 1, s, H_B, HD_B),
        kw[:, :D_I].reshape(1, 1, s, D_I),
        jnp.swapaxes(st_p, -1, -2).reshape(1, 1, H_A, DK_A, DK_A),
        u_tail[SUBLANES - (CONV_W - 1):].reshape(1, 1, CONV_W - 1, d_ff),
        kb2.reshape(1, nb, t_new, H_B, HD_B),
        vb2.reshape(1, nb, t_new, H_B, HD_B),
        kw2[:, :D_I].reshape(1, nb, t_new, D_I),
        jnp.swapaxes(st_s, -1, -2).reshape(1, nb, H_A, DK_A, DK_A),
        u_s.reshape(nb, t_new, d_ff)[:, t_new - (CONV_W - 1):].reshape(1, nb, CONV_W - 1, d_ff),
    )
```

```python
import functools

import jax
import jax.numpy as jnp
from jax import lax
from jax.experimental import pallas as pl
from jax.experimental.pallas import tpu as pltpu

F32 = jnp.float32
BF16 = jnp.bfloat16
I32 = jnp.int32

H_A = 4
DK_A = 128
H_B = 8
HD_B = 64
H_I = 8
D_I = 64
ROT = 16
ROPE_THETA = 500000.0
TOPK_MAX = 256
PAGE_SIZE = 128
CONV_W = 3
EPS = 1e-6
NEG_SCORE = -1e30
MASK_VALUE = -1e30
GROUP = 512
LOG2_E = 1.4426950408889634

LANES = 128
SUBLANES = 8
VMEM_LIMIT_BYTES = 61 * 1024 * 1024

HGRN_CHUNK = 64
HGRN_SUB = 16
BISECT_BLIND_PASSES = 10
BISECT_TESTED_ROUNDS = 8
ATT_TQ = 128
ATT_TK = 512
KV_CHUNK_PAGES = 8


def _silu(x):
    return x * jax.nn.sigmoid(x)


def _rms(x, g):
    return x * lax.rsqrt(jnp.mean(x * x, axis=-1, keepdims=True) + EPS) * g


def _split_bf16(x):
    hi = x.astype(BF16)
    lo = (x - hi.astype(F32)).astype(BF16)
    return hi, lo


def _rope(x, c, sa, sb):
    w = x.shape[1]
    return x * c + pltpu.roll(x, ROT // 2, 1) * sa + pltpu.roll(x, w - ROT // 2, 1) * sb


def _proj_in_kernel(x_ref, g_ref, w_ref, lb_ref, c_ref, sa_ref, sb_ref,
                    qa_ref, ka_ref, lf_ref, ia_ref, ga_ref, qb_ref, kb_ref, vb_ref, qi_ref, kw_ref, h_ref):
    h_ref[...] = _rms(x_ref[...], g_ref[...]).astype(BF16)

    def grp(i, width=GROUP):
        return jnp.dot(h_ref[...], w_ref[:, i * GROUP:i * GROUP + width], preferred_element_type=F32)

    c = jnp.tile(c_ref[...], (1, GROUP // LANES))
    sa = jnp.tile(sa_ref[...], (1, GROUP // LANES))
    sb = jnp.tile(sb_ref[...], (1, GROUP // LANES))

    qa_ref[...] = _silu(grp(0))
    lb = lb_ref[...]
    fg = lb + (1.0 - lb) * jax.nn.sigmoid(grp(1))
    ka_ref[...] = 1.0 - fg
    lf_ref[...] = jnp.log(fg)
    ia_ref[...] = grp(2)
    ga_ref[...] = _silu(grp(3))
    qb_ref[...] = _rope(grp(4), c, sa, sb)
    kb_ref[...] = _rope(grp(5), c, sa, sb)
    vb_ref[...] = grp(6)
    qi_ref[...] = _rope(grp(7), c, sa, sb)
    kw = grp(8, LANES)
    lane = lax.broadcasted_iota(I32, kw.shape, 1)
    roped = _rope(kw, c_ref[...], sa_ref[...], sb_ref[...])
    kw_ref[...] = jnp.where(lane < D_I, roped, kw * (H_I ** -0.5 * D_I ** -0.5))


def _proj_in(x2d, gain, w_pad, lb, cos_t, sin_a, sin_b, tm):
    m, d = x2d.shape
    n_pad = w_pad.shape[1]
    row = lambda i: (i, 0)
    const = lambda i: (0, 0)
    big = pl.BlockSpec((tm, GROUP), row)
    out_shapes = [jax.ShapeDtypeStruct((m, GROUP), F32)] * 9 + [jax.ShapeDtypeStruct((m, LANES), F32)]
    return pl.pallas_call(
        _proj_in_kernel,
        grid=(m // tm,),
        in_specs=[
            pl.BlockSpec((tm, d), row),
            pl.BlockSpec((1, d), const),
            pl.BlockSpec((d, n_pad), const, pipeline_mode=pl.Buffered(1)),
            pl.BlockSpec((1, GROUP), const),
            pl.BlockSpec((tm, LANES), row),
            pl.BlockSpec((tm, LANES), row),
            pl.BlockSpec((tm, LANES), row),
        ],
        out_specs=[big] * 9 + [pl.BlockSpec((tm, LANES), row)],
        out_shape=out_shapes,
        scratch_shapes=[pltpu.VMEM((tm, d), BF16)],
        compiler_params=pltpu.CompilerParams(
            dimension_semantics=("arbitrary",), vmem_limit_bytes=VMEM_LIMIT_BYTES),
        name="proj_in",
    )(x2d, gain, w_pad, lb, cos_t, sin_a, sin_b)


def _shift_rows(x, d):
    if d == 0:
        return x
    return pltpu.roll(x, d, 0)


def _cumsum_rows(g):
    c = g.shape[0]
    row = lax.broadcasted_iota(I32, g.shape, 0)
    k = 1
    while k < c:
        g = g + jnp.where(row >= k, _shift_rows(g, k), 0.0)
        k *= 2
    return g


def _hgrn_kernel(q_ref, k_ref, lf_ref, v_ref, ga_ref, gain_ref, s0_ref, o_ref, sfin_ref, st_ref, *, chunk, sub):
    ci = pl.program_id(1)

    @pl.when(ci == 0)
    def _():
        st_ref[...] = s0_ref[0]

    nsb = chunk // sub
    row = lax.broadcasted_iota(I32, (chunk, DK_A), 0)
    row_in_sub = row & (sub - 1)
    gain = gain_ref[...]

    for h in range(H_A):
        sl = slice(h * DK_A, (h + 1) * DK_A)
        q = q_ref[0, :, sl]
        k = k_ref[0, :, sl]
        v = v_ref[0, :, sl]
        b = _cumsum_rows(lf_ref[0, :, sl])
        st = st_ref[h]

        refs = [jnp.zeros((1, DK_A), F32)] + [b[i * sub - 1:i * sub, :] for i in range(1, nsb)]
        ref_rows = jnp.concatenate([jnp.broadcast_to(r, (sub, DK_A)) for r in refs], axis=0)
        q_rel = q * jnp.exp(b - ref_rows)

        o = lax.dot_general((q * jnp.exp(b)).astype(BF16), st.astype(BF16),
                            (((1,), (1,)), ((), ())), preferred_element_type=F32)

        off_rows = [jnp.zeros((sub, DK_A), F32)]
        for i in range(1, nsb):
            n_prev = i * sub
            k_rel = (k[:n_prev] * jnp.exp(refs[i] - b[:n_prev])).astype(BF16)
            att = lax.dot_general(q_rel[n_prev:n_prev + sub].astype(BF16), k_rel,
                                  (((1,), (1,)), ((), ())), preferred_element_type=F32)
            off_rows.append(jnp.dot(att.astype(BF16), v[:n_prev].astype(BF16), preferred_element_type=F32))
        if nsb > 1:
            o = o + jnp.concatenate(off_rows, axis=0)

        for d in range(sub):
            valid = row_in_sub >= d
            e = jnp.exp(jnp.where(valid, b - _shift_rows(b, d), 0.0))
            w = jnp.sum(q * _shift_rows(k, d) * e, axis=1, keepdims=True)
            o = o + jnp.where(valid, w * _shift_rows(v, d), 0.0)

        b_last = b[chunk - 1:chunk, :]
        k_dec = (k * jnp.exp(b_last - b)).astype(BF16)
        st_ref[h] = st * jnp.exp(b_last) + lax.dot_general(
            v.astype(BF16), k_dec, (((0,), (0,)), ((), ())), preferred_element_type=F32)

        o_ref[0, :, sl] = _rms(o, gain) * ga_ref[0, :, sl]

    @pl.when(ci == pl.num_programs(1) - 1)
    def _():
        sfin_ref[0] = st_ref[...]


def _hgrn(q, k, lf, v, ga, gain, s0_t, chunk, sub):
    b, t, w = q.shape
    blk = pl.BlockSpec((1, chunk, w), lambda bi, ci: (bi, ci, 0))
    st_spec = pl.BlockSpec((1, H_A, DK_A, DK_A), lambda bi, ci: (bi, 0, 0, 0))
    return pl.pallas_call(
        functools.partial(_hgrn_kernel, chunk=chunk, sub=sub),
        grid=(b, t // chunk),
        in_specs=[blk, blk, blk, blk, blk, pl.BlockSpec((1, DK_A), lambda bi, ci: (0, 0)), st_spec],
        out_specs=[blk, st_spec],
        out_shape=[jax.ShapeDtypeStruct((b, t, w), F32), jax.ShapeDtypeStruct((b, H_A, DK_A, DK_A), F32)],
        scratch_shapes=[pltpu.VMEM((H_A, DK_A, DK_A), F32)],
        compiler_params=pltpu.CompilerParams(
            dimension_semantics=("arbitrary", "arbitrary"), vmem_limit_bytes=VMEM_LIMIT_BYTES),
        name="hgrn",
    )(q, k, lf, v, ga, gain, s0_t)


def _rep_sum(x):
    return jnp.broadcast_to(jnp.sum(x, axis=1, keepdims=True), x.shape)


def _rep_max(x):
    return jnp.broadcast_to(jnp.max(x, axis=1, keepdims=True), x.shape)


def _rep_min(x):
    return jnp.broadcast_to(jnp.min(x, axis=1, keepdims=True), x.shape)


def _any_true(flag):
    return jnp.max(flag) > 0.5


def _lane_tiles(x):
    return [x[:, u * LANES:(u + 1) * LANES] for u in range(x.shape[1] // LANES)]


def _select_rows(sc_ref, nkb, n_hidden_tail, n_sel):
    _, r, w = sc_ref.shape
    shape = (r, LANES)
    neg = jnp.float32(NEG_SCORE)
    inf = jnp.float32(jnp.inf)
    kf = jnp.float32(n_sel)
    tail = n_hidden_tail.astype(F32)
    zeros = jnp.zeros(shape, F32)
    lane = lax.broadcasted_iota(I32, shape, 1)

    def count_ge(c):
        def body(j, acc):
            for xu in _lane_tiles(sc_ref[j]):
                acc = acc + jnp.where(xu >= c, 1.0, 0.0)
            return acc
        return _rep_sum(lax.fori_loop(0, nkb, body, zeros)) + jnp.where(neg >= c, tail, 0.0)

    def stats_body(j, carry):
        vmax, vmin_real, n_real = carry
        for x in _lane_tiles(sc_ref[j]):
            real = x > neg
            vmax = jnp.maximum(vmax, x)
            vmin_real = jnp.minimum(vmin_real, jnp.where(real, x, inf))
            n_real = n_real + jnp.where(real, 1.0, 0.0)
        return vmax, vmin_real, n_real

    def hidden_stats_body(j, carry):
        n_ge_neg, gmin = carry
        for x in _lane_tiles(sc_ref[j]):
            n_ge_neg = n_ge_neg + jnp.where(x >= neg, 1.0, 0.0)
            gmin = jnp.minimum(gmin, jnp.where(x > -inf, x, inf))
        return n_ge_neg, gmin

    vmax, vmin_real, n_real = lax.fori_loop(0, nkb, stats_body, (zeros - inf, zeros + inf, zeros))
    few = _any_true(jnp.where(_rep_sum(n_real) < kf, 1.0, 0.0))
    n_ge_neg, gmin = lax.fori_loop(0, jnp.where(few, nkb, 0), hidden_stats_body, (zeros, zeros + inf))
    has_tail = tail > 0.5
    vmax = _rep_max(vmax)
    vmax = jnp.where(has_tail, jnp.maximum(vmax, neg), vmax)
    vmin_real = _rep_min(vmin_real)
    gmin = _rep_min(gmin)
    gmin = jnp.where(has_tail, jnp.minimum(gmin, neg), gmin)
    n_real = _rep_sum(n_real)
    n_ge_neg = _rep_sum(n_ge_neg) + tail
    c_max = count_ge(vmax)

    at_max = c_max >= kf
    few_real = jnp.logical_and(jnp.logical_not(at_max), n_real < kf)
    at_neg = jnp.logical_and(few_real, n_ge_neg >= kf)
    below_neg = jnp.logical_and(few_real, n_ge_neg < kf)

    done = jnp.where(jnp.logical_or(at_max, at_neg), 1.0, 0.0)
    thr = jnp.where(at_max, vmax, jnp.where(at_neg, neg, zeros))
    n_gt = jnp.where(at_neg, n_real, zeros)
    n_ge = jnp.where(at_max, c_max, jnp.where(at_neg, n_ge_neg, zeros))
    tie = jnp.where(jnp.logical_and(done > 0.5, n_ge > kf), 1.0, 0.0)
    lo = jnp.where(below_neg, gmin, vmin_real)
    hi = jnp.where(below_neg, neg, vmax)
    c_hi = jnp.where(below_neg, n_ge_neg, c_max)

    def bisect(st):
        done, thr, n_gt, tie, lo, hi, c_hi = st
        mid = lo + (hi - lo) * 0.5
        c = count_ge(mid)
        live = done < 0.5
        hit = jnp.logical_and(live, c == kf)
        up = jnp.logical_and(live, c > kf)
        dn = jnp.logical_and(live, c < kf)
        return (jnp.where(hit, 1.0, done), jnp.where(hit, mid, thr), n_gt, tie,
                jnp.where(up, mid, lo), jnp.where(dn, mid, hi), jnp.where(dn, c, c_hi))

    def snap(st):
        done, thr, n_gt, tie, lo, hi, c_hi = st

        def body(j, m):
            for xu in _lane_tiles(sc_ref[j]):
                m = jnp.maximum(m, jnp.where(xu < hi, xu, -inf))
            return m
        below = _rep_max(lax.fori_loop(0, nkb, body, zeros - inf))
        below = jnp.where(jnp.logical_and(tail > 0.5, neg < hi), jnp.maximum(below, neg), below)
        c = count_ge(below)
        live = done < 0.5
        fin = jnp.logical_and(live, c >= kf)
        mv = jnp.logical_and(live, c < kf)
        return (jnp.where(fin, 1.0, done), jnp.where(fin, below, thr), jnp.where(fin, c_hi, n_gt),
                jnp.where(fin, jnp.where(c > kf, 1.0, 0.0), tie),
                lo, jnp.where(mv, below, hi), jnp.where(mv, c, c_hi))

    def not_done(st):
        return _any_true(1.0 - st[0])

    st = (done, thr, n_gt, tie, lo, hi, c_hi)
    st = lax.fori_loop(0, BISECT_BLIND_PASSES, lambda _, s: bisect(s), st)
    _, st = lax.while_loop(
        lambda ps: jnp.logical_and(ps[0] < BISECT_TESTED_ROUNDS, not_done(ps[1])),
        lambda ps: (ps[0] + 1, bisect(bisect(ps[1]))), (jnp.int32(0), st))
    st = lax.while_loop(not_done, lambda s: snap(bisect(s)), st)
    done, thr, n_gt, tie, lo, hi, c_hi = st

    need = kf - n_gt
    n_stored = nkb * w

    def count_eq_upto(jb):
        def body(j, acc):
            for u, xu in enumerate(_lane_tiles(sc_ref[j])):
                idx = j * w + u * LANES + lane
                acc = acc + jnp.where(jnp.logical_and(xu == thr, idx <= jb), 1.0, 0.0)
            return acc
        return _rep_sum(lax.fori_loop(0, nkb, body, zeros))

    def jstep(pj):
        p, (jlo, jhi) = pj
        jmid = jlo + lax.shift_right_arithmetic(jhi - jlo, 1)
        ok = count_eq_upto(jmid) >= need
        return p + 1, (jnp.where(ok, jlo, jmid), jnp.where(ok, jmid, jhi))

    n_jpass = jnp.where(_any_true(tie), 16, 0)
    jlo0 = jnp.full(shape, -1, I32)
    jhi0 = jnp.zeros(shape, I32) + n_stored
    _, (_, jhi) = lax.while_loop(lambda pj: pj[0] < n_jpass, jstep, (jnp.int32(0), (jlo0, jhi0)))
    jmax = jnp.where(tie > 0.5, jhi, jnp.int32(2 ** 30))
    return thr, jmax


def _attn_prompt_kernel(qi_ref, kw_ref, qb_ref, kit_ref, kt_ref, v_ref, o_ref,
                        sc_ref, wb_ref, qc_ref, qh_ref, sa_ref, sb_ref, mxa_ref, mxb_ref, ba_ref, bb_ref,
                        pa_ref, pb_ref,
                        m_ref, l_ref, acc_ref,
                        *, n_keys, n_sel):
    tq, tk = ATT_TQ, ATT_TK
    tiles = [slice(u * LANES, (u + 1) * LANES) for u in range(tk // LANES)]
    i = pl.program_id(0)
    nkb = (i * tq + tq + tk - 1) // tk
    last_blk = n_keys // tk - 1
    q_pos = i * tq + lax.broadcasted_iota(I32, (tq, LANES), 0)
    lane = lax.broadcasted_iota(I32, (tq, LANES), 1)

    @pl.when(i == 0)
    def _():
        sc_ref[...] = jnp.full(sc_ref.shape, NEG_SCORE, F32)

    kw = kw_ref[...]
    for h in range(H_I):
        hi, lo = _split_bf16(qi_ref[:, h * D_I:(h + 1) * D_I])
        qc_ref[h] = jnp.concatenate([hi, lo, hi], axis=1)
        wb_ref[h] = jnp.broadcast_to(kw[:, D_I + h:D_I + h + 1], (tq, LANES))

    def score_body(j, _):
        kit = kit_ref[j]
        for h in range(H_I):
            d = jnp.dot(qc_ref[h], kit, preferred_element_type=F32)
            wbh = wb_ref[h]
            for u, us in enumerate(tiles):
                val = wbh * jnp.maximum(d[:, us], 0.0)
                if h > 0:
                    val = sc_ref[j, :, us] + val
                if h == H_I - 1:
                    val = jnp.where(j * tk + u * LANES + lane <= q_pos, val, NEG_SCORE)
                sc_ref[j, :, us] = val
        return 0

    lax.fori_loop(0, nkb, score_body, 0)

    tail = jnp.zeros((tq, LANES), I32) + (n_keys - nkb * tk)
    thr, jmax = _select_rows(sc_ref, nkb, tail, n_sel)

    m_ref[...] = jnp.full(m_ref.shape, MASK_VALUE, F32)
    l_ref[...] = jnp.zeros(l_ref.shape, F32)
    acc_ref[...] = jnp.zeros(acc_ref.shape, F32)
    for h in range(H_B):
        qh_ref[h] = (qb_ref[:, h * HD_B:(h + 1) * HD_B] * (HD_B ** -0.5 * LOG2_E)).astype(BF16)
    first_half = lane < HD_B

    def selection_bias(blk, b_ref):
        for u, us in enumerate(tiles):
            xu = sc_ref[blk, :, us]
            idx = blk * tk + u * LANES + lane
            sel = jnp.logical_or(xu > thr, jnp.logical_and(xu == thr, idx <= jmax))
            sel = jnp.logical_and(sel, idx <= q_pos)
            b_ref[:, us] = jnp.where(sel, 0.0, MASK_VALUE)

    def masked_logits(blk, h, s_ref, mx_ref, b_ref):
        s = jnp.dot(qh_ref[h], kt_ref[blk, h * HD_B:(h + 1) * HD_B, :], preferred_element_type=F32) + b_ref[...]
        s_ref[h] = s
        mx = s[:, tiles[0]]
        for us in tiles[1:]:
            mx = jnp.maximum(mx, s[:, us])
        mx_ref[h] = mx

    def reduce_block(blk, s_ref, mx_ref, nxt, b_ref, p_ref):
        rows = pl.ds(pl.multiple_of(blk * tk, tk), tk)
        for pair in range(H_B // 2):
            ps = slice(pair * LANES, (pair + 1) * LANES)
            v_pair = v_ref[rows, ps]
            for half in range(2):
                h = 2 * pair + half
                m_old = m_ref[h]
                m_new = jnp.maximum(m_old, _rep_max(mx_ref[h]))
                alpha = jnp.exp2(m_old - m_new)
                acc_ref[h] = alpha * acc_ref[h]
                l_new = alpha * l_ref[h]
                for us in tiles:
                    pt = jnp.exp2(s_ref[h, :, us] - m_new)
                    l_new = l_new + pt
                    p_ref[h, :, us] = pt.astype(BF16)
                l_ref[h] = l_new
                m_ref[h] = m_new
                masked_logits(nxt, h, s_ref, mx_ref, b_ref)
                acc_ref[h] += jnp.dot(p_ref[h], v_pair, preferred_element_type=F32)

    selection_bias(0, ba_ref)
    selection_bias(1, bb_ref)
    for h in range(H_B):
        masked_logits(0, h, sa_ref, mxa_ref, ba_ref)
        masked_logits(1, h, sb_ref, mxb_ref, bb_ref)

    def pair_body(jj, _):
        a = 2 * jj
        nxt_a = jnp.minimum(a + 2, last_blk)
        nxt_b = jnp.minimum(a + 3, last_blk)
        selection_bias(nxt_a, ba_ref)
        selection_bias(nxt_b, bb_ref)
        reduce_block(a, sa_ref, mxa_ref, nxt_a, ba_ref, pa_ref)
        reduce_block(a + 1, sb_ref, mxb_ref, nxt_b, bb_ref, pb_ref)
        return 0

    lax.fori_loop(0, (nkb + 1) // 2, pair_body, 0)

    for pair in range(H_B // 2):
        ps = slice(pair * LANES, (pair + 1) * LANES)
        d0 = jnp.broadcast_to(jnp.sum(l_ref[2 * pair], axis=1, keepdims=True), (tq, LANES))
        d1 = jnp.broadcast_to(jnp.sum(l_ref[2 * pair + 1], axis=1, keepdims=True), (tq, LANES))
        o_ref[:, ps] = jnp.where(first_half, acc_ref[2 * pair] / d0, acc_ref[2 * pair + 1] / d1)


def _attn_prompt(qi, kw, qb, kit_blocks, kt_blocks, v_bf, n_sel):
    s = qi.shape[0]
    nkb_total = s // ATT_TK
    row = lambda i: (i, 0)
    whole = pl.BlockSpec(memory_space=pltpu.VMEM)
    return pl.pallas_call(
        functools.partial(_attn_prompt_kernel, n_keys=s, n_sel=n_sel),
        grid=(s // ATT_TQ,),
        in_specs=[
            pl.BlockSpec((ATT_TQ, GROUP), row),
            pl.BlockSpec((ATT_TQ, LANES), row),
            pl.BlockSpec((ATT_TQ, GROUP), row),
            whole, whole, whole,
        ],
        out_specs=pl.BlockSpec((ATT_TQ, GROUP), row),
        out_shape=jax.ShapeDtypeStruct((s, GROUP), F32),
        scratch_shapes=[
            pltpu.VMEM((nkb_total, ATT_TQ, ATT_TK), F32),
            pltpu.VMEM((H_I, ATT_TQ, LANES), F32),
            pltpu.VMEM((H_I, ATT_TQ, 3 * D_I), BF16),
            pltpu.VMEM((H_B, ATT_TQ, HD_B), BF16),
            pltpu.VMEM((H_B, ATT_TQ, ATT_TK), F32),
            pltpu.VMEM((H_B, ATT_TQ, ATT_TK), F32),
            pltpu.VMEM((H_B, ATT_TQ, LANES), F32),
            pltpu.VMEM((H_B, ATT_TQ, LANES), F32),
            pltpu.VMEM((ATT_TQ, ATT_TK), F32),
            pltpu.VMEM((ATT_TQ, ATT_TK), F32),
            pltpu.VMEM((H_B, ATT_TQ, ATT_TK), BF16),
            pltpu.VMEM((H_B, ATT_TQ, ATT_TK), BF16),
            pltpu.VMEM((H_B, ATT_TQ, LANES), F32),
            pltpu.VMEM((H_B, ATT_TQ, LANES), F32),
            pltpu.VMEM((H_B, ATT_TQ, LANES), F32),
        ],
        compiler_params=pltpu.CompilerParams(
            dimension_semantics=("arbitrary",), vmem_limit_bytes=VMEM_LIMIT_BYTES),
        name="attn_prompt",
    )(qi, kw, qb, kit_blocks, kt_blocks, v_bf)


def _attn_sample_kernel(pt_ref, qi_ref, kw_ref, qb_ref, kin_ref, kn_ref, vn_ref,
                        ckidx_hbm, ck_hbm, cv_hbm, o_ref,
                        kid_buf, k_buf, v_buf, sc_ref, m_ref, l_ref, acc_ref, sem_i, sem_k, sem_v,
                        *, n_pages, t_new, n_sel):
    b = pl.program_id(0)
    r = SUBLANES
    nrow = H_B * r
    n_chunks = n_pages // KV_CHUNK_PAGES

    def page_lanes(p):
        start = (p % KV_CHUNK_PAGES) * PAGE_SIZE
        return pl.ds(start if isinstance(start, int) else pl.multiple_of(start, PAGE_SIZE), PAGE_SIZE)

    def kidx_copy(p):
        return pltpu.make_async_copy(ckidx_hbm.at[pt_ref[b, p]],
                                     kid_buf.at[p // KV_CHUNK_PAGES, :, page_lanes(p)], sem_i)

    def kv_copies(c, slot, p):
        page = pt_ref[b, c * KV_CHUNK_PAGES + p]
        return (pltpu.make_async_copy(ck_hbm.at[page], k_buf.at[slot, :, :, page_lanes(p)], sem_k.at[slot]),
                pltpu.make_async_copy(cv_hbm.at[page], v_buf.at[slot, :, :, page_lanes(p)], sem_v.at[slot]))

    def start_chunk(c, slot):
        for p in range(KV_CHUNK_PAGES):
            ck, cv = kv_copies(c, slot, p)
            ck.start()
            cv.start()

    def wait_chunk(c, slot):
        for p in range(KV_CHUNK_PAGES):
            ck, cv = kv_copies(c, slot, p)
            ck.wait()
            cv.wait()

    def start_kidx(p, _):
        kidx_copy(p).start()
        return 0

    def wait_kidx(p, _):
        kidx_copy(p).wait()
        return 0

    lax.fori_loop(0, n_pages, start_kidx, 0)
    start_chunk(0, 0)

    kw = kw_ref[0]
    qi = qi_ref[0]
    q_hi, q_lo, w_rows = [], [], []
    for h in range(H_I):
        hi, lo = _split_bf16(qi[:, h * D_I:(h + 1) * D_I])
        q_hi.append(hi)
        q_lo.append(lo)
        w_rows.append(jnp.broadcast_to(kw[:, D_I + h:D_I + h + 1], (r, LANES)))
    q_hi = jnp.concatenate(q_hi, axis=0)
    q_hl = jnp.concatenate([q_hi, jnp.concatenate(q_lo, axis=0)], axis=0)
    w_rows = jnp.concatenate(w_rows, axis=0)
    nt = (((1,), (1,)), ((), ()))

    def key_scores(keys_t):
        k_hi, k_lo = _split_bf16(keys_t)
        d2 = jnp.dot(q_hl, k_hi, preferred_element_type=F32)
        d = d2[:H_I * r] + d2[H_I * r:] + jnp.dot(q_hi, k_lo, preferred_element_type=F32)
        out = []
        for u in range(keys_t.shape[1] // LANES):
            us = slice(u * LANES, (u + 1) * LANES)
            wd = w_rows * jnp.maximum(d[:, us], 0.0)
            acc = wd[0:r]
            for h in range(1, H_I):
                acc = acc + wd[h * r:(h + 1) * r]
            out.append(acc)
        return out

    lax.fori_loop(0, n_pages, wait_kidx, 0)

    def score_body(c, _):
        for u, tile in enumerate(key_scores(kid_buf[c])):
            sc_ref[c, :, u * LANES:(u + 1) * LANES] = tile
        return 0

    lax.fori_loop(0, n_chunks, score_body, 0)

    lane = lax.broadcasted_iota(I32, (r, LANES), 1)
    t_row = jnp.minimum(lax.broadcasted_iota(I32, (r, LANES), 0), t_new - 1)
    new_sc = key_scores(kin_ref[0])[0]
    new_vis = lane <= t_row
    sc_ref[n_chunks] = jnp.full((r, KV_CHUNK_PAGES * LANES), -jnp.inf, F32)
    sc_ref[n_chunks, :, 0:LANES] = jnp.where(lane >= t_new, -jnp.inf, jnp.where(new_vis, new_sc, NEG_SCORE))

    thr, jmax = _select_rows(sc_ref, n_chunks + 1, jnp.zeros((r, LANES), I32), n_sel)

    qb = qb_ref[0] * (HD_B ** -0.5)
    col_head = lax.shift_right_logical(lax.broadcasted_iota(I32, (r, GROUP), 1), HD_B.bit_length() - 1)
    q_bd = jnp.concatenate([jnp.where(col_head == h, qb, 0.0) for h in range(H_B)], axis=0).astype(BF16)

    m_ref[...] = jnp.full(m_ref.shape, MASK_VALUE, F32)
    l_ref[...] = jnp.zeros(l_ref.shape, F32)
    acc_ref[...] = jnp.zeros(acc_ref.shape, F32)

    def attend(blk, n_blk, k_t, v_t, extra_mask):
        bias = []
        for u in range(n_blk):
            x = sc_ref[blk, :, u * LANES:(u + 1) * LANES]
            idx = (blk * KV_CHUNK_PAGES + u) * LANES + lane
            sel = jnp.logical_or(x > thr, jnp.logical_and(x == thr, idx <= jmax))
            if extra_mask is not None:
                sel = jnp.logical_and(sel, extra_mask)
            bias.append(jnp.where(sel, 0.0, MASK_VALUE))
        bias = jnp.tile(jnp.concatenate(bias, axis=1), (H_B, 1))
        s = jnp.dot(q_bd, k_t.astype(BF16), preferred_element_type=F32) + bias
        m_old = m_ref[...]
        m_new = jnp.maximum(m_old, jnp.broadcast_to(jnp.max(s, axis=1, keepdims=True), m_old.shape))
        alpha = jnp.exp(m_old - m_new)
        p = jnp.exp(s - jnp.tile(m_new, (1, n_blk)))
        p_sum = p[:, :LANES]
        for u in range(1, n_blk):
            p_sum = p_sum + p[:, u * LANES:(u + 1) * LANES]
        l_ref[...] = alpha * l_ref[...] + p_sum
        pv = lax.dot_general(p.astype(BF16), v_t.astype(BF16), nt, preferred_element_type=F32)
        acc_ref[...] = jnp.tile(alpha, (1, GROUP // LANES)) * acc_ref[...] + pv
        m_ref[...] = m_new

    def chunk_body(c, _):
        slot = c % 2
        wait_chunk(c, slot)

        @pl.when(c + 1 < n_chunks)
        def _():
            start_chunk(c + 1, 1 - slot)

        n_keys_chunk = KV_CHUNK_PAGES * PAGE_SIZE
        attend(c, KV_CHUNK_PAGES, k_buf[slot].reshape(GROUP, n_keys_chunk),
               v_buf[slot].reshape(GROUP, n_keys_chunk), None)
        return 0

    lax.fori_loop(0, n_chunks, chunk_body, 0)
    attend(n_chunks, 1, kn_ref[0].reshape(GROUP, PAGE_SIZE), vn_ref[0].reshape(GROUP, PAGE_SIZE), new_vis)

    out = jnp.zeros((r, GROUP), F32)
    for h in range(H_B):
        rs = slice(h * r, (h + 1) * r)
        denom = jnp.sum(l_ref[rs, :], axis=1, keepdims=True)
        out = out + jnp.where(col_head == h, acc_ref[rs, :] / denom, 0.0)
    o_ref[0] = out


def _attn_sample(page_table, qi, kw, qb, ki_new, k_new, v_new, cache_kidx, cache_k, cache_v, t_new, n_sel):
    nb, n_pages = page_table.shape
    r = SUBLANES
    nrow = H_B * r
    per_b = lambda b, pt: (b, 0, 0)
    per_b4 = lambda b, pt: (b, 0, 0, 0)
    anyspec = pl.BlockSpec(memory_space=pl.ANY)
    chunk_keys = KV_CHUNK_PAGES * PAGE_SIZE
    n_chunks = n_pages // KV_CHUNK_PAGES
    grid_spec = pltpu.PrefetchScalarGridSpec(
        num_scalar_prefetch=1,
        grid=(nb,),
        in_specs=[
            pl.BlockSpec((1, r, GROUP), per_b),
            pl.BlockSpec((1, r, LANES), per_b),
            pl.BlockSpec((1, r, GROUP), per_b),
            pl.BlockSpec((1, D_I, PAGE_SIZE), per_b),
            pl.BlockSpec((1, H_B, HD_B, PAGE_SIZE), per_b4),
            pl.BlockSpec((1, H_B, HD_B, PAGE_SIZE), per_b4),
            anyspec, anyspec, anyspec,
        ],
        out_specs=pl.BlockSpec((1, r, GROUP), per_b),
        scratch_shapes=[
            pltpu.VMEM((n_chunks, D_I, chunk_keys), F32),
            pltpu.VMEM((2, H_B, HD_B, chunk_keys), F32),
            pltpu.VMEM((2, H_B, HD_B, chunk_keys), F32),
            pltpu.VMEM((n_chunks + 1, r, chunk_keys), F32),
            pltpu.VMEM((nrow, LANES), F32),
            pltpu.VMEM((nrow, LANES), F32),
            pltpu.VMEM((nrow, GROUP), F32),
            pltpu.SemaphoreType.DMA(()),
            pltpu.SemaphoreType.DMA((2,)),
            pltpu.SemaphoreType.DMA((2,)),
        ],
    )
    return pl.pallas_call(
        functools.partial(_attn_sample_kernel, n_pages=n_pages, t_new=t_new, n_sel=n_sel),
        grid_spec=grid_spec,
        out_shape=jax.ShapeDtypeStruct((nb, r, GROUP), F32),
        compiler_params=pltpu.CompilerParams(
            dimension_semantics=("arbitrary",), vmem_limit_bytes=VMEM_LIMIT_BYTES),
        name="attn_sample",
    )(page_table, qi, kw, qb, ki_new, k_new, v_new, cache_kidx, cache_k, cache_v)


def _out_ffn_kernel(x_ref, oa_ref, ob_ref, p1_ref, p2_ref, wo_ref, g2_ref, wu_ref, cw_ref, cb_ref, wd_ref, gf_ref,
                    y_ref, u_ref, carry_ref, x1_ref, h2_ref, *, d_ff, ff_tile, seq_rows, carry_mode):
    tm = x_ref.shape[0]
    i = pl.program_id(0)
    mixed = jnp.concatenate([oa_ref[...], ob_ref[...]], axis=1).astype(BF16)
    x1 = x_ref[...] + jnp.dot(mixed, wo_ref[...], preferred_element_type=F32)
    h2_ref[...] = _rms(x1, g2_ref[...]).astype(BF16)
    x1_ref[...] = x1

    row = lax.broadcasted_iota(I32, (tm, ff_tile), 0)
    if carry_mode:
        @pl.when(i == 0)
        def _():
            carry_ref[...] = jnp.zeros(carry_ref.shape, F32)
        t_in_seq = row
    else:
        t_in_seq = row & (seq_rows - 1)

    for c in range(d_ff // ff_tile):
        cs = slice(c * ff_tile, (c + 1) * ff_tile)
        h2 = h2_ref[...]
        u = jnp.dot(h2, wu_ref[:, cs], preferred_element_type=F32)
        v = jnp.dot(h2, wu_ref[:, d_ff + c * ff_tile:d_ff + (c + 1) * ff_tile], preferred_element_type=F32)
        if carry_mode:
            prev = carry_ref[:, cs]
            p1 = jnp.broadcast_to(prev[SUBLANES - 1:SUBLANES], (tm, ff_tile))
            p2 = jnp.where(row == 0, jnp.broadcast_to(prev[SUBLANES - 2:SUBLANES - 1], (tm, ff_tile)), p1)
            carry_ref[:, cs] = u[tm - SUBLANES:]
            u_ref[:, cs] = u[tm - SUBLANES:]
        else:
            p1 = p1_ref[:, cs]
            p2 = p2_ref[:, cs]
            u_ref[:, cs] = u
        u1 = jnp.where(t_in_seq >= 1, pltpu.roll(u, 1, 0), p1)
        u2 = jnp.where(t_in_seq >= 2, pltpu.roll(u, 2, 0), p2)
        conv = cb_ref[:, cs] + cw_ref[0:1, cs] * u2 + cw_ref[1:2, cs] * u1 + cw_ref[2:3, cs] * u
        gate = (_silu(conv) * v).astype(BF16)
        x1_ref[...] += jnp.dot(gate, wd_ref[cs, :], preferred_element_type=F32)

    y_ref[...] = _rms(x1_ref[...], gf_ref[...])


def _out_ffn(x2d, oa, ob, p1, p2, w_out, g2, w_up, conv_w, conv_b, w_down, gf, tm, seq_rows, carry_mode):
    m, d = x2d.shape
    d_ff = w_down.shape[0]
    ff_tile = 256
    row = lambda i: (i, 0)
    const = lambda i: (0, 0)
    if carry_mode:
        prev_spec = pl.BlockSpec((SUBLANES, d_ff), const)
        u_spec = pl.BlockSpec((SUBLANES, d_ff), const)
        u_shape = jax.ShapeDtypeStruct((SUBLANES, d_ff), F32)
    else:
        prev_spec = pl.BlockSpec((tm, d_ff), row)
        u_spec = pl.BlockSpec((tm, d_ff), row)
        u_shape = jax.ShapeDtypeStruct((m, d_ff), F32)
    return pl.pallas_call(
        functools.partial(_out_ffn_kernel, d_ff=d_ff, ff_tile=ff_tile, seq_rows=seq_rows, carry_mode=carry_mode),
        grid=(m // tm,),
        in_specs=[
            pl.BlockSpec((tm, d), row),
            pl.BlockSpec((tm, GROUP), row),
            pl.BlockSpec((tm, GROUP), row),
            prev_spec, prev_spec,
            pl.BlockSpec(w_out.shape, const, pipeline_mode=pl.Buffered(1)),
            pl.BlockSpec((1, d), const),
            pl.BlockSpec(w_up.shape, const, pipeline_mode=pl.Buffered(1)),
            pl.BlockSpec(conv_w.shape, const),
            pl.BlockSpec((1, d_ff), const),
            pl.BlockSpec(w_down.shape, const, pipeline_mode=pl.Buffered(1)),
            pl.BlockSpec((1, d), const),
        ],
        out_specs=[pl.BlockSpec((tm, d), row), u_spec],
        out_shape=[jax.ShapeDtypeStruct((m, d), F32), u_shape],
        scratch_shapes=[pltpu.VMEM((SUBLANES, d_ff), F32), pltpu.VMEM((tm, d), F32), pltpu.VMEM((tm, d), BF16)],
        compiler_params=pltpu.CompilerParams(
            dimension_semantics=("arbitrary",), vmem_limit_bytes=VMEM_LIMIT_BYTES),
        name="out_ffn",
    )(x2d, oa, ob, p1, p2, w_out, g2, w_up, conv_w, conv_b, w_down, gf)


def _rope_tables(pos):
    half = ROT // 2
    inv = jnp.power(ROPE_THETA, -jnp.arange(half, dtype=F32) * (2.0 / ROT))
    ang = pos.astype(F32)[:, None] * inv[None, :]
    cos, sin = jnp.cos(ang), jnp.sin(ang)
    n = pos.shape[0]
    ones = jnp.ones((n, HD_B - ROT), F32)
    zeros = jnp.zeros((n, HD_B - ROT), F32)
    zh = jnp.zeros((n, half), F32)
    c = jnp.concatenate([cos, cos, ones], axis=1)
    sa = jnp.concatenate([zh, sin, zeros], axis=1)
    sb = jnp.concatenate([-sin, zh, zeros], axis=1)
    rep = LANES // HD_B
    return jnp.tile(c, (1, rep)), jnp.tile(sa, (1, rep)), jnp.tile(sb, (1, rep))


def kernel(x_prompt, x_sample, cache_k, cache_v, cache_kidx, state_hgrn, state_conv, page_table,
           norm_mix_gain, w_in, lb_logits, hgrn_norm_gain, w_out, norm_ffn_gain, w_up, conv_w,
           conv_b, w_down, final_norm_gain):
    bp, s, d = x_prompt.shape
    nb, t_new, _ = x_sample.shape
    depth = w_in.shape[0]
    assert bp == 1 and depth == 1
    n_pages = page_table.shape[1]
    past = n_pages * PAGE_SIZE
    d_ff = w_down.shape[1]
    assert s % ATT_TK == 0 and s % HGRN_CHUNK == 0 and n_pages % KV_CHUNK_PAGES == 0
    assert t_new <= SUBLANES and CONV_W - 1 <= t_new

    lower_bounds = jnp.cumsum(jax.nn.softmax(lb_logits.astype(F32), axis=0), axis=0)
    lb = lower_bounds[0][None, :]
    n_in = w_in.shape[2]
    n_pad = 8 * GROUP + LANES
    w_in_bf = jnp.pad(w_in[0], ((0, 0), (0, n_pad - n_in))).astype(BF16)
    w_out_bf = w_out[0].astype(BF16)
    w_up_bf = w_up[0].astype(BF16)
    w_down_bf = w_down[0].astype(BF16)
    g_mix = norm_mix_gain[0][None, :]
    g_ffn = norm_ffn_gain[0][None, :]
    g_fin = final_norm_gain[None, :]
    g_hgrn = hgrn_norm_gain[0][None, :]
    conv_b2 = conv_b[0][None, :]

    xp = x_prompt.reshape(s, d)
    cp, sap, sbp = _rope_tables(jnp.arange(s, dtype=I32))
    qa, ka, lf, ia, ga, qb, kb, vb, qi, kw = _proj_in(xp, g_mix, w_in_bf, lb, cp, sap, sbp, tm=256)

    s0 = jnp.zeros((1, H_A, DK_A, DK_A), F32)
    r3 = lambda a: a.reshape(1, s, GROUP)
    oa_p, st_p = _hgrn(r3(qa), r3(ka), r3(lf), r3(ia), r3(ga), g_hgrn, s0, HGRN_CHUNK, HGRN_SUB)

    nkb = s // ATT_TK
    ki_hi, ki_lo = _split_bf16(kw[:, :D_I])
    kit = jnp.concatenate([ki_hi, ki_hi, ki_lo], axis=1)
    kit_blocks = kit.reshape(nkb, ATT_TK, 3 * D_I).transpose(0, 2, 1)
    kt_blocks = kb.astype(BF16).reshape(nkb, ATT_TK, GROUP).transpose(0, 2, 1)
    ob_p = _attn_prompt(qi, kw, qb, kit_blocks, kt_blocks, vb.astype(BF16), min(TOPK_MAX, s // 4))

    zero_prev = jnp.zeros((SUBLANES, d_ff), F32)
    y_p, u_tail = _out_ffn(xp, oa_p.reshape(s, GROUP), ob_p, zero_prev, zero_prev, w_out_bf, g_ffn, w_up_bf,
                           conv_w[0], conv_b2, w_down_bf, g_fin, tm=256, seq_rows=s, carry_mode=True)

    ms = nb * t_new
    xs = x_sample.reshape(ms, d)
    pos_s = jnp.tile(past + jnp.arange(t_new, dtype=I32), nb)
    cs, sas, sbs = _rope_tables(pos_s)
    qa2, ka2, lf2, ia2, ga2, qb2, kb2, vb2, qi2, kw2 = _proj_in(xs, g_mix, w_in_bf, lb, cs, sas, sbs, tm=ms)

    pad_front = SUBLANES - t_new
    fp = lambda a: jnp.pad(a.reshape(nb, t_new, GROUP), ((0, 0), (pad_front, 0), (0, 0)))
    s0_s = jnp.swapaxes(state_hgrn[0], -1, -2)
    oa_s, st_s = _hgrn(fp(qa2), fp(ka2), fp(lf2), fp(ia2), fp(ga2), g_hgrn, s0_s, SUBLANES, SUBLANES)
    oa_s = oa_s[:, pad_front:, :].reshape(ms, GROUP)

    def rp(a):
        a = a.reshape(nb, t_new, a.shape[-1])
        return jnp.concatenate([a, jnp.broadcast_to(a[:, -1:], (nb, SUBLANES - t_new, a.shape[-1]))], axis=1)
    page_pad = lambda a: jnp.pad(a.reshape(nb, t_new, a.shape[-1]), ((0, 0), (0, PAGE_SIZE - t_new), (0, 0)))
    new_page = lambda a: jnp.swapaxes(page_pad(a), 1, 2)
    heads = lambda a: new_page(a).reshape(nb, H_B, HD_B, PAGE_SIZE)
    ob_s = _attn_sample(
        page_table, rp(qi2), rp(kw2), rp(qb2), new_page(kw2[:, :D_I]), heads(kb2), heads(vb2),
        jnp.transpose(cache_kidx[0], (0, 2, 1)), jnp.transpose(cache_k[0], (0, 2, 3, 1)),
        jnp.transpose(cache_v[0], (0, 2, 3, 1)),
        t_new, min(TOPK_MAX, (past + t_new) // 4))
    ob_s = ob_s[:, :t_new, :].reshape(ms, GROUP)

    sc0 = state_conv[0]
    zrow = jnp.zeros((nb, t_new - 1, d_ff), F32)
    p1 = jnp.concatenate([sc0[:, 1:2], zrow], axis=1).reshape(ms, d_ff)
    p2 = jnp.concatenate([sc0[:, 0:2], zrow[:, 1:]], axis=1).reshape(ms, d_ff)
    y_s, u_s = _out_ffn(xs, oa_s, ob_s, p1, p2, w_out_bf, g_ffn, w_up_bf, conv_w[0], conv_b2, w_down_bf, g_fin,
                        tm=ms, seq_rows=t_new, carry_mode=False)

    return (
        y_p.reshape(1, s, d),
        y_s.reshape(nb, t_new, d),
        kb.reshape(1, 1, s, H_B, HD_B),
        vb.reshape(1, 1, s, H_B, HD_B),
        kw[:, :D_I].reshape(1, 1, s, D_I),
        jnp.swapaxes(st_p, -1, -2).reshape(1, 1, H_A, DK_A, DK_A),
        u_tail[SUBLANES - (CONV_W - 1):].reshape(1, 1, CONV_W - 1, d_ff),
        kb2.reshape(1, nb, t_new, H_B, HD_B),
        vb2.reshape(1, nb, t_new, H_B, HD_B),
        kw2[:, :D_I].reshape(1, nb, t_new, D_I),
        jnp.swapaxes(st_s, -1, -2).reshape(1, nb, H_A, DK_A, DK_A),
        u_s.reshape(nb, t_new, d_ff)[:, t_new - (CONV_W - 1):].reshape(1, nb, CONV_W - 1, d_ff),
    )
```

```python
import functools

import jax
import jax.numpy as jnp
from jax import lax
from jax.experimental import pallas as pl
from jax.experimental.pallas import tpu as pltpu

F32 = jnp.float32
BF16 = jnp.bfloat16
I32 = jnp.int32

H_A = 4
DK_A = 128
H_B = 8
HD_B = 64
H_I = 8
D_I = 64
ROT = 16
ROPE_THETA = 500000.0
TOPK_MAX = 256
PAGE_SIZE = 128
CONV_W = 3
EPS = 1e-6
NEG_SCORE = -1e30
MASK_VALUE = -1e30
GROUP = 512
LOG2_E = 1.4426950408889634

LANES = 128
SUBLANES = 8
VMEM_LIMIT_BYTES = 61 * 1024 * 1024

HGRN_CHUNK = 64
HGRN_SUB = 16
BISECT_BLIND_PASSES = 10
BISECT_TESTED_ROUNDS = 8
ATT_TQ = 128
ATT_TK = 512
KV_CHUNK_PAGES = 8
KV_SLOTS = 4


def _silu(x):
    return x * jax.nn.sigmoid(x)


def _rms(x, g):
    return x * lax.rsqrt(jnp.mean(x * x, axis=-1, keepdims=True) + EPS) * g


def _split_bf16(x):
    hi = x.astype(BF16)
    lo = (x - hi.astype(F32)).astype(BF16)
    return hi, lo


def _rope(x, c, sa, sb):
    w = x.shape[1]
    return x * c + pltpu.roll(x, ROT // 2, 1) * sa + pltpu.roll(x, w - ROT // 2, 1) * sb


def _proj_in_kernel(x_ref, g_ref, w_ref, lb_ref, c_ref, sa_ref, sb_ref,
                    qa_ref, ka_ref, lf_ref, ia_ref, ga_ref, qb_ref, kb_ref, vb_ref, qi_ref, kw_ref, h_ref):
    h_ref[...] = _rms(x_ref[...], g_ref[...]).astype(BF16)

    def grp(i, width=GROUP):
        return jnp.dot(h_ref[...], w_ref[:, i * GROUP:i * GROUP + width], preferred_element_type=F32)

    c = jnp.tile(c_ref[...], (1, GROUP // LANES))
    sa = jnp.tile(sa_ref[...], (1, GROUP // LANES))
    sb = jnp.tile(sb_ref[...], (1, GROUP // LANES))

    qa_ref[...] = _silu(grp(0))
    lb = lb_ref[...]
    fg = lb + (1.0 - lb) * jax.nn.sigmoid(grp(1))
    ka_ref[...] = 1.0 - fg
    lf_ref[...] = jnp.log(fg)
    ia_ref[...] = grp(2)
    ga_ref[...] = _silu(grp(3))
    qb_ref[...] = _rope(grp(4), c, sa, sb)
    kb_ref[...] = _rope(grp(5), c, sa, sb)
    vb_ref[...] = grp(6)
    qi_ref[...] = _rope(grp(7), c, sa, sb)
    kw = grp(8, LANES)
    lane = lax.broadcasted_iota(I32, kw.shape, 1)
    roped = _rope(kw, c_ref[...], sa_ref[...], sb_ref[...])
    kw_ref[...] = jnp.where(lane < D_I, roped, kw * (H_I ** -0.5 * D_I ** -0.5))


def _proj_in(x2d, gain, w_pad, lb, cos_t, sin_a, sin_b, tm):
    m, d = x2d.shape
    n_pad = w_pad.shape[1]
    row = lambda i: (i, 0)
    const = lambda i: (0, 0)
    big = pl.BlockSpec((tm, GROUP), row)
    out_shapes = [jax.ShapeDtypeStruct((m, GROUP), F32)] * 9 + [jax.ShapeDtypeStruct((m, LANES), F32)]
    return pl.pallas_call(
        _proj_in_kernel,
        grid=(m // tm,),
        in_specs=[
            pl.BlockSpec((tm, d), row),
            pl.BlockSpec((1, d), const),
            pl.BlockSpec((d, n_pad), const, pipeline_mode=pl.Buffered(1)),
            pl.BlockSpec((1, GROUP), const),
            pl.BlockSpec((tm, LANES), row),
            pl.BlockSpec((tm, LANES), row),
            pl.BlockSpec((tm, LANES), row),
        ],
        out_specs=[big] * 9 + [pl.BlockSpec((tm, LANES), row)],
        out_shape=out_shapes,
        scratch_shapes=[pltpu.VMEM((tm, d), BF16)],
        compiler_params=pltpu.CompilerParams(
            dimension_semantics=("arbitrary",), vmem_limit_bytes=VMEM_LIMIT_BYTES),
        name="proj_in",
    )(x2d, gain, w_pad, lb, cos_t, sin_a, sin_b)


def _shift_rows(x, d):
    if d == 0:
        return x
    return pltpu.roll(x, d, 0)


def _cumsum_rows(g):
    c = g.shape[0]
    row = lax.broadcasted_iota(I32, g.shape, 0)
    k = 1
    while k < c:
        g = g + jnp.where(row >= k, _shift_rows(g, k), 0.0)
        k *= 2
    return g


def _hgrn_kernel(q_ref, k_ref, lf_ref, v_ref, ga_ref, gain_ref, s0_ref, o_ref, sfin_ref, st_ref, *, chunk, sub):
    ci = pl.program_id(1)

    @pl.when(ci == 0)
    def _():
        st_ref[...] = s0_ref[0]

    nsb = chunk // sub
    row = lax.broadcasted_iota(I32, (chunk, DK_A), 0)
    row_in_sub = row & (sub - 1)
    gain = gain_ref[...]

    for h in range(H_A):
        sl = slice(h * DK_A, (h + 1) * DK_A)
        q = q_ref[0, :, sl]
        k = k_ref[0, :, sl]
        v = v_ref[0, :, sl]
        b = _cumsum_rows(lf_ref[0, :, sl])
        st = st_ref[h]

        refs = [jnp.zeros((1, DK_A), F32)] + [b[i * sub - 1:i * sub, :] for i in range(1, nsb)]
        ref_rows = jnp.concatenate([jnp.broadcast_to(r, (sub, DK_A)) for r in refs], axis=0)
        q_rel = q * jnp.exp(b - ref_rows)

        o = lax.dot_general((q * jnp.exp(b)).astype(BF16), st.astype(BF16),
                            (((1,), (1,)), ((), ())), preferred_element_type=F32)

        off_rows = [jnp.zeros((sub, DK_A), F32)]
        for i in range(1, nsb):
            n_prev = i * sub
            k_rel = (k[:n_prev] * jnp.exp(refs[i] - b[:n_prev])).astype(BF16)
            att = lax.dot_general(q_rel[n_prev:n_prev + sub].astype(BF16), k_rel,
                                  (((1,), (1,)), ((), ())), preferred_element_type=F32)
            off_rows.append(jnp.dot(att.astype(BF16), v[:n_prev].astype(BF16), preferred_element_type=F32))
        if nsb > 1:
            o = o + jnp.concatenate(off_rows, axis=0)

        for d in range(sub):
            valid = row_in_sub >= d
            e = jnp.exp(jnp.where(valid, b - _shift_rows(b, d), 0.0))
            w = jnp.sum(q * _shift_rows(k, d) * e, axis=1, keepdims=True)
            o = o + jnp.where(valid, w * _shift_rows(v, d), 0.0)

        b_last = b[chunk - 1:chunk, :]
        k_dec = (k * jnp.exp(b_last - b)).astype(BF16)
        st_ref[h] = st * jnp.exp(b_last) + lax.dot_general(
            v.astype(BF16), k_dec, (((0,), (0,)), ((), ())), preferred_element_type=F32)

        o_ref[0, :, sl] = _rms(o, gain) * ga_ref[0, :, sl]

    @pl.when(ci == pl.num_programs(1) - 1)
    def _():
        sfin_ref[0] = st_ref[...]


def _hgrn(q, k, lf, v, ga, gain, s0_t, chunk, sub):
    b, t, w = q.shape
    blk = pl.BlockSpec((1, chunk, w), lambda bi, ci: (bi, ci, 0))
    st_spec = pl.BlockSpec((1, H_A, DK_A, DK_A), lambda bi, ci: (bi, 0, 0, 0))
    return pl.pallas_call(
        functools.partial(_hgrn_kernel, chunk=chunk, sub=sub),
        grid=(b, t // chunk),
        in_specs=[blk, blk, blk, blk, blk, pl.BlockSpec((1, DK_A), lambda bi, ci: (0, 0)), st_spec],
        out_specs=[blk, st_spec],
        out_shape=[jax.ShapeDtypeStruct((b, t, w), F32), jax.ShapeDtypeStruct((b, H_A, DK_A, DK_A), F32)],
        scratch_shapes=[pltpu.VMEM((H_A, DK_A, DK_A), F32)],
        compiler_params=pltpu.CompilerParams(
            dimension_semantics=("arbitrary", "arbitrary"), vmem_limit_bytes=VMEM_LIMIT_BYTES),
        name="hgrn",
    )(q, k, lf, v, ga, gain, s0_t)


def _rep_sum(x):
    return jnp.broadcast_to(jnp.sum(x, axis=1, keepdims=True), x.shape)


def _rep_max(x):
    return jnp.broadcast_to(jnp.max(x, axis=1, keepdims=True), x.shape)


def _rep_min(x):
    return jnp.broadcast_to(jnp.min(x, axis=1, keepdims=True), x.shape)


def _any_true(flag):
    return jnp.max(flag) > 0.5


def _lane_tiles(x):
    return [x[:, u * LANES:(u + 1) * LANES] for u in range(x.shape[1] // LANES)]


def _select_rows(sc_ref, nkb, n_hidden_tail, n_sel):
    _, r, w = sc_ref.shape
    shape = (r, LANES)
    neg = jnp.float32(NEG_SCORE)
    inf = jnp.float32(jnp.inf)
    kf = jnp.float32(n_sel)
    tail = n_hidden_tail.astype(F32)
    zeros = jnp.zeros(shape, F32)
    lane = lax.broadcasted_iota(I32, shape, 1)

    def count_ge(c):
        def body(j, acc):
            for xu in _lane_tiles(sc_ref[j]):
                acc = acc + jnp.where(xu >= c, 1.0, 0.0)
            return acc
        return _rep_sum(lax.fori_loop(0, nkb, body, zeros)) + jnp.where(neg >= c, tail, 0.0)

    def stats_body(j, carry):
        vmax, vmin_real, n_real = carry
        for x in _lane_tiles(sc_ref[j]):
            real = x > neg
            vmax = jnp.maximum(vmax, x)
            vmin_real = jnp.minimum(vmin_real, jnp.where(real, x, inf))
            n_real = n_real + jnp.where(real, 1.0, 0.0)
        return vmax, vmin_real, n_real

    def hidden_stats_body(j, carry):
        n_ge_neg, gmin = carry
        for x in _lane_tiles(sc_ref[j]):
            n_ge_neg = n_ge_neg + jnp.where(x >= neg, 1.0, 0.0)
            gmin = jnp.minimum(gmin, jnp.where(x > -inf, x, inf))
        return n_ge_neg, gmin

    vmax, vmin_real, n_real = lax.fori_loop(0, nkb, stats_body, (zeros - inf, zeros + inf, zeros))
    few = _any_true(jnp.where(_rep_sum(n_real) < kf, 1.0, 0.0))
    n_ge_neg, gmin = lax.fori_loop(0, jnp.where(few, nkb, 0), hidden_stats_body, (zeros, zeros + inf))
    has_tail = tail > 0.5
    vmax = _rep_max(vmax)
    vmax = jnp.where(has_tail, jnp.maximum(vmax, neg), vmax)
    vmin_real = _rep_min(vmin_real)
    gmin = _rep_min(gmin)
    gmin = jnp.where(has_tail, jnp.minimum(gmin, neg), gmin)
    n_real = _rep_sum(n_real)
    n_ge_neg = _rep_sum(n_ge_neg) + tail
    c_max = count_ge(vmax)

    at_max = c_max >= kf
    few_real = jnp.logical_and(jnp.logical_not(at_max), n_real < kf)
    at_neg = jnp.logical_and(few_real, n_ge_neg >= kf)
    below_neg = jnp.logical_and(few_real, n_ge_neg < kf)

    done = jnp.where(jnp.logical_or(at_max, at_neg), 1.0, 0.0)
    thr = jnp.where(at_max, vmax, jnp.where(at_neg, neg, zeros))
    n_gt = jnp.where(at_neg, n_real, zeros)
    n_ge = jnp.where(at_max, c_max, jnp.where(at_neg, n_ge_neg, zeros))
    tie = jnp.where(jnp.logical_and(done > 0.5, n_ge > kf), 1.0, 0.0)
    lo = jnp.where(below_neg, gmin, vmin_real)
    hi = jnp.where(below_neg, neg, vmax)
    c_hi = jnp.where(below_neg, n_ge_neg, c_max)

    def bisect(st):
        done, thr, n_gt, tie, lo, hi, c_hi = st
        mid = lo + (hi - lo) * 0.5
        c = count_ge(mid)
        live = done < 0.5
        hit = jnp.logical_and(live, c == kf)
        up = jnp.logical_and(live, c > kf)
        dn = jnp.logical_and(live, c < kf)
        return (jnp.where(hit, 1.0, done), jnp.where(hit, mid, thr), n_gt, tie,
                jnp.where(up, mid, lo), jnp.where(dn, mid, hi), jnp.where(dn, c, c_hi))

    def snap(st):
        done, thr, n_gt, tie, lo, hi, c_hi = st

        def body(j, m):
            for xu in _lane_tiles(sc_ref[j]):
                m = jnp.maximum(m, jnp.where(xu < hi, xu, -inf))
            return m
        below = _rep_max(lax.fori_loop(0, nkb, body, zeros - inf))
        below = jnp.where(jnp.logical_and(tail > 0.5, neg < hi), jnp.maximum(below, neg), below)
        c = count_ge(below)
        live = done < 0.5
        fin = jnp.logical_and(live, c >= kf)
        mv = jnp.logical_and(live, c < kf)
        return (jnp.where(fin, 1.0, done), jnp.where(fin, below, thr), jnp.where(fin, c_hi, n_gt),
                jnp.where(fin, jnp.where(c > kf, 1.0, 0.0), tie),
                lo, jnp.where(mv, below, hi), jnp.where(mv, c, c_hi))

    def not_done(st):
        return _any_true(1.0 - st[0])

    st = (done, thr, n_gt, tie, lo, hi, c_hi)
    st = lax.fori_loop(0, BISECT_BLIND_PASSES, lambda _, s: bisect(s), st)
    _, st = lax.while_loop(
        lambda ps: jnp.logical_and(ps[0] < BISECT_TESTED_ROUNDS, not_done(ps[1])),
        lambda ps: (ps[0] + 1, bisect(bisect(ps[1]))), (jnp.int32(0), st))
    st = lax.while_loop(not_done, lambda s: snap(bisect(s)), st)
    done, thr, n_gt, tie, lo, hi, c_hi = st

    need = kf - n_gt
    n_stored = nkb * w

    def count_eq_upto(jb):
        def body(j, acc):
            for u, xu in enumerate(_lane_tiles(sc_ref[j])):
                idx = j * w + u * LANES + lane
                acc = acc + jnp.where(jnp.logical_and(xu == thr, idx <= jb), 1.0, 0.0)
            return acc
        return _rep_sum(lax.fori_loop(0, nkb, body, zeros))

    def jstep(pj):
        p, (jlo, jhi) = pj
        jmid = jlo + lax.shift_right_arithmetic(jhi - jlo, 1)
        ok = count_eq_upto(jmid) >= need
        return p + 1, (jnp.where(ok, jlo, jmid), jnp.where(ok, jmid, jhi))

    n_jpass = jnp.where(_any_true(tie), 16, 0)
    jlo0 = jnp.full(shape, -1, I32)
    jhi0 = jnp.zeros(shape, I32) + n_stored
    _, (_, jhi) = lax.while_loop(lambda pj: pj[0] < n_jpass, jstep, (jnp.int32(0), (jlo0, jhi0)))
    jmax = jnp.where(tie > 0.5, jhi, jnp.int32(2 ** 30))
    return thr, jmax


def _attn_prompt_kernel(qi_ref, kw_ref, qb_ref, kit_ref, kt_ref, v_ref, o_ref,
                        sc_ref, wb_ref, qc_ref, qh_ref, sa_ref, sb_ref, mxa_ref, mxb_ref, ba_ref, bb_ref,
                        pa_ref, pb_ref,
                        m_ref, acc_ref,
                        *, n_keys, n_sel):
    tq, tk = ATT_TQ, ATT_TK
    tiles = [slice(u * LANES, (u + 1) * LANES) for u in range(tk // LANES)]
    i = pl.program_id(0)
    nkb = (i * tq + tq + tk - 1) // tk
    last_blk = n_keys // tk - 1
    q_pos = i * tq + lax.broadcasted_iota(I32, (tq, LANES), 0)
    lane = lax.broadcasted_iota(I32, (tq, LANES), 1)

    @pl.when(i == 0)
    def _():
        sc_ref[...] = jnp.full(sc_ref.shape, NEG_SCORE, F32)

    kw = kw_ref[...]
    for h in range(H_I):
        hi, lo = _split_bf16(qi_ref[:, h * D_I:(h + 1) * D_I])
        qc_ref[h] = jnp.concatenate([hi, lo, hi], axis=1)
        wb_ref[h] = jnp.broadcast_to(kw[:, D_I + h:D_I + h + 1], (tq, LANES))

    def score_body(jj, _):
        for h in range(H_I):
            wbh = wb_ref[h]
            for j in (2 * jj, 2 * jj + 1):
                d = jnp.dot(qc_ref[h], kit_ref[j], preferred_element_type=F32)
                for u, us in enumerate(tiles):
                    val = wbh * jnp.maximum(d[:, us], 0.0)
                    if h > 0:
                        val = sc_ref[j, :, us] + val
                    if h == H_I - 1:
                        val = jnp.where(j * tk + u * LANES + lane <= q_pos, val, NEG_SCORE)
                    sc_ref[j, :, us] = val
        return 0

    lax.fori_loop(0, (nkb + 1) // 2, score_body, 0)

    tail = jnp.zeros((tq, LANES), I32) + (n_keys - nkb * tk)
    thr, jmax = _select_rows(sc_ref, nkb, tail, n_sel)

    m_ref[...] = jnp.full(m_ref.shape, MASK_VALUE, F32)
    acc_ref[...] = jnp.zeros(acc_ref.shape, F32)
    ones_blk = jnp.ones((tk, LANES), BF16)
    for h in range(H_B):
        qh_ref[h] = (qb_ref[:, h * HD_B:(h + 1) * HD_B] * (HD_B ** -0.5 * LOG2_E)).astype(BF16)
    first_half = lane < HD_B

    def selection_bias(blk, b_ref):
        for u, us in enumerate(tiles):
            xu = sc_ref[blk, :, us]
            idx = blk * tk + u * LANES + lane
            sel = jnp.logical_or(xu > thr, jnp.logical_and(xu == thr, idx <= jmax))
            sel = jnp.logical_and(sel, idx <= q_pos)
            b_ref[:, us] = jnp.where(sel, 0.0, MASK_VALUE)

    def masked_logits(blk, h, s_ref, mx_ref, b_ref):
        s = jnp.dot(qh_ref[h], kt_ref[blk, h * HD_B:(h + 1) * HD_B, :], preferred_element_type=F32) + b_ref[...]
        s_ref[h] = s
        mx = s[:, tiles[0]]
        for us in tiles[1:]:
            mx = jnp.maximum(mx, s[:, us])
        mx_ref[h] = mx

    def reduce_block(blk, s_ref, mx_ref, nxt, b_ref, p_ref):
        rows = pl.ds(pl.multiple_of(blk * tk, tk), tk)
        for pair in range(H_B // 2):
            ps = slice(pair * LANES, (pair + 1) * LANES)
            v_ext = jnp.concatenate([v_ref[rows, ps], ones_blk], axis=1)
            for half in range(2):
                h = 2 * pair + half
                m_old = m_ref[h]
                m_new = jnp.maximum(m_old, _rep_max(mx_ref[h]))
                alpha = jnp.exp2(m_old - m_new)
                acc_ref[h] = jnp.tile(alpha, (1, 2)) * acc_ref[h]
                for us in tiles:
                    p_ref[h, :, us] = jnp.exp2(s_ref[h, :, us] - m_new).astype(BF16)
                m_ref[h] = m_new
                masked_logits(nxt, h, s_ref, mx_ref, b_ref)
                acc_ref[h] += jnp.dot(p_ref[h], v_ext, preferred_element_type=F32)

    selection_bias(0, ba_ref)
    selection_bias(1, bb_ref)
    for h in range(H_B):
        masked_logits(0, h, sa_ref, mxa_ref, ba_ref)
        masked_logits(1, h, sb_ref, mxb_ref, bb_ref)

    def pair_body(jj, _):
        a = 2 * jj
        nxt_a = jnp.minimum(a + 2, last_blk)
        nxt_b = jnp.minimum(a + 3, last_blk)
        selection_bias(nxt_a, ba_ref)
        selection_bias(nxt_b, bb_ref)
        reduce_block(a, sa_ref, mxa_ref, nxt_a, ba_ref, pa_ref)
        reduce_block(a + 1, sb_ref, mxb_ref, nxt_b, bb_ref, pb_ref)
        return 0

    lax.fori_loop(0, (nkb + 1) // 2, pair_body, 0)

    for pair in range(H_B // 2):
        ps = slice(pair * LANES, (pair + 1) * LANES)
        a0, a1 = acc_ref[2 * pair], acc_ref[2 * pair + 1]
        o_ref[:, ps] = jnp.where(first_half, a0[:, :LANES] / a0[:, LANES:], a1[:, :LANES] / a1[:, LANES:])


def _attn_prompt(qi, kw, qb, kit_blocks, kt_blocks, v_bf, n_sel):
    s = qi.shape[0]
    nkb_total = s // ATT_TK
    row = lambda i: (i, 0)
    whole = pl.BlockSpec(memory_space=pltpu.VMEM)
    return pl.pallas_call(
        functools.partial(_attn_prompt_kernel, n_keys=s, n_sel=n_sel),
        grid=(s // ATT_TQ,),
        in_specs=[
            pl.BlockSpec((ATT_TQ, GROUP), row),
            pl.BlockSpec((ATT_TQ, LANES), row),
            pl.BlockSpec((ATT_TQ, GROUP), row),
            whole, whole, whole,
        ],
        out_specs=pl.BlockSpec((ATT_TQ, GROUP), row),
        out_shape=jax.ShapeDtypeStruct((s, GROUP), F32),
        scratch_shapes=[
            pltpu.VMEM((nkb_total, ATT_TQ, ATT_TK), F32),
            pltpu.VMEM((H_I, ATT_TQ, LANES), F32),
            pltpu.VMEM((H_I, ATT_TQ, 3 * D_I), BF16),
            pltpu.VMEM((H_B, ATT_TQ, HD_B), BF16),
            pltpu.VMEM((H_B, ATT_TQ, ATT_TK), F32),
            pltpu.VMEM((H_B, ATT_TQ, ATT_TK), F32),
            pltpu.VMEM((H_B, ATT_TQ, LANES), F32),
            pltpu.VMEM((H_B, ATT_TQ, LANES), F32),
            pltpu.VMEM((ATT_TQ, ATT_TK), F32),
            pltpu.VMEM((ATT_TQ, ATT_TK), F32),
            pltpu.VMEM((H_B, ATT_TQ, ATT_TK), BF16),
            pltpu.VMEM((H_B, ATT_TQ, ATT_TK), BF16),
            pltpu.VMEM((H_B, ATT_TQ, LANES), F32),
            pltpu.VMEM((H_B, ATT_TQ, 2 * LANES), F32),
        ],
        compiler_params=pltpu.CompilerParams(
            dimension_semantics=("arbitrary",), vmem_limit_bytes=VMEM_LIMIT_BYTES),
        name="attn_prompt",
    )(qi, kw, qb, kit_blocks, kt_blocks, v_bf)


def _attn_sample_kernel(pt_ref, qi_ref, kw_ref, qb_ref, kin_ref, kn_ref, vn_ref,
                        ckidx_hbm, ck_hbm, cv_hbm, o_ref,
                        kid_buf, k_buf, v_buf, sc_ref, m_ref, l_ref, acc_ref, sem_i, sem_k, sem_v,
                        *, n_pages, t_new, n_sel):
    b = pl.program_id(0)
    r = SUBLANES
    nrow = H_B * r
    n_chunks = n_pages // KV_CHUNK_PAGES

    def page_lanes(p):
        start = (p % KV_CHUNK_PAGES) * PAGE_SIZE
        return pl.ds(start if isinstance(start, int) else pl.multiple_of(start, PAGE_SIZE), PAGE_SIZE)

    def kidx_copy(p):
        return pltpu.make_async_copy(ckidx_hbm.at[pt_ref[b, p]],
                                     kid_buf.at[p // KV_CHUNK_PAGES, :, page_lanes(p)], sem_i)

    def kv_copies(c, slot, p):
        page = pt_ref[b, c * KV_CHUNK_PAGES + p]
        return (pltpu.make_async_copy(ck_hbm.at[page], k_buf.at[slot, :, :, page_lanes(p)], sem_k.at[slot]),
                pltpu.make_async_copy(cv_hbm.at[page], v_buf.at[slot, :, :, page_lanes(p)], sem_v.at[slot]))

    def start_chunk(c, slot):
        for p in range(KV_CHUNK_PAGES):
            ck, cv = kv_copies(c, slot, p)
            ck.start()
            cv.start()

    def wait_chunk(c, slot):
        for p in range(KV_CHUNK_PAGES):
            ck, cv = kv_copies(c, slot, p)
            ck.wait()
            cv.wait()

    def start_kidx(p, _):
        kidx_copy(p).start()
        return 0

    def wait_kidx(p, _):
        kidx_copy(p).wait()
        return 0

    lax.fori_loop(0, n_pages, start_kidx, 0)
    for c0 in range(KV_SLOTS - 1):
        start_chunk(c0, c0)

    kw = kw_ref[0]
    qi = qi_ref[0]
    q_hi, q_lo, w_rows = [], [], []
    for h in range(H_I):
        hi, lo = _split_bf16(qi[:, h * D_I:(h + 1) * D_I])
        q_hi.append(hi)
        q_lo.append(lo)
        w_rows.append(jnp.broadcast_to(kw[:, D_I + h:D_I + h + 1], (r, LANES)))
    q_hi = jnp.concatenate(q_hi, axis=0)
    q_hl = jnp.concatenate([q_hi, jnp.concatenate(q_lo, axis=0)], axis=0)
    w_rows = jnp.concatenate(w_rows, axis=0)
    nt = (((1,), (1,)), ((), ()))

    def key_scores(keys_t):
        k_hi, k_lo = _split_bf16(keys_t)
        d2 = jnp.dot(q_hl, k_hi, preferred_element_type=F32)
        d = d2[:H_I * r] + d2[H_I * r:] + jnp.dot(q_hi, k_lo, preferred_element_type=F32)
        out = []
        for u in range(keys_t.shape[1] // LANES):
            us = slice(u * LANES, (u + 1) * LANES)
            wd = w_rows * jnp.maximum(d[:, us], 0.0)
            acc = wd[0:r]
            for h in range(1, H_I):
                acc = acc + wd[h * r:(h + 1) * r]
            out.append(acc)
        return out

    lax.fori_loop(0, n_pages, wait_kidx, 0)

    def score_body(c, _):
        for u, tile in enumerate(key_scores(kid_buf[c])):
            sc_ref[c, :, u * LANES:(u + 1) * LANES] = tile
        return 0

    lax.fori_loop(0, n_chunks, score_body, 0)

    lane = lax.broadcasted_iota(I32, (r, LANES), 1)
    t_row = jnp.minimum(lax.broadcasted_iota(I32, (r, LANES), 0), t_new - 1)
    new_sc = key_scores(kin_ref[0])[0]
    new_vis = lane <= t_row
    sc_ref[n_chunks] = jnp.full((r, KV_CHUNK_PAGES * LANES), -jnp.inf, F32)
    sc_ref[n_chunks, :, 0:LANES] = jnp.where(lane >= t_new, -jnp.inf, jnp.where(new_vis, new_sc, NEG_SCORE))

    thr, jmax = _select_rows(sc_ref, n_chunks + 1, jnp.zeros((r, LANES), I32), n_sel)

    qb = qb_ref[0] * (HD_B ** -0.5)
    col_head = lax.shift_right_logical(lax.broadcasted_iota(I32, (r, GROUP), 1), HD_B.bit_length() - 1)
    q_bd = jnp.concatenate([jnp.where(col_head == h, qb, 0.0) for h in range(H_B)], axis=0).astype(BF16)

    m_ref[...] = jnp.full(m_ref.shape, MASK_VALUE, F32)
    l_ref[...] = jnp.zeros(l_ref.shape, F32)
    acc_ref[...] = jnp.zeros(acc_ref.shape, F32)

    def attend(blk, n_blk, k_t, v_t, extra_mask):
        bias = []
        for u in range(n_blk):
            x = sc_ref[blk, :, u * LANES:(u + 1) * LANES]
            idx = (blk * KV_CHUNK_PAGES + u) * LANES + lane
            sel = jnp.logical_or(x > thr, jnp.logical_and(x == thr, idx <= jmax))
            if extra_mask is not None:
                sel = jnp.logical_and(sel, extra_mask)
            bias.append(jnp.where(sel, 0.0, MASK_VALUE))
        bias = jnp.tile(jnp.concatenate(bias, axis=1), (H_B, 1))
        s = jnp.dot(q_bd, k_t.astype(BF16), preferred_element_type=F32) + bias
        m_old = m_ref[...]
        m_new = jnp.maximum(m_old, jnp.broadcast_to(jnp.max(s, axis=1, keepdims=True), m_old.shape))
        alpha = jnp.exp(m_old - m_new)
        p = jnp.exp(s - jnp.tile(m_new, (1, n_blk)))
        p_sum = p[:, :LANES]
        for u in range(1, n_blk):
            p_sum = p_sum + p[:, u * LANES:(u + 1) * LANES]
        l_ref[...] = alpha * l_ref[...] + p_sum
        pv = lax.dot_general(p.astype(BF16), v_t.astype(BF16), nt, preferred_element_type=F32)
        acc_ref[...] = jnp.tile(alpha, (1, GROUP // LANES)) * acc_ref[...] + pv
        m_ref[...] = m_new

    def chunk_body(c, _):
        slot = c % KV_SLOTS
        wait_chunk(c, slot)
        ahead = c + KV_SLOTS - 1

        @pl.when(ahead < n_chunks)
        def _():
            start_chunk(ahead, ahead % KV_SLOTS)

        n_keys_chunk = KV_CHUNK_PAGES * PAGE_SIZE
        attend(c, KV_CHUNK_PAGES, k_buf[slot].reshape(GROUP, n_keys_chunk),
               v_buf[slot].reshape(GROUP, n_keys_chunk), None)
        return 0

    lax.fori_loop(0, n_chunks, chunk_body, 0)
    attend(n_chunks, 1, kn_ref[0].reshape(GROUP, PAGE_SIZE), vn_ref[0].reshape(GROUP, PAGE_SIZE), new_vis)

    out = jnp.zeros((r, GROUP), F32)
    for h in range(H_B):
        rs = slice(h * r, (h + 1) * r)
        denom = jnp.sum(l_ref[rs, :], axis=1, keepdims=True)
        out = out + jnp.where(col_head == h, acc_ref[rs, :] / denom, 0.0)
    o_ref[0] = out


def _attn_sample(page_table, qi, kw, qb, ki_new, k_new, v_new, cache_kidx, cache_k, cache_v, t_new, n_sel):
    nb, n_pages = page_table.shape
    r = SUBLANES
    nrow = H_B * r
    per_b = lambda b, pt: (b, 0, 0)
    per_b4 = lambda b, pt: (b, 0, 0, 0)
    anyspec = pl.BlockSpec(memory_space=pl.ANY)
    chunk_keys = KV_CHUNK_PAGES * PAGE_SIZE
    n_chunks = n_pages // KV_CHUNK_PAGES
    grid_spec = pltpu.PrefetchScalarGridSpec(
        num_scalar_prefetch=1,
        grid=(nb,),
        in_specs=[
            pl.BlockSpec((1, r, GROUP), per_b),
            pl.BlockSpec((1, r, LANES), per_b),
            pl.BlockSpec((1, r, GROUP), per_b),
            pl.BlockSpec((1, D_I, PAGE_SIZE), per_b),
            pl.BlockSpec((1, H_B, HD_B, PAGE_SIZE), per_b4),
            pl.BlockSpec((1, H_B, HD_B, PAGE_SIZE), per_b4),
            anyspec, anyspec, anyspec,
        ],
        out_specs=pl.BlockSpec((1, r, GROUP), per_b),
        scratch_shapes=[
            pltpu.VMEM((n_chunks, D_I, chunk_keys), F32),
            pltpu.VMEM((KV_SLOTS, H_B, HD_B, chunk_keys), F32),
            pltpu.VMEM((KV_SLOTS, H_B, HD_B, chunk_keys), F32),
            pltpu.VMEM((n_chunks + 1, r, chunk_keys), F32),
            pltpu.VMEM((nrow, LANES), F32),
            pltpu.VMEM((nrow, LANES), F32),
            pltpu.VMEM((nrow, GROUP), F32),
            pltpu.SemaphoreType.DMA(()),
            pltpu.SemaphoreType.DMA((KV_SLOTS,)),
            pltpu.SemaphoreType.DMA((KV_SLOTS,)),
        ],
    )
    return pl.pallas_call(
        functools.partial(_attn_sample_kernel, n_pages=n_pages, t_new=t_new, n_sel=n_sel),
        grid_spec=grid_spec,
        out_shape=jax.ShapeDtypeStruct((nb, r, GROUP), F32),
        compiler_params=pltpu.CompilerParams(
            dimension_semantics=("arbitrary",), vmem_limit_bytes=VMEM_LIMIT_BYTES),
        name="attn_sample",
    )(page_table, qi, kw, qb, ki_new, k_new, v_new, cache_kidx, cache_k, cache_v)


def _out_ffn_kernel(x_ref, oa_ref, ob_ref, p1_ref, p2_ref, wo_ref, g2_ref, wu_ref, cw_ref, cb_ref, wd_ref, gf_ref,
                    y_ref, u_ref, carry_ref, x1_ref, h2_ref, *, d_ff, ff_tile, seq_rows, carry_mode):
    tm = x_ref.shape[0]
    i = pl.program_id(0)
    mixed = jnp.concatenate([oa_ref[...], ob_ref[...]], axis=1).astype(BF16)
    x1 = x_ref[...] + jnp.dot(mixed, wo_ref[...], preferred_element_type=F32)
    h2_ref[...] = _rms(x1, g2_ref[...]).astype(BF16)
    x1_ref[...] = x1

    row = lax.broadcasted_iota(I32, (tm, ff_tile), 0)
    if carry_mode:
        @pl.when(i == 0)
        def _():
            carry_ref[...] = jnp.zeros(carry_ref.shape, F32)
        t_in_seq = row
    else:
        t_in_seq = row & (seq_rows - 1)

    for c in range(d_ff // ff_tile):
        cs = slice(c * ff_tile, (c + 1) * ff_tile)
        h2 = h2_ref[...]
        u = jnp.dot(h2, wu_ref[:, cs], preferred_element_type=F32)
        v = jnp.dot(h2, wu_ref[:, d_ff + c * ff_tile:d_ff + (c + 1) * ff_tile], preferred_element_type=F32)
        if carry_mode:
            prev = carry_ref[:, cs]
            p1 = jnp.broadcast_to(prev[SUBLANES - 1:SUBLANES], (tm, ff_tile))
            p2 = jnp.where(row == 0, jnp.broadcast_to(prev[SUBLANES - 2:SUBLANES - 1], (tm, ff_tile)), p1)
            carry_ref[:, cs] = u[tm - SUBLANES:]
            u_ref[:, cs] = u[tm - SUBLANES:]
        else:
            p1 = p1_ref[:, cs]
            p2 = p2_ref[:, cs]
            u_ref[:, cs] = u
        u1 = jnp.where(t_in_seq >= 1, pltpu.roll(u, 1, 0), p1)
        u2 = jnp.where(t_in_seq >= 2, pltpu.roll(u, 2, 0), p2)
        conv = cb_ref[:, cs] + cw_ref[0:1, cs] * u2 + cw_ref[1:2, cs] * u1 + cw_ref[2:3, cs] * u
        gate = (_silu(conv) * v).astype(BF16)
        x1_ref[...] += jnp.dot(gate, wd_ref[cs, :], preferred_element_type=F32)

    y_ref[...] = _rms(x1_ref[...], gf_ref[...])


def _out_ffn(x2d, oa, ob, p1, p2, w_out, g2, w_up, conv_w, conv_b, w_down, gf, tm, seq_rows, carry_mode):
    m, d = x2d.shape
    d_ff = w_down.shape[0]
    ff_tile = 256
    row = lambda i: (i, 0)
    const = lambda i: (0, 0)
    if carry_mode:
        prev_spec = pl.BlockSpec((SUBLANES, d_ff), const)
        u_spec = pl.BlockSpec((SUBLANES, d_ff), const)
        u_shape = jax.ShapeDtypeStruct((SUBLANES, d_ff), F32)
    else:
        prev_spec = pl.BlockSpec((tm, d_ff), row)
        u_spec = pl.BlockSpec((tm, d_ff), row)
        u_shape = jax.ShapeDtypeStruct((m, d_ff), F32)
    return pl.pallas_call(
        functools.partial(_out_ffn_kernel, d_ff=d_ff, ff_tile=ff_tile, seq_rows=seq_rows, carry_mode=carry_mode),
        grid=(m // tm,),
        in_specs=[
            pl.BlockSpec((tm, d), row),
            pl.BlockSpec((tm, GROUP), row),
            pl.BlockSpec((tm, GROUP), row),
            prev_spec, prev_spec,
            pl.BlockSpec(w_out.shape, const, pipeline_mode=pl.Buffered(1)),
            pl.BlockSpec((1, d), const),
            pl.BlockSpec(w_up.shape, const, pipeline_mode=pl.Buffered(1)),
            pl.BlockSpec(conv_w.shape, const),
            pl.BlockSpec((1, d_ff), const),
            pl.BlockSpec(w_down.shape, const, pipeline_mode=pl.Buffered(1)),
            pl.BlockSpec((1, d), const),
        ],
        out_specs=[pl.BlockSpec((tm, d), row), u_spec],
        out_shape=[jax.ShapeDtypeStruct((m, d), F32), u_shape],
        scratch_shapes=[pltpu.VMEM((SUBLANES, d_ff), F32), pltpu.VMEM((tm, d), F32), pltpu.VMEM((tm, d), BF16)],
        compiler_params=pltpu.CompilerParams(
            dimension_semantics=("arbitrary",), vmem_limit_bytes=VMEM_LIMIT_BYTES),
        name="out_ffn",
    )(x2d, oa, ob, p1, p2, w_out, g2, w_up, conv_w, conv_b, w_down, gf)


def _rope_tables(pos):
    half = ROT // 2
    inv = jnp.power(ROPE_THETA, -jnp.arange(half, dtype=F32) * (2.0 / ROT))
    ang = pos.astype(F32)[:, None] * inv[None, :]
    cos, sin = jnp.cos(ang), jnp.sin(ang)
    n = pos.shape[0]
    ones = jnp.ones((n, HD_B - ROT), F32)
    zeros = jnp.zeros((n, HD_B - ROT), F32)
    zh = jnp.zeros((n, half), F32)
    c = jnp.concatenate([cos, cos, ones], axis=1)
    sa = jnp.concatenate([zh, sin, zeros], axis=1)
    sb = jnp.concatenate([-sin, zh, zeros], axis=1)
    rep = LANES // HD_B
    return jnp.tile(c, (1, rep)), jnp.tile(sa, (1, rep)), jnp.tile(sb, (1, rep))


def kernel(x_prompt, x_sample, cache_k, cache_v, cache_kidx, state_hgrn, state_conv, page_table,
           norm_mix_gain, w_in, lb_logits, hgrn_norm_gain, w_out, norm_ffn_gain, w_up, conv_w,
           conv_b, w_down, final_norm_gain):
    bp, s, d = x_prompt.shape
    nb, t_new, _ = x_sample.shape
    depth = w_in.shape[0]
    assert bp == 1 and depth == 1
    n_pages = page_table.shape[1]
    past = n_pages * PAGE_SIZE
    d_ff = w_down.shape[1]
    assert s % (2 * ATT_TK) == 0 and s % HGRN_CHUNK == 0
    assert n_pages % KV_CHUNK_PAGES == 0 and n_pages // KV_CHUNK_PAGES >= KV_SLOTS
    assert t_new <= SUBLANES and CONV_W - 1 <= t_new

    lower_bounds = jnp.cumsum(jax.nn.softmax(lb_logits.astype(F32), axis=0), axis=0)
    lb = lower_bounds[0][None, :]
    n_in = w_in.shape[2]
    n_pad = 8 * GROUP + LANES
    w_in_bf = jnp.pad(w_in[0], ((0, 0), (0, n_pad - n_in))).astype(BF16)
    w_out_bf = w_out[0].astype(BF16)
    w_up_bf = w_up[0].astype(BF16)
    w_down_bf = w_down[0].astype(BF16)
    g_mix = norm_mix_gain[0][None, :]
    g_ffn = norm_ffn_gain[0][None, :]
    g_fin = final_norm_gain[None, :]
    g_hgrn = hgrn_norm_gain[0][None, :]
    conv_b2 = conv_b[0][None, :]

    xp = x_prompt.reshape(s, d)
    cp, sap, sbp = _rope_tables(jnp.arange(s, dtype=I32))
    qa, ka, lf, ia, ga, qb, kb, vb, qi, kw = _proj_in(xp, g_mix, w_in_bf, lb, cp, sap, sbp, tm=256)

    s0 = jnp.zeros((1, H_A, DK_A, DK_A), F32)
    r3 = lambda a: a.reshape(1, s, GROUP)
    oa_p, st_p = _hgrn(r3(qa), r3(ka), r3(lf), r3(ia), r3(ga), g_hgrn, s0, HGRN_CHUNK, HGRN_SUB)

    nkb = s // ATT_TK
    ki_hi, ki_lo = _split_bf16(kw[:, :D_I])
    kit = jnp.concatenate([ki_hi, ki_hi, ki_lo], axis=1)
    kit_blocks = kit.reshape(nkb, ATT_TK, 3 * D_I).transpose(0, 2, 1)
    kt_blocks = kb.astype(BF16).reshape(nkb, ATT_TK, GROUP).transpose(0, 2, 1)
    ob_p = _attn_prompt(qi, kw, qb, kit_blocks, kt_blocks, vb.astype(BF16), min(TOPK_MAX, s // 4))

    zero_prev = jnp.zeros((SUBLANES, d_ff), F32)
    y_p, u_tail = _out_ffn(xp, oa_p.reshape(s, GROUP), ob_p, zero_prev, zero_prev, w_out_bf, g_ffn, w_up_bf,
                           conv_w[0], conv_b2, w_down_bf, g_fin, tm=256, seq_rows=s, carry_mode=True)

    ms = nb * t_new
    xs = x_sample.reshape(ms, d)
    pos_s = jnp.tile(past + jnp.arange(t_new, dtype=I32), nb)
    cs, sas, sbs = _rope_tables(pos_s)
    qa2, ka2, lf2, ia2, ga2, qb2, kb2, vb2, qi2, kw2 = _proj_in(xs, g_mix, w_in_bf, lb, cs, sas, sbs, tm=ms)

    pad_front = SUBLANES - t_new
    fp = lambda a: jnp.pad(a.reshape(nb, t_new, GROUP), ((0, 0), (pad_front, 0), (0, 0)))
    s0_s = jnp.swapaxes(state_hgrn[0], -1, -2)
    oa_s, st_s = _hgrn(fp(qa2), fp(ka2), fp(lf2), fp(ia2), fp(ga2), g_hgrn, s0_s, SUBLANES, SUBLANES)
    oa_s = oa_s[:, pad_front:, :].reshape(ms, GROUP)

    def rp(a):
        a = a.reshape(nb, t_new, a.shape[-1])
        return jnp.concatenate([a, jnp.broadcast_to(a[:, -1:], (nb, SUBLANES - t_new, a.shape[-1]))], axis=1)
    page_pad = lambda a: jnp.pad(a.reshape(nb, t_new, a.shape[-1]), ((0, 0), (0, PAGE_SIZE - t_new), (0, 0)))
    new_page = lambda a: jnp.swapaxes(page_pad(a), 1, 2)
    heads = lambda a: new_page(a).reshape(nb, H_B, HD_B, PAGE_SIZE)
    ob_s = _attn_sample(
        page_table, rp(qi2), rp(kw2), rp(qb2), new_page(kw2[:, :D_I]), heads(kb2), heads(vb2),
        jnp.transpose(cache_kidx[0], (0, 2, 1)), jnp.transpose(cache_k[0], (0, 2, 3, 1)),
        jnp.transpose(cache_v[0], (0, 2, 3, 1)),
        t_new, min(TOPK_MAX, (past + t_new) // 4))
    ob_s = ob_s[:, :t_new, :].reshape(ms, GROUP)

    sc0 = state_conv[0]
    zrow = jnp.zeros((nb, t_new - 1, d_ff), F32)
    p1 = jnp.concatenate([sc0[:, 1:2], zrow], axis=1).reshape(ms, d_ff)
    p2 = jnp.concatenate([sc0[:, 0:2], zrow[:, 1:]], axis=1).reshape(ms, d_ff)
    y_s, u_s = _out_ffn(xs, oa_s, ob_s, p1, p2, w_out_bf, g_ffn, w_up_bf, conv_w[0], conv_b2, w_down_bf, g_fin,
                        tm=ms, seq_rows=t_new, carry_mode=False)

    return (
        y_p.reshape(1, s, d),
        y_s.reshape(nb, t_new, d),
        kb.reshape(1, 1, s, H_B, HD_B),
        vb.reshape(1, 1, s, H_B, HD_B),
        kw[:, :D_I].reshape(1, 1, s, D_I),
        jnp.swapaxes(st_p, -1, -2).reshape(1, 1, H_A, DK_A, DK_A),
        u_tail[SUBLANES - (CONV_W - 1):].reshape(1, 1, CONV_W - 1, d_ff),
        kb2.reshape(1, nb, t_new, H_B, HD_B),
        vb2.reshape(1, nb, t_new, H_B, HD_B),
        kw2[:, :D_I].reshape(1, nb, t_new, D_I),
        jnp.swapaxes(st_s, -1, -2).reshape(1, nb, H_A, DK_A, DK_A),
        u_s.reshape(nb, t_new, d_ff)[:, t_new - (CONV_W - 1):].reshape(1, nb, CONV_W - 1, d_ff),
    )
```

```python
import functools

import jax
import jax.numpy as jnp
from jax import lax
from jax.experimental import pallas as pl
from jax.experimental.pallas import tpu as pltpu

F32 = jnp.float32
BF16 = jnp.bfloat16
I32 = jnp.int32

H_A = 4
DK_A = 128
H_B = 8
HD_B = 64
H_I = 8
D_I = 64
ROT = 16
ROPE_THETA = 500000.0
TOPK_MAX = 256
PAGE_SIZE = 128
CONV_W = 3
EPS = 1e-6
NEG_SCORE = -1e30
MASK_VALUE = -1e30
GROUP = 512
LOG2_E = 1.4426950408889634

LANES = 128
SUBLANES = 8
VMEM_LIMIT_BYTES = 61 * 1024 * 1024

PROJ_ROWS = 512
HGRN_CHUNK = 64
HGRN_SUB = 16
BISECT_BLIND_PASSES = 10
BISECT_TESTED_ROUNDS = 8
ATT_TQ = 128
ATT_TK = 512
KV_CHUNK_PAGES = 8
KV_SLOTS = 4


def _silu(x):
    return x * jax.nn.sigmoid(x)


def _rms(x, g):
    return x * lax.rsqrt(jnp.mean(x * x, axis=-1, keepdims=True) + EPS) * g


def _split_bf16(x):
    hi = x.astype(BF16)
    lo = (x - hi.astype(F32)).astype(BF16)
    return hi, lo


def _rope(x, c, sa, sb):
    w = x.shape[1]
    return x * c + pltpu.roll(x, ROT // 2, 1) * sa + pltpu.roll(x, w - ROT // 2, 1) * sb


def _proj_in_kernel(x_ref, g_ref, w_ref, lb_ref, c_ref, sa_ref, sb_ref,
                    qa_ref, ka_ref, lf_ref, ia_ref, ga_ref, qb_ref, kb_ref, vb_ref, qi_ref, kw_ref, h_ref):
    h_ref[...] = _rms(x_ref[...], g_ref[...]).astype(BF16)

    def grp(i, width=GROUP):
        return jnp.dot(h_ref[...], w_ref[:, i * GROUP:i * GROUP + width], preferred_element_type=F32)

    c = jnp.tile(c_ref[...], (1, GROUP // LANES))
    sa = jnp.tile(sa_ref[...], (1, GROUP // LANES))
    sb = jnp.tile(sb_ref[...], (1, GROUP // LANES))

    qa_ref[...] = _silu(grp(0))
    lb = lb_ref[...]
    fg = lb + (1.0 - lb) * jax.nn.sigmoid(grp(1))
    ka_ref[...] = 1.0 - fg
    lf_ref[...] = jnp.log(fg)
    ia_ref[...] = grp(2)
    ga_ref[...] = _silu(grp(3))
    qb_ref[...] = _rope(grp(4), c, sa, sb)
    kb_ref[...] = _rope(grp(5), c, sa, sb)
    vb_ref[...] = grp(6)
    qi_ref[...] = _rope(grp(7), c, sa, sb)
    kw = grp(8, LANES)
    lane = lax.broadcasted_iota(I32, kw.shape, 1)
    roped = _rope(kw, c_ref[...], sa_ref[...], sb_ref[...])
    kw_ref[...] = jnp.where(lane < D_I, roped, kw * (H_I ** -0.5 * D_I ** -0.5))


def _proj_in(x2d, gain, w_pad, lb, cos_t, sin_a, sin_b, tm):
    m, d = x2d.shape
    n_pad = w_pad.shape[1]
    row = lambda i: (i, 0)
    const = lambda i: (0, 0)
    big = pl.BlockSpec((tm, GROUP), row)
    out_shapes = [jax.ShapeDtypeStruct((m, GROUP), F32)] * 9 + [jax.ShapeDtypeStruct((m, LANES), F32)]
    return pl.pallas_call(
        _proj_in_kernel,
        grid=(m // tm,),
        in_specs=[
            pl.BlockSpec((tm, d), row),
            pl.BlockSpec((1, d), const),
            pl.BlockSpec((d, n_pad), const, pipeline_mode=pl.Buffered(1)),
            pl.BlockSpec((1, GROUP), const),
            pl.BlockSpec((tm, LANES), row),
            pl.BlockSpec((tm, LANES), row),
            pl.BlockSpec((tm, LANES), row),
        ],
        out_specs=[big] * 9 + [pl.BlockSpec((tm, LANES), row)],
        out_shape=out_shapes,
        scratch_shapes=[pltpu.VMEM((tm, d), BF16)],
        compiler_params=pltpu.CompilerParams(
            dimension_semantics=("arbitrary",), vmem_limit_bytes=VMEM_LIMIT_BYTES),
        name="proj_in",
    )(x2d, gain, w_pad, lb, cos_t, sin_a, sin_b)


def _shift_rows(x, d):
    if d == 0:
        return x
    return pltpu.roll(x, d, 0)


def _cumsum_rows(g):
    c = g.shape[0]
    row = lax.broadcasted_iota(I32, g.shape, 0)
    k = 1
    while k < c:
        g = g + jnp.where(row >= k, _shift_rows(g, k), 0.0)
        k *= 2
    return g


def _hgrn_kernel(q_ref, k_ref, lf_ref, v_ref, ga_ref, gain_ref, s0_ref, o_ref, sfin_ref, st_ref, *, chunk, sub):
    ci = pl.program_id(1)

    @pl.when(ci == 0)
    def _():
        st_ref[...] = s0_ref[0]

    nsb = chunk // sub
    row = lax.broadcasted_iota(I32, (chunk, DK_A), 0)
    row_in_sub = row & (sub - 1)
    gain = gain_ref[...]

    for h in range(H_A):
        sl = slice(h * DK_A, (h + 1) * DK_A)
        q = q_ref[0, :, sl]
        k = k_ref[0, :, sl]
        v = v_ref[0, :, sl]
        b = _cumsum_rows(lf_ref[0, :, sl])
        st = st_ref[h]

        refs = [jnp.zeros((1, DK_A), F32)] + [b[i * sub - 1:i * sub, :] for i in range(1, nsb)]
        ref_rows = jnp.concatenate([jnp.broadcast_to(r, (sub, DK_A)) for r in refs], axis=0)
        q_rel = q * jnp.exp(b - ref_rows)

        o = lax.dot_general((q * jnp.exp(b)).astype(BF16), st.astype(BF16),
                            (((1,), (1,)), ((), ())), preferred_element_type=F32)

        off_rows = [jnp.zeros((sub, DK_A), F32)]
        for i in range(1, nsb):
            n_prev = i * sub
            k_rel = (k[:n_prev] * jnp.exp(refs[i] - b[:n_prev])).astype(BF16)
            att = lax.dot_general(q_rel[n_prev:n_prev + sub].astype(BF16), k_rel,
                                  (((1,), (1,)), ((), ())), preferred_element_type=F32)
            off_rows.append(jnp.dot(att.astype(BF16), v[:n_prev].astype(BF16), preferred_element_type=F32))
        if nsb > 1:
            o = o + jnp.concatenate(off_rows, axis=0)

        for d in range(sub):
            valid = row_in_sub >= d
            e = jnp.exp(jnp.where(valid, b - _shift_rows(b, d), 0.0))
            w = jnp.sum(q * _shift_rows(k, d) * e, axis=1, keepdims=True)
            o = o + jnp.where(valid, w * _shift_rows(v, d), 0.0)

        b_last = b[chunk - 1:chunk, :]
        k_dec = (k * jnp.exp(b_last - b)).astype(BF16)
        st_ref[h] = st * jnp.exp(b_last) + lax.dot_general(
            v.astype(BF16), k_dec, (((0,), (0,)), ((), ())), preferred_element_type=F32)

        o_ref[0, :, sl] = _rms(o, gain) * ga_ref[0, :, sl]

    @pl.when(ci == pl.num_programs(1) - 1)
    def _():
        sfin_ref[0] = st_ref[...]


def _hgrn(q, k, lf, v, ga, gain, s0_t, chunk, sub):
    b, t, w = q.shape
    blk = pl.BlockSpec((1, chunk, w), lambda bi, ci: (bi, ci, 0))
    st_spec = pl.BlockSpec((1, H_A, DK_A, DK_A), lambda bi, ci: (bi, 0, 0, 0))
    return pl.pallas_call(
        functools.partial(_hgrn_kernel, chunk=chunk, sub=sub),
        grid=(b, t // chunk),
        in_specs=[blk, blk, blk, blk, blk, pl.BlockSpec((1, DK_A), lambda bi, ci: (0, 0)), st_spec],
        out_specs=[blk, st_spec],
        out_shape=[jax.ShapeDtypeStruct((b, t, w), F32), jax.ShapeDtypeStruct((b, H_A, DK_A, DK_A), F32)],
        scratch_shapes=[pltpu.VMEM((H_A, DK_A, DK_A), F32)],
        compiler_params=pltpu.CompilerParams(
            dimension_semantics=("arbitrary", "arbitrary"), vmem_limit_bytes=VMEM_LIMIT_BYTES),
        name="hgrn",
    )(q, k, lf, v, ga, gain, s0_t)


def _rep_sum(x):
    return jnp.broadcast_to(jnp.sum(x, axis=1, keepdims=True), x.shape)


def _rep_max(x):
    return jnp.broadcast_to(jnp.max(x, axis=1, keepdims=True), x.shape)


def _rep_min(x):
    return jnp.broadcast_to(jnp.min(x, axis=1, keepdims=True), x.shape)


def _any_true(flag):
    return jnp.max(flag) > 0.5


def _lane_tiles(x):
    return [x[:, u * LANES:(u + 1) * LANES] for u in range(x.shape[1] // LANES)]


def _select_rows(sc_ref, nkb, n_hidden_tail, n_sel):
    _, r, w = sc_ref.shape
    shape = (r, LANES)
    neg = jnp.float32(NEG_SCORE)
    inf = jnp.float32(jnp.inf)
    kf = jnp.float32(n_sel)
    tail = n_hidden_tail.astype(F32)
    zeros = jnp.zeros(shape, F32)
    lane = lax.broadcasted_iota(I32, shape, 1)

    def count_ge(c):
        def body(j, acc):
            for xu in _lane_tiles(sc_ref[j]):
                acc = acc + jnp.where(xu >= c, 1.0, 0.0)
            return acc
        return _rep_sum(lax.fori_loop(0, nkb, body, zeros)) + jnp.where(neg >= c, tail, 0.0)

    def stats_body(j, carry):
        vmax, vmin_real, n_real = carry
        for x in _lane_tiles(sc_ref[j]):
            real = x > neg
            vmax = jnp.maximum(vmax, x)
            vmin_real = jnp.minimum(vmin_real, jnp.where(real, x, inf))
            n_real = n_real + jnp.where(real, 1.0, 0.0)
        return vmax, vmin_real, n_real

    def hidden_stats_body(j, carry):
        n_ge_neg, gmin = carry
        for x in _lane_tiles(sc_ref[j]):
            n_ge_neg = n_ge_neg + jnp.where(x >= neg, 1.0, 0.0)
            gmin = jnp.minimum(gmin, jnp.where(x > -inf, x, inf))
        return n_ge_neg, gmin

    vmax, vmin_real, n_real = lax.fori_loop(0, nkb, stats_body, (zeros - inf, zeros + inf, zeros))
    few = _any_true(jnp.where(_rep_sum(n_real) < kf, 1.0, 0.0))
    n_ge_neg, gmin = lax.fori_loop(0, jnp.where(few, nkb, 0), hidden_stats_body, (zeros, zeros + inf))
    has_tail = tail > 0.5
    vmax = _rep_max(vmax)
    vmax = jnp.where(has_tail, jnp.maximum(vmax, neg), vmax)
    vmin_real = _rep_min(vmin_real)
    gmin = _rep_min(gmin)
    gmin = jnp.where(has_tail, jnp.minimum(gmin, neg), gmin)
    n_real = _rep_sum(n_real)
    n_ge_neg = _rep_sum(n_ge_neg) + tail
    above_max = vmax + jnp.maximum(jnp.abs(vmax), 1e-30) * 1e-6

    few_real = n_real < kf
    at_neg = jnp.logical_and(few_real, n_ge_neg >= kf)
    below_neg = jnp.logical_and(few_real, n_ge_neg < kf)

    done = jnp.where(at_neg, 1.0, 0.0)
    thr = jnp.where(at_neg, neg, zeros)
    n_gt = jnp.where(at_neg, n_real, zeros)
    tie = jnp.where(jnp.logical_and(at_neg, n_ge_neg > kf), 1.0, 0.0)
    lo = jnp.where(below_neg, gmin, vmin_real)
    hi = jnp.where(below_neg, neg, above_max)
    c_hi = jnp.where(below_neg, n_ge_neg, zeros)

    def bisect(st):
        done, thr, n_gt, tie, lo, hi, c_hi = st
        mid = lo + (hi - lo) * 0.5
        c = count_ge(mid)
        live = done < 0.5
        hit = jnp.logical_and(live, c == kf)
        up = jnp.logical_and(live, c > kf)
        dn = jnp.logical_and(live, c < kf)
        return (jnp.where(hit, 1.0, done), jnp.where(hit, mid, thr), n_gt, tie,
                jnp.where(up, mid, lo), jnp.where(dn, mid, hi), jnp.where(dn, c, c_hi))

    def snap(st):
        done, thr, n_gt, tie, lo, hi, c_hi = st

        def body(j, m):
            for xu in _lane_tiles(sc_ref[j]):
                m = jnp.maximum(m, jnp.where(xu < hi, xu, -inf))
            return m
        below = _rep_max(lax.fori_loop(0, nkb, body, zeros - inf))
        below = jnp.where(jnp.logical_and(tail > 0.5, neg < hi), jnp.maximum(below, neg), below)
        c = count_ge(below)
        live = done < 0.5
        fin = jnp.logical_and(live, c >= kf)
        mv = jnp.logical_and(live, c < kf)
        return (jnp.where(fin, 1.0, done), jnp.where(fin, below, thr), jnp.where(fin, c_hi, n_gt),
                jnp.where(fin, jnp.where(c > kf, 1.0, 0.0), tie),
                lo, jnp.where(mv, below, hi), jnp.where(mv, c, c_hi))

    def not_done(st):
        return _any_true(1.0 - st[0])

    st = (done, thr, n_gt, tie, lo, hi, c_hi)
    st = lax.fori_loop(0, BISECT_BLIND_PASSES, lambda _, s: bisect(s), st)
    _, st = lax.while_loop(
        lambda ps: jnp.logical_and(ps[0] < BISECT_TESTED_ROUNDS, not_done(ps[1])),
        lambda ps: (ps[0] + 1, bisect(bisect(ps[1]))), (jnp.int32(0), st))
    st = lax.while_loop(not_done, lambda s: snap(bisect(s)), st)
    done, thr, n_gt, tie, lo, hi, c_hi = st

    need = kf - n_gt
    n_stored = nkb * w

    def count_eq_upto(jb):
        def body(j, acc):
            for u, xu in enumerate(_lane_tiles(sc_ref[j])):
                idx = j * w + u * LANES + lane
                acc = acc + jnp.where(jnp.logical_and(xu == thr, idx <= jb), 1.0, 0.0)
            return acc
        return _rep_sum(lax.fori_loop(0, nkb, body, zeros))

    def jstep(pj):
        p, (jlo, jhi) = pj
        jmid = jlo + lax.shift_right_arithmetic(jhi - jlo, 1)
        ok = count_eq_upto(jmid) >= need
        return p + 1, (jnp.where(ok, jlo, jmid), jnp.where(ok, jmid, jhi))

    n_jpass = jnp.where(_any_true(tie), 16, 0)
    jlo0 = jnp.full(shape, -1, I32)
    jhi0 = jnp.zeros(shape, I32) + n_stored
    _, (_, jhi) = lax.while_loop(lambda pj: pj[0] < n_jpass, jstep, (jnp.int32(0), (jlo0, jhi0)))
    jmax = jnp.where(tie > 0.5, jhi, jnp.int32(2 ** 30))
    return thr, jmax


def _attn_prompt_kernel(qi_ref, kw_ref, qb_ref, kit_ref, kt_ref, v_ref, o_ref,
                        sc_ref, wb_ref, qc_ref, qh_ref, sa_ref, sb_ref, mxa_ref, mxb_ref, ba_ref, bb_ref,
                        pa_ref, pb_ref,
                        m_ref, acc_ref,
                        *, n_keys, n_sel):
    tq, tk = ATT_TQ, ATT_TK
    tiles = [slice(u * LANES, (u + 1) * LANES) for u in range(tk // LANES)]
    i = pl.program_id(0)
    nkb = (i * tq + tq + tk - 1) // tk
    last_blk = n_keys // tk - 1
    q_pos = i * tq + lax.broadcasted_iota(I32, (tq, LANES), 0)
    lane = lax.broadcasted_iota(I32, (tq, LANES), 1)

    @pl.when(i == 0)
    def _():
        sc_ref[...] = jnp.full(sc_ref.shape, NEG_SCORE, F32)

    kw = kw_ref[...]
    for h in range(H_I):
        hi, lo = _split_bf16(qi_ref[:, h * D_I:(h + 1) * D_I])
        qc_ref[h] = jnp.concatenate([hi, lo, hi], axis=1)
        wb_ref[h] = jnp.broadcast_to(kw[:, D_I + h:D_I + h + 1], (tq, LANES))

    def score_body(jj, _):
        for h in range(H_I):
            wbh = wb_ref[h]
            for j in (2 * jj, 2 * jj + 1):
                d = jnp.dot(qc_ref[h], kit_ref[j], preferred_element_type=F32)
                for u, us in enumerate(tiles):
                    val = wbh * jnp.maximum(d[:, us], 0.0)
                    if h > 0:
                        val = sc_ref[j, :, us] + val
                    if h == H_I - 1:
                        val = jnp.where(j * tk + u * LANES + lane <= q_pos, val, NEG_SCORE)
                    sc_ref[j, :, us] = val
        return 0

    lax.fori_loop(0, (nkb + 1) // 2, score_body, 0)

    tail = jnp.zeros((tq, LANES), I32) + (n_keys - nkb * tk)
    thr, jmax = _select_rows(sc_ref, nkb, tail, n_sel)

    m_ref[...] = jnp.full(m_ref.shape, MASK_VALUE, F32)
    acc_ref[...] = jnp.zeros(acc_ref.shape, F32)
    ones_blk = jnp.ones((tk, LANES), BF16)
    for h in range(H_B):
        qh_ref[h] = (qb_ref[:, h * HD_B:(h + 1) * HD_B] * (HD_B ** -0.5 * LOG2_E)).astype(BF16)
    first_half = lane < HD_B

    def selection_bias(blk, b_ref):
        for u, us in enumerate(tiles):
            xu = sc_ref[blk, :, us]
            idx = blk * tk + u * LANES + lane
            sel = jnp.logical_or(xu > thr, jnp.logical_and(xu == thr, idx <= jmax))
            sel = jnp.logical_and(sel, idx <= q_pos)
            b_ref[:, us] = jnp.where(sel, 0.0, MASK_VALUE)

    def masked_logits(blk, h, s_ref, mx_ref, b_ref):
        s = jnp.dot(qh_ref[h], kt_ref[blk, h * HD_B:(h + 1) * HD_B, :], preferred_element_type=F32) + b_ref[...]
        s_ref[h] = s
        mx = s[:, tiles[0]]
        for us in tiles[1:]:
            mx = jnp.maximum(mx, s[:, us])
        mx_ref[h] = mx

    def reduce_block(blk, s_ref, mx_ref, nxt, b_ref, p_ref):
        rows = pl.ds(pl.multiple_of(blk * tk, tk), tk)
        for pair in range(H_B // 2):
            ps = slice(pair * LANES, (pair + 1) * LANES)
            v_ext = jnp.concatenate([v_ref[rows, ps], ones_blk], axis=1)
            for half in range(2):
                h = 2 * pair + half
                m_old = m_ref[h]
                m_new = jnp.maximum(m_old, _rep_max(mx_ref[h]))
                alpha = jnp.exp2(m_old - m_new)
                acc_ref[h] = jnp.tile(alpha, (1, 2)) * acc_ref[h]
                for us in tiles:
                    p_ref[h, :, us] = jnp.exp2(s_ref[h, :, us] - m_new).astype(BF16)
                m_ref[h] = m_new
                masked_logits(nxt, h, s_ref, mx_ref, b_ref)
                acc_ref[h] += jnp.dot(p_ref[h], v_ext, preferred_element_type=F32)

    selection_bias(0, ba_ref)
    selection_bias(1, bb_ref)
    for h in range(H_B):
        masked_logits(0, h, sa_ref, mxa_ref, ba_ref)
        masked_logits(1, h, sb_ref, mxb_ref, bb_ref)

    def pair_body(jj, _):
        a = 2 * jj
        nxt_a = jnp.minimum(a + 2, last_blk)
        nxt_b = jnp.minimum(a + 3, last_blk)
        selection_bias(nxt_a, ba_ref)
        selection_bias(nxt_b, bb_ref)
        reduce_block(a, sa_ref, mxa_ref, nxt_a, ba_ref, pa_ref)
        reduce_block(a + 1, sb_ref, mxb_ref, nxt_b, bb_ref, pb_ref)
        return 0

    lax.fori_loop(0, (nkb + 1) // 2, pair_body, 0)

    for pair in range(H_B // 2):
        ps = slice(pair * LANES, (pair + 1) * LANES)
        a0, a1 = acc_ref[2 * pair], acc_ref[2 * pair + 1]
        o_ref[:, ps] = jnp.where(first_half, a0[:, :LANES] / a0[:, LANES:], a1[:, :LANES] / a1[:, LANES:])


def _attn_prompt(qi, kw, qb, kit_blocks, kt_blocks, v_bf, n_sel):
    s = qi.shape[0]
    nkb_total = s // ATT_TK
    row = lambda i: (i, 0)
    whole = pl.BlockSpec(memory_space=pltpu.VMEM)
    return pl.pallas_call(
        functools.partial(_attn_prompt_kernel, n_keys=s, n_sel=n_sel),
        grid=(s // ATT_TQ,),
        in_specs=[
            pl.BlockSpec((ATT_TQ, GROUP), row),
            pl.BlockSpec((ATT_TQ, LANES), row),
            pl.BlockSpec((ATT_TQ, GROUP), row),
            whole, whole, whole,
        ],
        out_specs=pl.BlockSpec((ATT_TQ, GROUP), row),
        out_shape=jax.ShapeDtypeStruct((s, GROUP), F32),
        scratch_shapes=[
            pltpu.VMEM((nkb_total, ATT_TQ, ATT_TK), F32),
            pltpu.VMEM((H_I, ATT_TQ, LANES), F32),
            pltpu.VMEM((H_I, ATT_TQ, 3 * D_I), BF16),
            pltpu.VMEM((H_B, ATT_TQ, HD_B), BF16),
            pltpu.VMEM((H_B, ATT_TQ, ATT_TK), F32),
            pltpu.VMEM((H_B, ATT_TQ, ATT_TK), F32),
            pltpu.VMEM((H_B, ATT_TQ, LANES), F32),
            pltpu.VMEM((H_B, ATT_TQ, LANES), F32),
            pltpu.VMEM((ATT_TQ, ATT_TK), F32),
            pltpu.VMEM((ATT_TQ, ATT_TK), F32),
            pltpu.VMEM((H_B, ATT_TQ, ATT_TK), BF16),
            pltpu.VMEM((H_B, ATT_TQ, ATT_TK), BF16),
            pltpu.VMEM((H_B, ATT_TQ, LANES), F32),
            pltpu.VMEM((H_B, ATT_TQ, 2 * LANES), F32),
        ],
        compiler_params=pltpu.CompilerParams(
            dimension_semantics=("arbitrary",), vmem_limit_bytes=VMEM_LIMIT_BYTES),
        name="attn_prompt",
    )(qi, kw, qb, kit_blocks, kt_blocks, v_bf)


def _attn_sample_kernel(pt_ref, qi_ref, kw_ref, qb_ref, kin_ref, kn_ref, vn_ref,
                        ckidx_hbm, ck_hbm, cv_hbm, o_ref,
                        kid_buf, k_buf, v_buf, sc_ref, m_ref, l_ref, acc_ref, sem_i, sem_k, sem_v,
                        *, n_pages, t_new, n_sel):
    b = pl.program_id(0)
    r = SUBLANES
    nrow = H_B * r
    n_chunks = n_pages // KV_CHUNK_PAGES

    def page_lanes(p):
        start = (p % KV_CHUNK_PAGES) * PAGE_SIZE
        return pl.ds(start if isinstance(start, int) else pl.multiple_of(start, PAGE_SIZE), PAGE_SIZE)

    def kidx_copy(p):
        return pltpu.make_async_copy(ckidx_hbm.at[pt_ref[b, p]],
                                     kid_buf.at[p // KV_CHUNK_PAGES, :, page_lanes(p)], sem_i)

    def kv_copies(c, slot, p):
        page = pt_ref[b, c * KV_CHUNK_PAGES + p]
        return (pltpu.make_async_copy(ck_hbm.at[page], k_buf.at[slot, :, :, page_lanes(p)], sem_k.at[slot]),
                pltpu.make_async_copy(cv_hbm.at[page], v_buf.at[slot, :, :, page_lanes(p)], sem_v.at[slot]))

    def start_chunk(c, slot):
        for p in range(KV_CHUNK_PAGES):
            ck, cv = kv_copies(c, slot, p)
            ck.start()
            cv.start()

    def wait_chunk(c, slot):
        for p in range(KV_CHUNK_PAGES):
            ck, cv = kv_copies(c, slot, p)
            ck.wait()
            cv.wait()

    def start_kidx(p, _):
        kidx_copy(p).start()
        return 0

    def wait_kidx(p, _):
        kidx_copy(p).wait()
        return 0

    lax.fori_loop(0, n_pages, start_kidx, 0)
    for c0 in range(KV_SLOTS - 1):
        start_chunk(c0, c0)

    kw = kw_ref[0]
    qi = qi_ref[0]
    q_hi, q_lo, w_rows = [], [], []
    for h in range(H_I):
        hi, lo = _split_bf16(qi[:, h * D_I:(h + 1) * D_I])
        q_hi.append(hi)
        q_lo.append(lo)
        w_rows.append(jnp.broadcast_to(kw[:, D_I + h:D_I + h + 1], (r, LANES)))
    q_hi = jnp.concatenate(q_hi, axis=0)
    q_hl = jnp.concatenate([q_hi, jnp.concatenate(q_lo, axis=0)], axis=0)
    w_rows = jnp.concatenate(w_rows, axis=0)
    nt = (((1,), (1,)), ((), ()))

    def key_scores(keys_t):
        k_hi, k_lo = _split_bf16(keys_t)
        d2 = jnp.dot(q_hl, k_hi, preferred_element_type=F32)
        d = d2[:H_I * r] + d2[H_I * r:] + jnp.dot(q_hi, k_lo, preferred_element_type=F32)
        out = []
        for u in range(keys_t.shape[1] // LANES):
            us = slice(u * LANES, (u + 1) * LANES)
            wd = w_rows * jnp.maximum(d[:, us], 0.0)
            acc = wd[0:r]
            for h in range(1, H_I):
                acc = acc + wd[h * r:(h + 1) * r]
            out.append(acc)
        return out

    lax.fori_loop(0, n_pages, wait_kidx, 0)

    def score_body(c, _):
        for u, tile in enumerate(key_scores(kid_buf[c])):
            sc_ref[c, :, u * LANES:(u + 1) * LANES] = tile
        return 0

    lax.fori_loop(0, n_chunks, score_body, 0)

    lane = lax.broadcasted_iota(I32, (r, LANES), 1)
    t_row = jnp.minimum(lax.broadcasted_iota(I32, (r, LANES), 0), t_new - 1)
    new_sc = key_scores(kin_ref[0])[0]
    new_vis = lane <= t_row
    sc_ref[n_chunks] = jnp.full((r, KV_CHUNK_PAGES * LANES), -jnp.inf, F32)
    sc_ref[n_chunks, :, 0:LANES] = jnp.where(lane >= t_new, -jnp.inf, jnp.where(new_vis, new_sc, NEG_SCORE))

    thr, jmax = _select_rows(sc_ref, n_chunks + 1, jnp.zeros((r, LANES), I32), n_sel)

    qb = qb_ref[0] * (HD_B ** -0.5)
    col_head = lax.shift_right_logical(lax.broadcasted_iota(I32, (r, GROUP), 1), HD_B.bit_length() - 1)
    q_bd = jnp.concatenate([jnp.where(col_head == h, qb, 0.0) for h in range(H_B)], axis=0).astype(BF16)

    m_ref[...] = jnp.full(m_ref.shape, MASK_VALUE, F32)
    l_ref[...] = jnp.zeros(l_ref.shape, F32)
    acc_ref[...] = jnp.zeros(acc_ref.shape, F32)

    def attend(blk, n_blk, k_t, v_t, extra_mask):
        bias = []
        for u in range(n_blk):
            x = sc_ref[blk, :, u * LANES:(u + 1) * LANES]
            idx = (blk * KV_CHUNK_PAGES + u) * LANES + lane
            sel = jnp.logical_or(x > thr, jnp.logical_and(x == thr, idx <= jmax))
            if extra_mask is not None:
                sel = jnp.logical_and(sel, extra_mask)
            bias.append(jnp.where(sel, 0.0, MASK_VALUE))
        bias = jnp.tile(jnp.concatenate(bias, axis=1), (H_B, 1))
        s = jnp.dot(q_bd, k_t.astype(BF16), preferred_element_type=F32) + bias
        m_old = m_ref[...]
        m_new = jnp.maximum(m_old, jnp.broadcast_to(jnp.max(s, axis=1, keepdims=True), m_old.shape))
        alpha = jnp.exp(m_old - m_new)
        p = jnp.exp(s - jnp.tile(m_new, (1, n_blk)))
        p_sum = p[:, :LANES]
        for u in range(1, n_blk):
            p_sum = p_sum + p[:, u * LANES:(u + 1) * LANES]
        l_ref[...] = alpha * l_ref[...] + p_sum
        pv = lax.dot_general(p.astype(BF16), v_t.astype(BF16), nt, preferred_element_type=F32)
        acc_ref[...] = jnp.tile(alpha, (1, GROUP // LANES)) * acc_ref[...] + pv
        m_ref[...] = m_new

    def chunk_body(c, _):
        slot = c % KV_SLOTS
        wait_chunk(c, slot)
        ahead = c + KV_SLOTS - 1

        @pl.when(ahead < n_chunks)
        def _():
            start_chunk(ahead, ahead % KV_SLOTS)

        n_keys_chunk = KV_CHUNK_PAGES * PAGE_SIZE
        attend(c, KV_CHUNK_PAGES, k_buf[slot].reshape(GROUP, n_keys_chunk),
               v_buf[slot].reshape(GROUP, n_keys_chunk), None)
        return 0

    lax.fori_loop(0, n_chunks, chunk_body, 0)
    attend(n_chunks, 1, kn_ref[0].reshape(GROUP, PAGE_SIZE), vn_ref[0].reshape(GROUP, PAGE_SIZE), new_vis)

    out = jnp.zeros((r, GROUP), F32)
    for h in range(H_B):
        rs = slice(h * r, (h + 1) * r)
        denom = jnp.sum(l_ref[rs, :], axis=1, keepdims=True)
        out = out + jnp.where(col_head == h, acc_ref[rs, :] / denom, 0.0)
    o_ref[0] = out


def _attn_sample(page_table, qi, kw, qb, ki_new, k_new, v_new, cache_kidx, cache_k, cache_v, t_new, n_sel):
    nb, n_pages = page_table.shape
    r = SUBLANES
    nrow = H_B * r
    per_b = lambda b, pt: (b, 0, 0)
    per_b4 = lambda b, pt: (b, 0, 0, 0)
    anyspec = pl.BlockSpec(memory_space=pl.ANY)
    chunk_keys = KV_CHUNK_PAGES * PAGE_SIZE
    n_chunks = n_pages // KV_CHUNK_PAGES
    grid_spec = pltpu.PrefetchScalarGridSpec(
        num_scalar_prefetch=1,
        grid=(nb,),
        in_specs=[
            pl.BlockSpec((1, r, GROUP), per_b),
            pl.BlockSpec((1, r, LANES), per_b),
            pl.BlockSpec((1, r, GROUP), per_b),
            pl.BlockSpec((1, D_I, PAGE_SIZE), per_b),
            pl.BlockSpec((1, H_B, HD_B, PAGE_SIZE), per_b4),
            pl.BlockSpec((1, H_B, HD_B, PAGE_SIZE), per_b4),
            anyspec, anyspec, anyspec,
        ],
        out_specs=pl.BlockSpec((1, r, GROUP), per_b),
        scratch_shapes=[
            pltpu.VMEM((n_chunks, D_I, chunk_keys), F32),
            pltpu.VMEM((KV_SLOTS, H_B, HD_B, chunk_keys), F32),
            pltpu.VMEM((KV_SLOTS, H_B, HD_B, chunk_keys), F32),
            pltpu.VMEM((n_chunks + 1, r, chunk_keys), F32),
            pltpu.VMEM((nrow, LANES), F32),
            pltpu.VMEM((nrow, LANES), F32),
            pltpu.VMEM((nrow, GROUP), F32),
            pltpu.SemaphoreType.DMA(()),
            pltpu.SemaphoreType.DMA((KV_SLOTS,)),
            pltpu.SemaphoreType.DMA((KV_SLOTS,)),
        ],
    )
    return pl.pallas_call(
        functools.partial(_attn_sample_kernel, n_pages=n_pages, t_new=t_new, n_sel=n_sel),
        grid_spec=grid_spec,
        out_shape=jax.ShapeDtypeStruct((nb, r, GROUP), F32),
        compiler_params=pltpu.CompilerParams(
            dimension_semantics=("arbitrary",), vmem_limit_bytes=VMEM_LIMIT_BYTES),
        name="attn_sample",
    )(page_table, qi, kw, qb, ki_new, k_new, v_new, cache_kidx, cache_k, cache_v)


def _out_ffn_kernel(x_ref, oa_ref, ob_ref, p1_ref, p2_ref, wo_ref, g2_ref, wu_ref, cw_ref, cb_ref, wd_ref, gf_ref,
                    y_ref, u_ref, carry_ref, x1_ref, h2_ref, *, d_ff, ff_tile, seq_rows, carry_mode):
    tm = x_ref.shape[0]
    i = pl.program_id(0)
    mixed = jnp.concatenate([oa_ref[...], ob_ref[...]], axis=1).astype(BF16)
    x1 = x_ref[...] + jnp.dot(mixed, wo_ref[...], preferred_element_type=F32)
    h2_ref[...] = _rms(x1, g2_ref[...]).astype(BF16)
    x1_ref[...] = x1

    row = lax.broadcasted_iota(I32, (tm, ff_tile), 0)
    if carry_mode:
        @pl.when(i == 0)
        def _():
            carry_ref[...] = jnp.zeros(carry_ref.shape, F32)
        t_in_seq = row
    else:
        t_in_seq = row & (seq_rows - 1)

    for c in range(d_ff // ff_tile):
        cs = slice(c * ff_tile, (c + 1) * ff_tile)
        h2 = h2_ref[...]
        u = jnp.dot(h2, wu_ref[:, cs], preferred_element_type=F32)
        v = jnp.dot(h2, wu_ref[:, d_ff + c * ff_tile:d_ff + (c + 1) * ff_tile], preferred_element_type=F32)
        if carry_mode:
            prev = carry_ref[:, cs]
            p1 = jnp.broadcast_to(prev[SUBLANES - 1:SUBLANES], (tm, ff_tile))
            p2 = jnp.where(row == 0, jnp.broadcast_to(prev[SUBLANES - 2:SUBLANES - 1], (tm, ff_tile)), p1)
            carry_ref[:, cs] = u[tm - SUBLANES:]
            u_ref[:, cs] = u[tm - SUBLANES:]
        else:
            p1 = p1_ref[:, cs]
            p2 = p2_ref[:, cs]
            u_ref[:, cs] = u
        u1 = jnp.where(t_in_seq >= 1, pltpu.roll(u, 1, 0), p1)
        u2 = jnp.where(t_in_seq >= 2, pltpu.roll(u, 2, 0), p2)
        conv = cb_ref[:, cs] + cw_ref[0:1, cs] * u2 + cw_ref[1:2, cs] * u1 + cw_ref[2:3, cs] * u
        gate = (_silu(conv) * v).astype(BF16)
        x1_ref[...] += jnp.dot(gate, wd_ref[cs, :], preferred_element_type=F32)

    y_ref[...] = _rms(x1_ref[...], gf_ref[...])


def _out_ffn(x2d, oa, ob, p1, p2, w_out, g2, w_up, conv_w, conv_b, w_down, gf, tm, seq_rows, carry_mode):
    m, d = x2d.shape
    d_ff = w_down.shape[0]
    ff_tile = 256
    row = lambda i: (i, 0)
    const = lambda i: (0, 0)
    if carry_mode:
        prev_spec = pl.BlockSpec((SUBLANES, d_ff), const)
        u_spec = pl.BlockSpec((SUBLANES, d_ff), const)
        u_shape = jax.ShapeDtypeStruct((SUBLANES, d_ff), F32)
    else:
        prev_spec = pl.BlockSpec((tm, d_ff), row)
        u_spec = pl.BlockSpec((tm, d_ff), row)
        u_shape = jax.ShapeDtypeStruct((m, d_ff), F32)
    return pl.pallas_call(
        functools.partial(_out_ffn_kernel, d_ff=d_ff, ff_tile=ff_tile, seq_rows=seq_rows, carry_mode=carry_mode),
        grid=(m // tm,),
        in_specs=[
            pl.BlockSpec((tm, d), row),
            pl.BlockSpec((tm, GROUP), row),
            pl.BlockSpec((tm, GROUP), row),
            prev_spec, prev_spec,
            pl.BlockSpec(w_out.shape, const, pipeline_mode=pl.Buffered(1)),
            pl.BlockSpec((1, d), const),
            pl.BlockSpec(w_up.shape, const, pipeline_mode=pl.Buffered(1)),
            pl.BlockSpec(conv_w.shape, const),
            pl.BlockSpec((1, d_ff), const),
            pl.BlockSpec(w_down.shape, const, pipeline_mode=pl.Buffered(1)),
            pl.BlockSpec((1, d), const),
        ],
        out_specs=[pl.BlockSpec((tm, d), row), u_spec],
        out_shape=[jax.ShapeDtypeStruct((m, d), F32), u_shape],
        scratch_shapes=[pltpu.VMEM((SUBLANES, d_ff), F32), pltpu.VMEM((tm, d), F32), pltpu.VMEM((tm, d), BF16)],
        compiler_params=pltpu.CompilerParams(
            dimension_semantics=("arbitrary",), vmem_limit_bytes=VMEM_LIMIT_BYTES),
        name="out_ffn",
    )(x2d, oa, ob, p1, p2, w_out, g2, w_up, conv_w, conv_b, w_down, gf)


def _rope_tables(pos):
    half = ROT // 2
    inv = jnp.power(ROPE_THETA, -jnp.arange(half, dtype=F32) * (2.0 / ROT))
    ang = pos.astype(F32)[:, None] * inv[None, :]
    cos, sin = jnp.cos(ang), jnp.sin(ang)
    n = pos.shape[0]
    ones = jnp.ones((n, HD_B - ROT), F32)
    zeros = jnp.zeros((n, HD_B - ROT), F32)
    zh = jnp.zeros((n, half), F32)
    c = jnp.concatenate([cos, cos, ones], axis=1)
    sa = jnp.concatenate([zh, sin, zeros], axis=1)
    sb = jnp.concatenate([-sin, zh, zeros], axis=1)
    rep = LANES // HD_B
    return jnp.tile(c, (1, rep)), jnp.tile(sa, (1, rep)), jnp.tile(sb, (1, rep))


def kernel(x_prompt, x_sample, cache_k, cache_v, cache_kidx, state_hgrn, state_conv, page_table,
           norm_mix_gain, w_in, lb_logits, hgrn_norm_gain, w_out, norm_ffn_gain, w_up, conv_w,
           conv_b, w_down, final_norm_gain):
    bp, s, d = x_prompt.shape
    nb, t_new, _ = x_sample.shape
    depth = w_in.shape[0]
    assert bp == 1 and depth == 1
    n_pages = page_table.shape[1]
    past = n_pages * PAGE_SIZE
    d_ff = w_down.shape[1]
    assert s % (2 * ATT_TK) == 0 and s % HGRN_CHUNK == 0
    assert n_pages % KV_CHUNK_PAGES == 0 and n_pages // KV_CHUNK_PAGES >= KV_SLOTS
    assert t_new <= SUBLANES and CONV_W - 1 <= t_new

    lower_bounds = jnp.cumsum(jax.nn.softmax(lb_logits.astype(F32), axis=0), axis=0)
    lb = lower_bounds[0][None, :]
    n_in = w_in.shape[2]
    n_pad = 8 * GROUP + LANES
    w_in_bf = jnp.pad(w_in[0], ((0, 0), (0, n_pad - n_in))).astype(BF16)
    w_out_bf = w_out[0].astype(BF16)
    w_up_bf = w_up[0].astype(BF16)
    w_down_bf = w_down[0].astype(BF16)
    g_mix = norm_mix_gain[0][None, :]
    g_ffn = norm_ffn_gain[0][None, :]
    g_fin = final_norm_gain[None, :]
    g_hgrn = hgrn_norm_gain[0][None, :]
    conv_b2 = conv_b[0][None, :]

    xp = x_prompt.reshape(s, d)
    cp, sap, sbp = _rope_tables(jnp.arange(s, dtype=I32))
    qa, ka, lf, ia, ga, qb, kb, vb, qi, kw = _proj_in(xp, g_mix, w_in_bf, lb, cp, sap, sbp, tm=PROJ_ROWS)

    s0 = jnp.zeros((1, H_A, DK_A, DK_A), F32)
    r3 = lambda a: a.reshape(1, s, GROUP)
    oa_p, st_p = _hgrn(r3(qa), r3(ka), r3(lf), r3(ia), r3(ga), g_hgrn, s0, HGRN_CHUNK, HGRN_SUB)

    nkb = s // ATT_TK
    ki_hi, ki_lo = _split_bf16(kw[:, :D_I])
    kit = jnp.concatenate([ki_hi, ki_hi, ki_lo], axis=1)
    kit_blocks = kit.reshape(nkb, ATT_TK, 3 * D_I).transpose(0, 2, 1)
    kt_blocks = kb.astype(BF16).reshape(nkb, ATT_TK, GROUP).transpose(0, 2, 1)
    ob_p = _attn_prompt(qi, kw, qb, kit_blocks, kt_blocks, vb.astype(BF16), min(TOPK_MAX, s // 4))

    zero_prev = jnp.zeros((SUBLANES, d_ff), F32)
    y_p, u_tail = _out_ffn(xp, oa_p.reshape(s, GROUP), ob_p, zero_prev, zero_prev, w_out_bf, g_ffn, w_up_bf,
                           conv_w[0], conv_b2, w_down_bf, g_fin, tm=PROJ_ROWS, seq_rows=s, carry_mode=True)

    ms = nb * t_new
    xs = x_sample.reshape(ms, d)
    pos_s = jnp.tile(past + jnp.arange(t_new, dtype=I32), nb)
    cs, sas, sbs = _rope_tables(pos_s)
    qa2, ka2, lf2, ia2, ga2, qb2, kb2, vb2, qi2, kw2 = _proj_in(xs, g_mix, w_in_bf, lb, cs, sas, sbs, tm=ms)

    pad_front = SUBLANES - t_new
    fp = lambda a: jnp.pad(a.reshape(nb, t_new, GROUP), ((0, 0), (pad_front, 0), (0, 0)))
    s0_s = jnp.swapaxes(state_hgrn[0], -1, -2)
    oa_s, st_s = _hgrn(fp(qa2), fp(ka2), fp(lf2), fp(ia2), fp(ga2), g_hgrn, s0_s, SUBLANES, SUBLANES)
    oa_s = oa_s[:, pad_front:, :].reshape(ms, GROUP)

    def rp(a):
        a = a.reshape(nb, t_new, a.shape[-1])
        return jnp.concatenate([a, jnp.broadcast_to(a[:, -1:], (nb, SUBLANES - t_new, a.shape[-1]))], axis=1)
    page_pad = lambda a: jnp.pad(a.reshape(nb, t_new, a.shape[-1]), ((0, 0), (0, PAGE_SIZE - t_new), (0, 0)))
    new_page = lambda a: jnp.swapaxes(page_pad(a), 1, 2)
    heads = lambda a: new_page(a).reshape(nb, H_B, HD_B, PAGE_SIZE)
    ob_s = _attn_sample(
        page_table, rp(qi2), rp(kw2), rp(qb2), new_page(kw2[:, :D_I]), heads(kb2), heads(vb2),
        jnp.transpose(cache_kidx[0], (0, 2, 1)), jnp.transpose(cache_k[0], (0, 2, 3, 1)),
        jnp.transpose(cache_v[0], (0, 2, 3, 1)),
        t_new, min(TOPK_MAX, (past + t_new) // 4))
    ob_s = ob_s[:, :t_new, :].reshape(ms, GROUP)

    sc0 = state_conv[0]
    zrow = jnp.zeros((nb, t_new - 1, d_ff), F32)
    p1 = jnp.concatenate([sc0[:, 1:2], zrow], axis=1).reshape(ms, d_ff)
    p2 = jnp.concatenate([sc0[:, 0:2], zrow[:, 1:]], axis=1).reshape(ms, d_ff)
    y_s, u_s = _out_ffn(xs, oa_s, ob_s, p1, p2, w_out_bf, g_ffn, w_up_bf, conv_w[0], conv_b2, w_down_bf, g_fin,
                        tm=ms, seq_rows=t_new, carry_mode=False)

    return (
        y_p.reshape(1, s, d),
        y_s.reshape(nb, t_new, d),
        kb.reshape(1, 1, s, H_B, HD_B),
        vb.reshape(1, 1, s, H_B, HD_B),
        kw[:, :D_I].reshape(1, 1, s, D_I),
        jnp.swapaxes(st_p, -1, -2).reshape(1, 1, H_A, DK_A, DK_A),
        u_tail[SUBLANES - (CONV_W - 1):].reshape(1, 1, CONV_W - 1, d_ff),
        kb2.reshape(1, nb, t_new, H_B, HD_B),
        vb2.reshape(1, nb, t_new, H_B, HD_B),
        kw2[:, :D_I].reshape(1, nb, t_new, D_I),
        jnp.swapaxes(st_s, -1, -2).reshape(1, nb, H_A, DK_A, DK_A),
        u_s.reshape(nb, t_new, d_ff)[:, t_new - (CONV_W - 1):].reshape(1, nb, CONV_W - 1, d_ff),
    )
```

```python
import functools

import jax
import jax.numpy as jnp
from jax import lax
from jax.experimental import pallas as pl
from jax.experimental.pallas import tpu as pltpu

F32 = jnp.float32
BF16 = jnp.bfloat16
I32 = jnp.int32

H_A = 4
DK_A = 128
H_B = 8
HD_B = 64
H_I = 8
D_I = 64
ROT = 16
ROPE_THETA = 500000.0
TOPK_MAX = 256
PAGE_SIZE = 128
CONV_W = 3
EPS = 1e-6
NEG_SCORE = -1e30
MASK_VALUE = -1e30
GROUP = 512
LOG2_E = 1.4426950408889634

LANES = 128
SUBLANES = 8
VMEM_LIMIT_BYTES = 61 * 1024 * 1024

PROJ_ROWS = 512
HGRN_CHUNK = 64
HGRN_SUB = 16
BISECT_BLIND_PASSES = 10
BISECT_TESTED_ROUNDS = 8
ATT_TQ = 128
ATT_TK = 512
KV_CHUNK_PAGES = 8
KV_SLOTS = 4


def _silu(x):
    return x * jax.nn.sigmoid(x)


def _rms(x, g):
    return x * lax.rsqrt(jnp.mean(x * x, axis=-1, keepdims=True) + EPS) * g


def _split_bf16(x):
    hi = x.astype(BF16)
    lo = (x - hi.astype(F32)).astype(BF16)
    return hi, lo


def _rope(x, c, sa, sb):
    w = x.shape[1]
    return x * c + pltpu.roll(x, ROT // 2, 1) * sa + pltpu.roll(x, w - ROT // 2, 1) * sb


def _proj_in_kernel(x_ref, g_ref, w_ref, lb_ref, c_ref, sa_ref, sb_ref,
                    qa_ref, ka_ref, lf_ref, ia_ref, ga_ref, qb_ref, kb_ref, vb_ref, qi_ref, kw_ref, h_ref):
    h_ref[...] = _rms(x_ref[...], g_ref[...]).astype(BF16)

    def grp(i, width=GROUP):
        return jnp.dot(h_ref[...], w_ref[:, i * GROUP:i * GROUP + width], preferred_element_type=F32)

    c = jnp.tile(c_ref[...], (1, GROUP // LANES))
    sa = jnp.tile(sa_ref[...], (1, GROUP // LANES))
    sb = jnp.tile(sb_ref[...], (1, GROUP // LANES))

    qa_ref[...] = _silu(grp(0))
    lb = lb_ref[...]
    fg = lb + (1.0 - lb) * jax.nn.sigmoid(grp(1))
    ka_ref[...] = 1.0 - fg
    lf_ref[...] = jnp.log(fg)
    ia_ref[...] = grp(2)
    ga_ref[...] = _silu(grp(3))
    qb_ref[...] = _rope(grp(4), c, sa, sb)
    kb_ref[...] = _rope(grp(5), c, sa, sb)
    vb_ref[...] = grp(6)
    qi_ref[...] = _rope(grp(7), c, sa, sb)
    kw = grp(8, LANES)
    lane = lax.broadcasted_iota(I32, kw.shape, 1)
    roped = _rope(kw, c_ref[...], sa_ref[...], sb_ref[...])
    kw_ref[...] = jnp.where(lane < D_I, roped, kw * (H_I ** -0.5 * D_I ** -0.5))


def _proj_in(x2d, gain, w_pad, lb, cos_t, sin_a, sin_b, tm):
    m, d = x2d.shape
    n_pad = w_pad.shape[1]
    row = lambda i: (i, 0)
    const = lambda i: (0, 0)
    big = pl.BlockSpec((tm, GROUP), row)
    out_shapes = [jax.ShapeDtypeStruct((m, GROUP), F32)] * 9 + [jax.ShapeDtypeStruct((m, LANES), F32)]
    return pl.pallas_call(
        _proj_in_kernel,
        grid=(m // tm,),
        in_specs=[
            pl.BlockSpec((tm, d), row),
            pl.BlockSpec((1, d), const),
            pl.BlockSpec((d, n_pad), const, pipeline_mode=pl.Buffered(1)),
            pl.BlockSpec((1, GROUP), const),
            pl.BlockSpec((tm, LANES), row),
            pl.BlockSpec((tm, LANES), row),
            pl.BlockSpec((tm, LANES), row),
        ],
        out_specs=[big] * 9 + [pl.BlockSpec((tm, LANES), row)],
        out_shape=out_shapes,
        scratch_shapes=[pltpu.VMEM((tm, d), BF16)],
        compiler_params=pltpu.CompilerParams(
            dimension_semantics=("arbitrary",), vmem_limit_bytes=VMEM_LIMIT_BYTES),
        name="proj_in",
    )(x2d, gain, w_pad, lb, cos_t, sin_a, sin_b)


def _shift_rows(x, d):
    if d == 0:
        return x
    return pltpu.roll(x, d, 0)


def _cumsum_rows(g):
    c = g.shape[0]
    row = lax.broadcasted_iota(I32, g.shape, 0)
    k = 1
    while k < c:
        g = g + jnp.where(row >= k, _shift_rows(g, k), 0.0)
        k *= 2
    return g


def _hgrn_kernel(q_ref, k_ref, lf_ref, v_ref, ga_ref, gain_ref, s0_ref, o_ref, sfin_ref, st_ref, *, chunk, sub):
    ci = pl.program_id(1)

    @pl.when(ci == 0)
    def _():
        st_ref[...] = s0_ref[0]

    nsb = chunk // sub
    row = lax.broadcasted_iota(I32, (chunk, DK_A), 0)
    row_in_sub = row & (sub - 1)
    gain = gain_ref[...]

    for h in range(H_A):
        sl = slice(h * DK_A, (h + 1) * DK_A)
        q = q_ref[0, :, sl]
        k = k_ref[0, :, sl]
        v = v_ref[0, :, sl]
        b = _cumsum_rows(lf_ref[0, :, sl])
        st = st_ref[h]

        refs = [jnp.zeros((1, DK_A), F32)] + [b[i * sub - 1:i * sub, :] for i in range(1, nsb)]
        ref_rows = jnp.concatenate([jnp.broadcast_to(r, (sub, DK_A)) for r in refs], axis=0)
        q_rel = q * jnp.exp(b - ref_rows)

        o = lax.dot_general((q * jnp.exp(b)).astype(BF16), st.astype(BF16),
                            (((1,), (1,)), ((), ())), preferred_element_type=F32)

        off_rows = [jnp.zeros((sub, DK_A), F32)]
        for i in range(1, nsb):
            n_prev = i * sub
            k_rel = (k[:n_prev] * jnp.exp(refs[i] - b[:n_prev])).astype(BF16)
            att = lax.dot_general(q_rel[n_prev:n_prev + sub].astype(BF16), k_rel,
                                  (((1,), (1,)), ((), ())), preferred_element_type=F32)
            off_rows.append(jnp.dot(att.astype(BF16), v[:n_prev].astype(BF16), preferred_element_type=F32))
        if nsb > 1:
            o = o + jnp.concatenate(off_rows, axis=0)

        for d in range(sub):
            valid = row_in_sub >= d
            e = jnp.exp(jnp.where(valid, b - _shift_rows(b, d), 0.0))
            w = jnp.sum(q * _shift_rows(k, d) * e, axis=1, keepdims=True)
            o = o + jnp.where(valid, w * _shift_rows(v, d), 0.0)

        b_last = b[chunk - 1:chunk, :]
        k_dec = (k * jnp.exp(b_last - b)).astype(BF16)
        st_ref[h] = st * jnp.exp(b_last) + lax.dot_general(
            v.astype(BF16), k_dec, (((0,), (0,)), ((), ())), preferred_element_type=F32)

        o_ref[0, :, sl] = _rms(o, gain) * ga_ref[0, :, sl]

    @pl.when(ci == pl.num_programs(1) - 1)
    def _():
        sfin_ref[0] = st_ref[...]


def _hgrn(q, k, lf, v, ga, gain, s0_t, chunk, sub):
    b, t, w = q.shape
    blk = pl.BlockSpec((1, chunk, w), lambda bi, ci: (bi, ci, 0))
    st_spec = pl.BlockSpec((1, H_A, DK_A, DK_A), lambda bi, ci: (bi, 0, 0, 0))
    return pl.pallas_call(
        functools.partial(_hgrn_kernel, chunk=chunk, sub=sub),
        grid=(b, t // chunk),
        in_specs=[blk, blk, blk, blk, blk, pl.BlockSpec((1, DK_A), lambda bi, ci: (0, 0)), st_spec],
        out_specs=[blk, st_spec],
        out_shape=[jax.ShapeDtypeStruct((b, t, w), F32), jax.ShapeDtypeStruct((b, H_A, DK_A, DK_A), F32)],
        scratch_shapes=[pltpu.VMEM((H_A, DK_A, DK_A), F32)],
        compiler_params=pltpu.CompilerParams(
            dimension_semantics=("arbitrary", "arbitrary"), vmem_limit_bytes=VMEM_LIMIT_BYTES),
        name="hgrn",
    )(q, k, lf, v, ga, gain, s0_t)


def _rep_sum(x):
    return jnp.broadcast_to(jnp.sum(x, axis=1, keepdims=True), x.shape)


def _rep_max(x):
    return jnp.broadcast_to(jnp.max(x, axis=1, keepdims=True), x.shape)


def _rep_min(x):
    return jnp.broadcast_to(jnp.min(x, axis=1, keepdims=True), x.shape)


def _any_true(flag):
    return jnp.max(flag) > 0.5


def _lane_tiles(x):
    return [x[:, u * LANES:(u + 1) * LANES] for u in range(x.shape[1] // LANES)]


def _select_rows(sc_ref, nkb, n_hidden_tail, n_sel):
    _, r, w = sc_ref.shape
    shape = (r, LANES)
    neg = jnp.float32(NEG_SCORE)
    inf = jnp.float32(jnp.inf)
    kf = jnp.float32(n_sel)
    tail = n_hidden_tail.astype(F32)
    zeros = jnp.zeros(shape, F32)
    lane = lax.broadcasted_iota(I32, shape, 1)

    def count_ge(c):
        def body(j, acc):
            for xu in _lane_tiles(sc_ref[j]):
                acc = acc + jnp.where(xu >= c, 1.0, 0.0)
            return acc
        return _rep_sum(lax.fori_loop(0, nkb, body, zeros)) + jnp.where(neg >= c, tail, 0.0)

    def stats_body(j, carry):
        vmax, vmin_real, n_real = carry
        for x in _lane_tiles(sc_ref[j]):
            real = x > neg
            vmax = jnp.maximum(vmax, x)
            vmin_real = jnp.minimum(vmin_real, jnp.where(real, x, inf))
            n_real = n_real + jnp.where(real, 1.0, 0.0)
        return vmax, vmin_real, n_real

    def hidden_stats_body(j, carry):
        n_ge_neg, gmin = carry
        for x in _lane_tiles(sc_ref[j]):
            n_ge_neg = n_ge_neg + jnp.where(x >= neg, 1.0, 0.0)
            gmin = jnp.minimum(gmin, jnp.where(x > -inf, x, inf))
        return n_ge_neg, gmin

    vmax, vmin_real, n_real = lax.fori_loop(0, nkb, stats_body, (zeros - inf, zeros + inf, zeros))
    few = _any_true(jnp.where(_rep_sum(n_real) < kf, 1.0, 0.0))
    n_ge_neg, gmin = lax.fori_loop(0, jnp.where(few, nkb, 0), hidden_stats_body, (zeros, zeros + inf))
    has_tail = tail > 0.5
    vmax = _rep_max(vmax)
    vmax = jnp.where(has_tail, jnp.maximum(vmax, neg), vmax)
    vmin_real = _rep_min(vmin_real)
    gmin = _rep_min(gmin)
    gmin = jnp.where(has_tail, jnp.minimum(gmin, neg), gmin)
    n_real = _rep_sum(n_real)
    n_ge_neg = _rep_sum(n_ge_neg) + tail
    above_max = vmax + jnp.maximum(jnp.abs(vmax), 1e-30) * 1e-6

    few_real = n_real < kf
    at_neg = jnp.logical_and(few_real, n_ge_neg >= kf)
    below_neg = jnp.logical_and(few_real, n_ge_neg < kf)

    done = jnp.where(at_neg, 1.0, 0.0)
    thr = jnp.where(at_neg, neg, zeros)
    n_gt = jnp.where(at_neg, n_real, zeros)
    tie = jnp.where(jnp.logical_and(at_neg, n_ge_neg > kf), 1.0, 0.0)
    lo = jnp.where(below_neg, gmin, vmin_real)
    hi = jnp.where(below_neg, neg, above_max)
    c_hi = jnp.where(below_neg, n_ge_neg, zeros)

    def bisect(st):
        done, thr, n_gt, tie, lo, hi, c_hi = st
        mid = lo + (hi - lo) * 0.5
        c = count_ge(mid)
        live = done < 0.5
        hit = jnp.logical_and(live, c == kf)
        up = jnp.logical_and(live, c > kf)
        dn = jnp.logical_and(live, c < kf)
        return (jnp.where(hit, 1.0, done), jnp.where(hit, mid, thr), n_gt, tie,
                jnp.where(up, mid, lo), jnp.where(dn, mid, hi), jnp.where(dn, c, c_hi))

    def snap(st):
        done, thr, n_gt, tie, lo, hi, c_hi = st

        def body(j, m):
            for xu in _lane_tiles(sc_ref[j]):
                m = jnp.maximum(m, jnp.where(xu < hi, xu, -inf))
            return m
        below = _rep_max(lax.fori_loop(0, nkb, body, zeros - inf))
        below = jnp.where(jnp.logical_and(tail > 0.5, neg < hi), jnp.maximum(below, neg), below)
        c = count_ge(below)
        live = done < 0.5
        fin = jnp.logical_and(live, c >= kf)
        mv = jnp.logical_and(live, c < kf)
        return (jnp.where(fin, 1.0, done), jnp.where(fin, below, thr), jnp.where(fin, c_hi, n_gt),
                jnp.where(fin, jnp.where(c > kf, 1.0, 0.0), tie),
                lo, jnp.where(mv, below, hi), jnp.where(mv, c, c_hi))

    def not_done(st):
        return _any_true(1.0 - st[0])

    st = (done, thr, n_gt, tie, lo, hi, c_hi)
    st = lax.fori_loop(0, BISECT_BLIND_PASSES, lambda _, s: bisect(s), st)
    _, st = lax.while_loop(
        lambda ps: jnp.logical_and(ps[0] < BISECT_TESTED_ROUNDS, not_done(ps[1])),
        lambda ps: (ps[0] + 1, bisect(bisect(ps[1]))), (jnp.int32(0), st))
    st = lax.while_loop(not_done, lambda s: snap(bisect(s)), st)
    done, thr, n_gt, tie, lo, hi, c_hi = st

    need = kf - n_gt
    n_stored = nkb * w

    def count_eq_upto(jb):
        def body(j, acc):
            for u, xu in enumerate(_lane_tiles(sc_ref[j])):
                idx = j * w + u * LANES + lane
                acc = acc + jnp.where(jnp.logical_and(xu == thr, idx <= jb), 1.0, 0.0)
            return acc
        return _rep_sum(lax.fori_loop(0, nkb, body, zeros))

    def jstep(pj):
        p, (jlo, jhi) = pj
        jmid = jlo + lax.shift_right_arithmetic(jhi - jlo, 1)
        ok = count_eq_upto(jmid) >= need
        return p + 1, (jnp.where(ok, jlo, jmid), jnp.where(ok, jmid, jhi))

    n_jpass = jnp.where(_any_true(tie), 16, 0)
    jlo0 = jnp.full(shape, -1, I32)
    jhi0 = jnp.zeros(shape, I32) + n_stored
    _, (_, jhi) = lax.while_loop(lambda pj: pj[0] < n_jpass, jstep, (jnp.int32(0), (jlo0, jhi0)))
    jmax = jnp.where(tie > 0.5, jhi, jnp.int32(2 ** 30))
    return thr, jmax


def _attn_prompt_kernel(qi_ref, kw_ref, qb_ref, kit_ref, kt_ref, v_ref, o_ref,
                        sc_ref, wb_ref, qc_ref, qh_ref, sa_ref, sb_ref, mxa_ref, mxb_ref, ba_ref, bb_ref,
                        pa_ref, pb_ref,
                        m_ref, acc_ref,
                        *, n_keys, n_sel):
    tq, tk = ATT_TQ, ATT_TK
    tiles = [slice(u * LANES, (u + 1) * LANES) for u in range(tk // LANES)]
    i = pl.program_id(0)
    nkb = (i * tq + tq + tk - 1) // tk
    last_blk = n_keys // tk - 1
    q_pos = i * tq + lax.broadcasted_iota(I32, (tq, LANES), 0)
    lane = lax.broadcasted_iota(I32, (tq, LANES), 1)

    @pl.when(i == 0)
    def _():
        sc_ref[...] = jnp.full(sc_ref.shape, NEG_SCORE, F32)

    kw = kw_ref[...]
    for h in range(H_I):
        hi, lo = _split_bf16(qi_ref[:, h * D_I:(h + 1) * D_I])
        qc_ref[h] = jnp.concatenate([hi, lo, hi], axis=1)
        wb_ref[h] = jnp.broadcast_to(kw[:, D_I + h:D_I + h + 1], (tq, LANES))

    def score_body(jj, _):
        for h in range(H_I):
            wbh = wb_ref[h]
            for j in (2 * jj, 2 * jj + 1):
                d = jnp.dot(qc_ref[h], kit_ref[j], preferred_element_type=F32)
                for u, us in enumerate(tiles):
                    val = wbh * jnp.maximum(d[:, us], 0.0)
                    if h > 0:
                        val = sc_ref[j, :, us] + val
                    if h == H_I - 1:
                        val = jnp.where(j * tk + u * LANES + lane <= q_pos, val, NEG_SCORE)
                    sc_ref[j, :, us] = val
        return 0

    lax.fori_loop(0, (nkb + 1) // 2, score_body, 0)

    tail = jnp.zeros((tq, LANES), I32) + (n_keys - nkb * tk)
    thr, jmax = _select_rows(sc_ref, nkb, tail, n_sel)

    m_ref[...] = jnp.full(m_ref.shape, MASK_VALUE, F32)
    acc_ref[...] = jnp.zeros(acc_ref.shape, F32)
    ones_blk = jnp.ones((tk, LANES), BF16)
    for h in range(H_B):
        qh_ref[h] = (qb_ref[:, h * HD_B:(h + 1) * HD_B] * (HD_B ** -0.5 * LOG2_E)).astype(BF16)
    first_half = lane < HD_B

    def selection_bias(blk, b_ref):
        for u, us in enumerate(tiles):
            xu = sc_ref[blk, :, us]
            idx = blk * tk + u * LANES + lane
            sel = jnp.logical_or(xu > thr, jnp.logical_and(xu == thr, idx <= jmax))
            sel = jnp.logical_and(sel, idx <= q_pos)
            b_ref[:, us] = jnp.where(sel, 0.0, MASK_VALUE)

    def masked_logits(blk, h, s_ref, mx_ref, b_ref):
        s = jnp.dot(qh_ref[h], kt_ref[blk, h * HD_B:(h + 1) * HD_B, :], preferred_element_type=F32) + b_ref[...]
        s_ref[h] = s
        mx = s[:, tiles[0]]
        for us in tiles[1:]:
            mx = jnp.maximum(mx, s[:, us])
        mx_ref[h] = _rep_max(mx)

    def reduce_block(blk, s_ref, mx_ref, nxt, b_ref, p_ref):
        rows = pl.ds(pl.multiple_of(blk * tk, tk), tk)
        for pair in range(H_B // 2):
            ps = slice(pair * LANES, (pair + 1) * LANES)
            v_ext = jnp.concatenate([v_ref[rows, ps], ones_blk], axis=1)
            for half in range(2):
                h = 2 * pair + half
                m_old = m_ref[h]
                m_new = jnp.maximum(m_old, mx_ref[h])
                alpha = jnp.exp2(m_old - m_new)
                acc_ref[h] = jnp.tile(alpha, (1, 2)) * acc_ref[h]
                for us in tiles:
                    p_ref[h, :, us] = jnp.exp2(s_ref[h, :, us] - m_new).astype(BF16)
                m_ref[h] = m_new
                masked_logits(nxt, h, s_ref, mx_ref, b_ref)
                acc_ref[h] += jnp.dot(p_ref[h], v_ext, preferred_element_type=F32)

    selection_bias(0, ba_ref)
    selection_bias(1, bb_ref)
    for h in range(H_B):
        masked_logits(0, h, sa_ref, mxa_ref, ba_ref)
        masked_logits(1, h, sb_ref, mxb_ref, bb_ref)

    def pair_body(jj, _):
        a = 2 * jj
        nxt_a = jnp.minimum(a + 2, last_blk)
        nxt_b = jnp.minimum(a + 3, last_blk)
        selection_bias(nxt_a, ba_ref)
        selection_bias(nxt_b, bb_ref)
        reduce_block(a, sa_ref, mxa_ref, nxt_a, ba_ref, pa_ref)
        reduce_block(a + 1, sb_ref, mxb_ref, nxt_b, bb_ref, pb_ref)
        return 0

    lax.fori_loop(0, (nkb + 1) // 2, pair_body, 0)

    for pair in range(H_B // 2):
        ps = slice(pair * LANES, (pair + 1) * LANES)
        a0, a1 = acc_ref[2 * pair], acc_ref[2 * pair + 1]
        o_ref[:, ps] = jnp.where(first_half, a0[:, :LANES] / a0[:, LANES:], a1[:, :LANES] / a1[:, LANES:])


def _attn_prompt(qi, kw, qb, kit_blocks, kt_blocks, v_bf, n_sel):
    s = qi.shape[0]
    nkb_total = s // ATT_TK
    row = lambda i: (i, 0)
    whole = pl.BlockSpec(memory_space=pltpu.VMEM)
    return pl.pallas_call(
        functools.partial(_attn_prompt_kernel, n_keys=s, n_sel=n_sel),
        grid=(s // ATT_TQ,),
        in_specs=[
            pl.BlockSpec((ATT_TQ, GROUP), row),
            pl.BlockSpec((ATT_TQ, LANES), row),
            pl.BlockSpec((ATT_TQ, GROUP), row),
            whole, whole, whole,
        ],
        out_specs=pl.BlockSpec((ATT_TQ, GROUP), row),
        out_shape=jax.ShapeDtypeStruct((s, GROUP), F32),
        scratch_shapes=[
            pltpu.VMEM((nkb_total, ATT_TQ, ATT_TK), F32),
            pltpu.VMEM((H_I, ATT_TQ, LANES), F32),
            pltpu.VMEM((H_I, ATT_TQ, 3 * D_I), BF16),
            pltpu.VMEM((H_B, ATT_TQ, HD_B), BF16),
            pltpu.VMEM((H_B, ATT_TQ, ATT_TK), F32),
            pltpu.VMEM((H_B, ATT_TQ, ATT_TK), F32),
            pltpu.VMEM((H_B, ATT_TQ, LANES), F32),
            pltpu.VMEM((H_B, ATT_TQ, LANES), F32),
            pltpu.VMEM((ATT_TQ, ATT_TK), F32),
            pltpu.VMEM((ATT_TQ, ATT_TK), F32),
            pltpu.VMEM((H_B, ATT_TQ, ATT_TK), BF16),
            pltpu.VMEM((H_B, ATT_TQ, ATT_TK), BF16),
            pltpu.VMEM((H_B, ATT_TQ, LANES), F32),
            pltpu.VMEM((H_B, ATT_TQ, 2 * LANES), F32),
        ],
        compiler_params=pltpu.CompilerParams(
            dimension_semantics=("arbitrary",), vmem_limit_bytes=VMEM_LIMIT_BYTES),
        name="attn_prompt",
    )(qi, kw, qb, kit_blocks, kt_blocks, v_bf)


def _attn_sample_kernel(pt_ref, qi_ref, kw_ref, qb_ref, kin_ref, kn_ref, vn_ref,
                        ckidx_hbm, ck_hbm, cv_hbm, o_ref,
                        kid_buf, k_buf, v_buf, sc_ref, m_ref, l_ref, acc_ref, sem_i, sem_k, sem_v,
                        *, n_pages, t_new, n_sel):
    b = pl.program_id(0)
    r = SUBLANES
    nrow = H_B * r
    n_chunks = n_pages // KV_CHUNK_PAGES

    def page_lanes(p):
        start = (p % KV_CHUNK_PAGES) * PAGE_SIZE
        return pl.ds(start if isinstance(start, int) else pl.multiple_of(start, PAGE_SIZE), PAGE_SIZE)

    def kidx_copy(p):
        return pltpu.make_async_copy(ckidx_hbm.at[pt_ref[b, p]],
                                     kid_buf.at[p // KV_CHUNK_PAGES, :, page_lanes(p)], sem_i)

    def kv_copies(c, slot, p):
        page = pt_ref[b, c * KV_CHUNK_PAGES + p]
        return (pltpu.make_async_copy(ck_hbm.at[page], k_buf.at[slot, :, :, page_lanes(p)], sem_k.at[slot]),
                pltpu.make_async_copy(cv_hbm.at[page], v_buf.at[slot, :, :, page_lanes(p)], sem_v.at[slot]))

    def start_chunk(c, slot):
        for p in range(KV_CHUNK_PAGES):
            ck, cv = kv_copies(c, slot, p)
            ck.start()
            cv.start()

    def wait_chunk(c, slot):
        for p in range(KV_CHUNK_PAGES):
            ck, cv = kv_copies(c, slot, p)
            ck.wait()
            cv.wait()

    def start_kidx(p, _):
        kidx_copy(p).start()
        return 0

    def wait_kidx(p, _):
        kidx_copy(p).wait()
        return 0

    lax.fori_loop(0, n_pages, start_kidx, 0)
    for c0 in range(KV_SLOTS - 1):
        start_chunk(c0, c0)

    kw = kw_ref[0]
    qi = qi_ref[0]
    q_hi, q_lo, w_rows = [], [], []
    for h in range(H_I):
        hi, lo = _split_bf16(qi[:, h * D_I:(h + 1) * D_I])
        q_hi.append(hi)
        q_lo.append(lo)
        w_rows.append(jnp.broadcast_to(kw[:, D_I + h:D_I + h + 1], (r, LANES)))
    q_hi = jnp.concatenate(q_hi, axis=0)
    q_hl = jnp.concatenate([q_hi, jnp.concatenate(q_lo, axis=0)], axis=0)
    w_rows = jnp.concatenate(w_rows, axis=0)
    nt = (((1,), (1,)), ((), ()))

    def key_scores(keys_t):
        k_hi, k_lo = _split_bf16(keys_t)
        d2 = jnp.dot(q_hl, k_hi, preferred_element_type=F32)
        d = d2[:H_I * r] + d2[H_I * r:] + jnp.dot(q_hi, k_lo, preferred_element_type=F32)
        out = []
        for u in range(keys_t.shape[1] // LANES):
            us = slice(u * LANES, (u + 1) * LANES)
            wd = w_rows * jnp.maximum(d[:, us], 0.0)
            acc = wd[0:r]
            for h in range(1, H_I):
                acc = acc + wd[h * r:(h + 1) * r]
            out.append(acc)
        return out

    lax.fori_loop(0, n_pages, wait_kidx, 0)

    def score_body(c, _):
        for u, tile in enumerate(key_scores(kid_buf[c])):
            sc_ref[c, :, u * LANES:(u + 1) * LANES] = tile
        return 0

    lax.fori_loop(0, n_chunks, score_body, 0)

    lane = lax.broadcasted_iota(I32, (r, LANES), 1)
    t_row = jnp.minimum(lax.broadcasted_iota(I32, (r, LANES), 0), t_new - 1)
    new_sc = key_scores(kin_ref[0])[0]
    new_vis = lane <= t_row
    sc_ref[n_chunks] = jnp.full((r, KV_CHUNK_PAGES * LANES), -jnp.inf, F32)
    sc_ref[n_chunks, :, 0:LANES] = jnp.where(lane >= t_new, -jnp.inf, jnp.where(new_vis, new_sc, NEG_SCORE))

    thr, jmax = _select_rows(sc_ref, n_chunks + 1, jnp.zeros((r, LANES), I32), n_sel)

    qb = qb_ref[0] * (HD_B ** -0.5)
    col_head = lax.shift_right_logical(lax.broadcasted_iota(I32, (r, GROUP), 1), HD_B.bit_length() - 1)
    q_bd = jnp.concatenate([jnp.where(col_head == h, qb, 0.0) for h in range(H_B)], axis=0).astype(BF16)

    m_ref[...] = jnp.full(m_ref.shape, MASK_VALUE, F32)
    l_ref[...] = jnp.zeros(l_ref.shape, F32)
    acc_ref[...] = jnp.zeros(acc_ref.shape, F32)

    def attend(blk, n_blk, k_t, v_t, extra_mask):
        bias = []
        for u in range(n_blk):
            x = sc_ref[blk, :, u * LANES:(u + 1) * LANES]
            idx = (blk * KV_CHUNK_PAGES + u) * LANES + lane
            sel = jnp.logical_or(x > thr, jnp.logical_and(x == thr, idx <= jmax))
            if extra_mask is not None:
                sel = jnp.logical_and(sel, extra_mask)
            bias.append(jnp.where(sel, 0.0, MASK_VALUE))
        bias = jnp.tile(jnp.concatenate(bias, axis=1), (H_B, 1))
        s = jnp.dot(q_bd, k_t.astype(BF16), preferred_element_type=F32) + bias
        m_old = m_ref[...]
        m_new = jnp.maximum(m_old, jnp.broadcast_to(jnp.max(s, axis=1, keepdims=True), m_old.shape))
        alpha = jnp.exp(m_old - m_new)
        p = jnp.exp(s - jnp.tile(m_new, (1, n_blk)))
        p_sum = p[:, :LANES]
        for u in range(1, n_blk):
            p_sum = p_sum + p[:, u * LANES:(u + 1) * LANES]
        l_ref[...] = alpha * l_ref[...] + p_sum
        pv = lax.dot_general(p.astype(BF16), v_t.astype(BF16), nt, preferred_element_type=F32)
        acc_ref[...] = jnp.tile(alpha, (1, GROUP // LANES)) * acc_ref[...] + pv
        m_ref[...] = m_new

    def chunk_body(c, _):
        slot = c % KV_SLOTS
        wait_chunk(c, slot)
        ahead = c + KV_SLOTS - 1

        @pl.when(ahead < n_chunks)
        def _():
            start_chunk(ahead, ahead % KV_SLOTS)

        n_keys_chunk = KV_CHUNK_PAGES * PAGE_SIZE
        attend(c, KV_CHUNK_PAGES, k_buf[slot].reshape(GROUP, n_keys_chunk),
               v_buf[slot].reshape(GROUP, n_keys_chunk), None)
        return 0

    lax.fori_loop(0, n_chunks, chunk_body, 0)
    attend(n_chunks, 1, kn_ref[0].reshape(GROUP, PAGE_SIZE), vn_ref[0].reshape(GROUP, PAGE_SIZE), new_vis)

    out = jnp.zeros((r, GROUP), F32)
    for h in range(H_B):
        rs = slice(h * r, (h + 1) * r)
        denom = jnp.sum(l_ref[rs, :], axis=1, keepdims=True)
        out = out + jnp.where(col_head == h, acc_ref[rs, :] / denom, 0.0)
    o_ref[0] = out


def _attn_sample(page_table, qi, kw, qb, ki_new, k_new, v_new, cache_kidx, cache_k, cache_v, t_new, n_sel):
    nb, n_pages = page_table.shape
    r = SUBLANES
    nrow = H_B * r
    per_b = lambda b, pt: (b, 0, 0)
    per_b4 = lambda b, pt: (b, 0, 0, 0)
    anyspec = pl.BlockSpec(memory_space=pl.ANY)
    chunk_keys = KV_CHUNK_PAGES * PAGE_SIZE
    n_chunks = n_pages // KV_CHUNK_PAGES
    grid_spec = pltpu.PrefetchScalarGridSpec(
        num_scalar_prefetch=1,
        grid=(nb,),
        in_specs=[
            pl.BlockSpec((1, r, GROUP), per_b),
            pl.BlockSpec((1, r, LANES), per_b),
            pl.BlockSpec((1, r, GROUP), per_b),
            pl.BlockSpec((1, D_I, PAGE_SIZE), per_b),
            pl.BlockSpec((1, H_B, HD_B, PAGE_SIZE), per_b4),
            pl.BlockSpec((1, H_B, HD_B, PAGE_SIZE), per_b4),
            anyspec, anyspec, anyspec,
        ],
        out_specs=pl.BlockSpec((1, r, GROUP), per_b),
        scratch_shapes=[
            pltpu.VMEM((n_chunks, D_I, chunk_keys), F32),
            pltpu.VMEM((KV_SLOTS, H_B, HD_B, chunk_keys), F32),
            pltpu.VMEM((KV_SLOTS, H_B, HD_B, chunk_keys), F32),
            pltpu.VMEM((n_chunks + 1, r, chunk_keys), F32),
            pltpu.VMEM((nrow, LANES), F32),
            pltpu.VMEM((nrow, LANES), F32),
            pltpu.VMEM((nrow, GROUP), F32),
            pltpu.SemaphoreType.DMA(()),
            pltpu.SemaphoreType.DMA((KV_SLOTS,)),
            pltpu.SemaphoreType.DMA((KV_SLOTS,)),
        ],
    )
    return pl.pallas_call(
        functools.partial(_attn_sample_kernel, n_pages=n_pages, t_new=t_new, n_sel=n_sel),
        grid_spec=grid_spec,
        out_shape=jax.ShapeDtypeStruct((nb, r, GROUP), F32),
        compiler_params=pltpu.CompilerParams(
            dimension_semantics=("arbitrary",), vmem_limit_bytes=VMEM_LIMIT_BYTES),
        name="attn_sample",
    )(page_table, qi, kw, qb, ki_new, k_new, v_new, cache_kidx, cache_k, cache_v)


def _out_ffn_kernel(x_ref, oa_ref, ob_ref, p1_ref, p2_ref, wo_ref, g2_ref, wu_ref, cw_ref, cb_ref, wd_ref, gf_ref,
                    y_ref, u_ref, carry_ref, x1_ref, h2_ref, *, d_ff, ff_tile, seq_rows, carry_mode):
    tm = x_ref.shape[0]
    i = pl.program_id(0)
    mixed = jnp.concatenate([oa_ref[...], ob_ref[...]], axis=1).astype(BF16)
    x1 = x_ref[...] + jnp.dot(mixed, wo_ref[...], preferred_element_type=F32)
    h2_ref[...] = _rms(x1, g2_ref[...]).astype(BF16)
    x1_ref[...] = x1

    row = lax.broadcasted_iota(I32, (tm, ff_tile), 0)
    if carry_mode:
        @pl.when(i == 0)
        def _():
            carry_ref[...] = jnp.zeros(carry_ref.shape, F32)
        t_in_seq = row
    else:
        t_in_seq = row & (seq_rows - 1)

    for c in range(d_ff // ff_tile):
        cs = slice(c * ff_tile, (c + 1) * ff_tile)
        h2 = h2_ref[...]
        u = jnp.dot(h2, wu_ref[:, cs], preferred_element_type=F32)
        v = jnp.dot(h2, wu_ref[:, d_ff + c * ff_tile:d_ff + (c + 1) * ff_tile], preferred_element_type=F32)
        if carry_mode:
            prev = carry_ref[:, cs]
            p1 = jnp.broadcast_to(prev[SUBLANES - 1:SUBLANES], (tm, ff_tile))
            p2 = jnp.where(row == 0, jnp.broadcast_to(prev[SUBLANES - 2:SUBLANES - 1], (tm, ff_tile)), p1)
            carry_ref[:, cs] = u[tm - SUBLANES:]
            u_ref[:, cs] = u[tm - SUBLANES:]
        else:
            p1 = p1_ref[:, cs]
            p2 = p2_ref[:, cs]
            u_ref[:, cs] = u
        u1 = jnp.where(t_in_seq >= 1, pltpu.roll(u, 1, 0), p1)
        u2 = jnp.where(t_in_seq >= 2, pltpu.roll(u, 2, 0), p2)
        conv = cb_ref[:, cs] + cw_ref[0:1, cs] * u2 + cw_ref[1:2, cs] * u1 + cw_ref[2:3, cs] * u
        gate = (_silu(conv) * v).astype(BF16)
        x1_ref[...] += jnp.dot(gate, wd_ref[cs, :], preferred_element_type=F32)

    y_ref[...] = _rms(x1_ref[...], gf_ref[...])


def _out_ffn(x2d, oa, ob, p1, p2, w_out, g2, w_up, conv_w, conv_b, w_down, gf, tm, seq_rows, carry_mode):
    m, d = x2d.shape
    d_ff = w_down.shape[0]
    ff_tile = 256
    row = lambda i: (i, 0)
    const = lambda i: (0, 0)
    if carry_mode:
        prev_spec = pl.BlockSpec((SUBLANES, d_ff), const)
        u_spec = pl.BlockSpec((SUBLANES, d_ff), const)
        u_shape = jax.ShapeDtypeStruct((SUBLANES, d_ff), F32)
    else:
        prev_spec = pl.BlockSpec((tm, d_ff), row)
        u_spec = pl.BlockSpec((tm, d_ff), row)
        u_shape = jax.ShapeDtypeStruct((m, d_ff), F32)
    return pl.pallas_call(
        functools.partial(_out_ffn_kernel, d_ff=d_ff, ff_tile=ff_tile, seq_rows=seq_rows, carry_mode=carry_mode),
        grid=(m // tm,),
        in_specs=[
            pl.BlockSpec((tm, d), row),
            pl.BlockSpec((tm, GROUP), row),
            pl.BlockSpec((tm, GROUP), row),
            prev_spec, prev_spec,
            pl.BlockSpec(w_out.shape, const, pipeline_mode=pl.Buffered(1)),
            pl.BlockSpec((1, d), const),
            pl.BlockSpec(w_up.shape, const, pipeline_mode=pl.Buffered(1)),
            pl.BlockSpec(conv_w.shape, const),
            pl.BlockSpec((1, d_ff), const),
            pl.BlockSpec(w_down.shape, const, pipeline_mode=pl.Buffered(1)),
            pl.BlockSpec((1, d), const),
        ],
        out_specs=[pl.BlockSpec((tm, d), row), u_spec],
        out_shape=[jax.ShapeDtypeStruct((m, d), F32), u_shape],
        scratch_shapes=[pltpu.VMEM((SUBLANES, d_ff), F32), pltpu.VMEM((tm, d), F32), pltpu.VMEM((tm, d), BF16)],
        compiler_params=pltpu.CompilerParams(
            dimension_semantics=("arbitrary",), vmem_limit_bytes=VMEM_LIMIT_BYTES),
        name="out_ffn",
    )(x2d, oa, ob, p1, p2, w_out, g2, w_up, conv_w, conv_b, w_down, gf)


def _rope_tables(pos):
    half = ROT // 2
    inv = jnp.power(ROPE_THETA, -jnp.arange(half, dtype=F32) * (2.0 / ROT))
    ang = pos.astype(F32)[:, None] * inv[None, :]
    cos, sin = jnp.cos(ang), jnp.sin(ang)
    n = pos.shape[0]
    ones = jnp.ones((n, HD_B - ROT), F32)
    zeros = jnp.zeros((n, HD_B - ROT), F32)
    zh = jnp.zeros((n, half), F32)
    c = jnp.concatenate([cos, cos, ones], axis=1)
    sa = jnp.concatenate([zh, sin, zeros], axis=1)
    sb = jnp.concatenate([-sin, zh, zeros], axis=1)
    rep = LANES // HD_B
    return jnp.tile(c, (1, rep)), jnp.tile(sa, (1, rep)), jnp.tile(sb, (1, rep))


def kernel(x_prompt, x_sample, cache_k, cache_v, cache_kidx, state_hgrn, state_conv, page_table,
           norm_mix_gain, w_in, lb_logits, hgrn_norm_gain, w_out, norm_ffn_gain, w_up, conv_w,
           conv_b, w_down, final_norm_gain):
    bp, s, d = x_prompt.shape
    nb, t_new, _ = x_sample.shape
    depth = w_in.shape[0]
    assert bp == 1 and depth == 1
    n_pages = page_table.shape[1]
    past = n_pages * PAGE_SIZE
    d_ff = w_down.shape[1]
    assert s % (2 * ATT_TK) == 0 and s % HGRN_CHUNK == 0
    assert n_pages % KV_CHUNK_PAGES == 0 and n_pages // KV_CHUNK_PAGES >= KV_SLOTS
    assert t_new <= SUBLANES and CONV_W - 1 <= t_new

    lower_bounds = jnp.cumsum(jax.nn.softmax(lb_logits.astype(F32), axis=0), axis=0)
    lb = lower_bounds[0][None, :]
    n_in = w_in.shape[2]
    n_pad = 8 * GROUP + LANES
    w_in_bf = jnp.pad(w_in[0], ((0, 0), (0, n_pad - n_in))).astype(BF16)
    w_out_bf = w_out[0].astype(BF16)
    w_up_bf = w_up[0].astype(BF16)
    w_down_bf = w_down[0].astype(BF16)
    g_mix = norm_mix_gain[0][None, :]
    g_ffn = norm_ffn_gain[0][None, :]
    g_fin = final_norm_gain[None, :]
    g_hgrn = hgrn_norm_gain[0][None, :]
    conv_b2 = conv_b[0][None, :]

    xp = x_prompt.reshape(s, d)
    cp, sap, sbp = _rope_tables(jnp.arange(s, dtype=I32))
    qa, ka, lf, ia, ga, qb, kb, vb, qi, kw = _proj_in(xp, g_mix, w_in_bf, lb, cp, sap, sbp, tm=PROJ_ROWS)

    s0 = jnp.zeros((1, H_A, DK_A, DK_A), F32)
    r3 = lambda a: a.reshape(1, s, GROUP)
    oa_p, st_p = _hgrn(r3(qa), r3(ka), r3(lf), r3(ia), r3(ga), g_hgrn, s0, HGRN_CHUNK, HGRN_SUB)

    nkb = s // ATT_TK
    ki_hi, ki_lo = _split_bf16(kw[:, :D_I])
    kit = jnp.concatenate([ki_hi, ki_hi, ki_lo], axis=1)
    kit_blocks = kit.reshape(nkb, ATT_TK, 3 * D_I).transpose(0, 2, 1)
    kt_blocks = kb.astype(BF16).reshape(nkb, ATT_TK, GROUP).transpose(0, 2, 1)
    ob_p = _attn_prompt(qi, kw, qb, kit_blocks, kt_blocks, vb.astype(BF16), min(TOPK_MAX, s // 4))

    zero_prev = jnp.zeros((SUBLANES, d_ff), F32)
    y_p, u_tail = _out_ffn(xp, oa_p.reshape(s, GROUP), ob_p, zero_prev, zero_prev, w_out_bf, g_ffn, w_up_bf,
                           conv_w[0], conv_b2, w_down_bf, g_fin, tm=PROJ_ROWS, seq_rows=s, carry_mode=True)

    ms = nb * t_new
    xs = x_sample.reshape(ms, d)
    pos_s = jnp.tile(past + jnp.arange(t_new, dtype=I32), nb)
    cs, sas, sbs = _rope_tables(pos_s)
    qa2, ka2, lf2, ia2, ga2, qb2, kb2, vb2, qi2, kw2 = _proj_in(xs, g_mix, w_in_bf, lb, cs, sas, sbs, tm=ms)

    pad_front = SUBLANES - t_new
    fp = lambda a: jnp.pad(a.reshape(nb, t_new, GROUP), ((0, 0), (pad_front, 0), (0, 0)))
    s0_s = jnp.swapaxes(state_hgrn[0], -1, -2)
    oa_s, st_s = _hgrn(fp(qa2), fp(ka2), fp(lf2), fp(ia2), fp(ga2), g_hgrn, s0_s, SUBLANES, SUBLANES)
    oa_s = oa_s[:, pad_front:, :].reshape(ms, GROUP)

    def rp(a):
        a = a.reshape(nb, t_new, a.shape[-1])
        return jnp.concatenate([a, jnp.broadcast_to(a[:, -1:], (nb, SUBLANES - t_new, a.shape[-1]))], axis=1)
    page_pad = lambda a: jnp.pad(a.reshape(nb, t_new, a.shape[-1]), ((0, 0), (0, PAGE_SIZE - t_new), (0, 0)))
    new_page = lambda a: jnp.swapaxes(page_pad(a), 1, 2)
    heads = lambda a: new_page(a).reshape(nb, H_B, HD_B, PAGE_SIZE)
    ob_s = _attn_sample(
        page_table, rp(qi2), rp(kw2), rp(qb2), new_page(kw2[:, :D_I]), heads(kb2), heads(vb2),
        jnp.transpose(cache_kidx[0], (0, 2, 1)), jnp.transpose(cache_k[0], (0, 2, 3, 1)),
        jnp.transpose(cache_v[0], (0, 2, 3, 1)),
        t_new, min(TOPK_MAX, (past + t_new) // 4))
    ob_s = ob_s[:, :t_new, :].reshape(ms, GROUP)

    sc0 = state_conv[0]
    zrow = jnp.zeros((nb, t_new - 1, d_ff), F32)
    p1 = jnp.concatenate([sc0[:, 1:2], zrow], axis=1).reshape(ms, d_ff)
    p2 = jnp.concatenate([sc0[:, 0:2], zrow[:, 1:]], axis=1).reshape(ms, d_ff)
    y_s, u_s = _out_ffn(xs, oa_s, ob_s, p1, p2, w_out_bf, g_ffn, w_up_bf, conv_w[0], conv_b2, w_down_bf, g_fin,
                        tm=ms, seq_rows=t_new, carry_mode=False)

    return (
        y_p.reshape(1, s, d),
        y_s.reshape(nb, t_new, d),
        kb.reshape(1, 1, s, H_B, HD_B),
        vb.reshape(1, 1, s, H_B, HD_B),
        kw[:, :D_I].reshape(1, 1, s, D_I),
        jnp.swapaxes(st_p, -1, -2).reshape(1, 1, H_A, DK_A, DK_A),
        u_tail[SUBLANES - (CONV_W - 1):].reshape(1, 1, CONV_W - 1, d_ff),
        kb2.reshape(1, nb, t_new, H_B, HD_B),
        vb2.reshape(1, nb, t_new, H_B, HD_B),
        kw2[:, :D_I].reshape(1, nb, t_new, D_I),
        jnp.swapaxes(st_s, -1, -2).reshape(1, nb, H_A, DK_A, DK_A),
        u_s.reshape(nb, t_new, d_ff)[:, t_new - (CONV_W - 1):].reshape(1, nb, CONV_W - 1, d_ff),
    )
```

```python
import functools

import jax
import jax.numpy as jnp
from jax import lax
from jax.experimental import pallas as pl
from jax.experimental.pallas import tpu as pltpu

F32 = jnp.float32
BF16 = jnp.bfloat16
I32 = jnp.int32

H_A = 4
DK_A = 128
H_B = 8
HD_B = 64
H_I = 8
D_I = 64
ROT = 16
ROPE_THETA = 500000.0
TOPK_MAX = 256
PAGE_SIZE = 128
CONV_W = 3
EPS = 1e-6
NEG_SCORE = -1e30
MASK_VALUE = -1e30
GROUP = 512
LOG2_E = 1.4426950408889634

LANES = 128
SUBLANES = 8
VMEM_LIMIT_BYTES = 61 * 1024 * 1024

PROJ_ROWS = 512
HGRN_CHUNK = 64
HGRN_SUB = 16
BISECT_BLIND_PASSES = 10
BISECT_TESTED_ROUNDS = 5
ATT_TQ = 128
ATT_TK = 512
KV_CHUNK_PAGES = 8
KV_SLOTS = 4


def _silu(x):
    return x * jax.nn.sigmoid(x)


def _rms(x, g):
    return x * lax.rsqrt(jnp.mean(x * x, axis=-1, keepdims=True) + EPS) * g


def _split_bf16(x):
    hi = x.astype(BF16)
    lo = (x - hi.astype(F32)).astype(BF16)
    return hi, lo


def _rope(x, c, sa, sb):
    w = x.shape[1]
    return x * c + pltpu.roll(x, ROT // 2, 1) * sa + pltpu.roll(x, w - ROT // 2, 1) * sb


def _proj_in_kernel(x_ref, g_ref, w_ref, lb_ref, c_ref, sa_ref, sb_ref,
                    qa_ref, ka_ref, lf_ref, ia_ref, ga_ref, qb_ref, kb_ref, vb_ref, qi_ref, kw_ref, h_ref):
    h_ref[...] = _rms(x_ref[...], g_ref[...]).astype(BF16)

    def grp(i, width=GROUP):
        return jnp.dot(h_ref[...], w_ref[:, i * GROUP:i * GROUP + width], preferred_element_type=F32)

    c = jnp.tile(c_ref[...], (1, GROUP // LANES))
    sa = jnp.tile(sa_ref[...], (1, GROUP // LANES))
    sb = jnp.tile(sb_ref[...], (1, GROUP // LANES))

    qa_ref[...] = _silu(grp(0))
    lb = lb_ref[...]
    fg = lb + (1.0 - lb) * jax.nn.sigmoid(grp(1))
    ka_ref[...] = 1.0 - fg
    lf_ref[...] = jnp.log(fg)
    ia_ref[...] = grp(2)
    ga_ref[...] = _silu(grp(3))
    qb_ref[...] = _rope(grp(4), c, sa, sb)
    kb_ref[...] = _rope(grp(5), c, sa, sb)
    vb_ref[...] = grp(6)
    qi_ref[...] = _rope(grp(7), c, sa, sb)
    kw = grp(8, LANES)
    lane = lax.broadcasted_iota(I32, kw.shape, 1)
    roped = _rope(kw, c_ref[...], sa_ref[...], sb_ref[...])
    kw_ref[...] = jnp.where(lane < D_I, roped, kw * (H_I ** -0.5 * D_I ** -0.5))


def _proj_in(x2d, gain, w_pad, lb, cos_t, sin_a, sin_b, tm):
    m, d = x2d.shape
    n_pad = w_pad.shape[1]
    row = lambda i: (i, 0)
    const = lambda i: (0, 0)
    big = pl.BlockSpec((tm, GROUP), row)
    out_shapes = [jax.ShapeDtypeStruct((m, GROUP), F32)] * 9 + [jax.ShapeDtypeStruct((m, LANES), F32)]
    return pl.pallas_call(
        _proj_in_kernel,
        grid=(m // tm,),
        in_specs=[
            pl.BlockSpec((tm, d), row),
            pl.BlockSpec((1, d), const),
            pl.BlockSpec((d, n_pad), const, pipeline_mode=pl.Buffered(1)),
            pl.BlockSpec((1, GROUP), const),
            pl.BlockSpec((tm, LANES), row),
            pl.BlockSpec((tm, LANES), row),
            pl.BlockSpec((tm, LANES), row),
        ],
        out_specs=[big] * 9 + [pl.BlockSpec((tm, LANES), row)],
        out_shape=out_shapes,
        scratch_shapes=[pltpu.VMEM((tm, d), BF16)],
        compiler_params=pltpu.CompilerParams(
            dimension_semantics=("arbitrary",), vmem_limit_bytes=VMEM_LIMIT_BYTES),
        name="proj_in",
    )(x2d, gain, w_pad, lb, cos_t, sin_a, sin_b)


def _shift_rows(x, d):
    if d == 0:
        return x
    return pltpu.roll(x, d, 0)


def _cumsum_rows(g):
    c = g.shape[0]
    row = lax.broadcasted_iota(I32, g.shape, 0)
    k = 1
    while k < c:
        g = g + jnp.where(row >= k, _shift_rows(g, k), 0.0)
        k *= 2
    return g


def _hgrn_kernel(q_ref, k_ref, lf_ref, v_ref, ga_ref, gain_ref, s0_ref, o_ref, sfin_ref, st_ref, *, chunk, sub):
    ci = pl.program_id(1)

    @pl.when(ci == 0)
    def _():
        st_ref[...] = s0_ref[0]

    nsb = chunk // sub
    row = lax.broadcasted_iota(I32, (chunk, DK_A), 0)
    row_in_sub = row & (sub - 1)
    gain = gain_ref[...]

    for h in range(H_A):
        sl = slice(h * DK_A, (h + 1) * DK_A)
        q = q_ref[0, :, sl]
        k = k_ref[0, :, sl]
        v = v_ref[0, :, sl]
        b = _cumsum_rows(lf_ref[0, :, sl])
        st = st_ref[h]

        refs = [jnp.zeros((1, DK_A), F32)] + [b[i * sub - 1:i * sub, :] for i in range(1, nsb)]
        ref_rows = jnp.concatenate([jnp.broadcast_to(r, (sub, DK_A)) for r in refs], axis=0)
        q_rel = q * jnp.exp(b - ref_rows)

        o = lax.dot_general((q * jnp.exp(b)).astype(BF16), st.astype(BF16),
                            (((1,), (1,)), ((), ())), preferred_element_type=F32)

        off_rows = [jnp.zeros((sub, DK_A), F32)]
        for i in range(1, nsb):
            n_prev = i * sub
            k_rel = (k[:n_prev] * jnp.exp(refs[i] - b[:n_prev])).astype(BF16)
            att = lax.dot_general(q_rel[n_prev:n_prev + sub].astype(BF16), k_rel,
                                  (((1,), (1,)), ((), ())), preferred_element_type=F32)
            off_rows.append(jnp.dot(att.astype(BF16), v[:n_prev].astype(BF16), preferred_element_type=F32))
        if nsb > 1:
            o = o + jnp.concatenate(off_rows, axis=0)

        for d in range(sub):
            valid = row_in_sub >= d
            e = jnp.exp(jnp.where(valid, b - _shift_rows(b, d), 0.0))
            w = jnp.sum(q * _shift_rows(k, d) * e, axis=1, keepdims=True)
            o = o + jnp.where(valid, w * _shift_rows(v, d), 0.0)

        b_last = b[chunk - 1:chunk, :]
        k_dec = (k * jnp.exp(b_last - b)).astype(BF16)
        st_ref[h] = st * jnp.exp(b_last) + lax.dot_general(
            v.astype(BF16), k_dec, (((0,), (0,)), ((), ())), preferred_element_type=F32)

        o_ref[0, :, sl] = _rms(o, gain) * ga_ref[0, :, sl]

    @pl.when(ci == pl.num_programs(1) - 1)
    def _():
        sfin_ref[0] = st_ref[...]


def _hgrn(q, k, lf, v, ga, gain, s0_t, chunk, sub):
    b, t, w = q.shape
    blk = pl.BlockSpec((1, chunk, w), lambda bi, ci: (bi, ci, 0))
    st_spec = pl.BlockSpec((1, H_A, DK_A, DK_A), lambda bi, ci: (bi, 0, 0, 0))
    return pl.pallas_call(
        functools.partial(_hgrn_kernel, chunk=chunk, sub=sub),
        grid=(b, t // chunk),
        in_specs=[blk, blk, blk, blk, blk, pl.BlockSpec((1, DK_A), lambda bi, ci: (0, 0)), st_spec],
        out_specs=[blk, st_spec],
        out_shape=[jax.ShapeDtypeStruct((b, t, w), F32), jax.ShapeDtypeStruct((b, H_A, DK_A, DK_A), F32)],
        scratch_shapes=[pltpu.VMEM((H_A, DK_A, DK_A), F32)],
        compiler_params=pltpu.CompilerParams(
            dimension_semantics=("arbitrary", "arbitrary"), vmem_limit_bytes=VMEM_LIMIT_BYTES),
        name="hgrn",
    )(q, k, lf, v, ga, gain, s0_t)


def _rep_sum(x):
    return jnp.broadcast_to(jnp.sum(x, axis=1, keepdims=True), x.shape)


def _rep_max(x):
    return jnp.broadcast_to(jnp.max(x, axis=1, keepdims=True), x.shape)


def _rep_min(x):
    return jnp.broadcast_to(jnp.min(x, axis=1, keepdims=True), x.shape)


def _any_true(flag):
    return jnp.max(flag) > 0.5


def _lane_tiles(x):
    return [x[:, u * LANES:(u + 1) * LANES] for u in range(x.shape[1] // LANES)]


def _select_rows(sc_ref, nkb, n_hidden_tail, n_sel):
    _, r, w = sc_ref.shape
    shape = (r, LANES)
    neg = jnp.float32(NEG_SCORE)
    inf = jnp.float32(jnp.inf)
    kf = jnp.float32(n_sel)
    tail = n_hidden_tail.astype(F32)
    zeros = jnp.zeros(shape, F32)
    lane = lax.broadcasted_iota(I32, shape, 1)

    def count_ge(c):
        def body(j, acc):
            for xu in _lane_tiles(sc_ref[j]):
                acc = acc + jnp.where(xu >= c, 1.0, 0.0)
            return acc
        if isinstance(nkb, int):
            acc = functools.reduce(lambda a, b: a + b, [body(j, zeros) for j in range(nkb)])
        else:
            acc = lax.fori_loop(0, nkb, body, zeros)
        return _rep_sum(acc) + jnp.where(neg >= c, tail, 0.0)

    assert n_sel <= 2 * LANES

    def stats_body(j, carry):
        top1, top2, n_real = carry
        for x in _lane_tiles(sc_ref[j]):
            n_real = n_real + jnp.where(x > neg, 1.0, 0.0)
            top2 = jnp.maximum(top2, jnp.minimum(top1, x))
            top1 = jnp.maximum(top1, x)
        return top1, top2, n_real

    def rare_stats_body(j, carry):
        vmin_real, n_ge_neg, gmin = carry
        for x in _lane_tiles(sc_ref[j]):
            vmin_real = jnp.minimum(vmin_real, jnp.where(x > neg, x, inf))
            n_ge_neg = n_ge_neg + jnp.where(x >= neg, 1.0, 0.0)
            gmin = jnp.minimum(gmin, jnp.where(x > -inf, x, inf))
        return vmin_real, n_ge_neg, gmin

    top1, top2, n_real = lax.fori_loop(0, nkb, stats_body, (zeros - inf, zeros - inf, zeros))
    has_tail = tail > 0.5
    vmax = _rep_max(top1)
    vmax = jnp.where(has_tail, jnp.maximum(vmax, neg), vmax)
    n_real = _rep_sum(n_real)
    lo_lanes = _rep_min(top2)
    rare = _any_true(jnp.where(jnp.logical_or(n_real < kf, lo_lanes <= neg), 1.0, 0.0))
    vmin_real, n_ge_neg, gmin = lax.fori_loop(0, jnp.where(rare, nkb, 0), rare_stats_body,
                                              (zeros + inf, zeros, zeros + inf))
    vmin_real = jnp.where(lo_lanes > neg, lo_lanes, _rep_min(vmin_real))
    gmin = _rep_min(gmin)
    gmin = jnp.where(has_tail, jnp.minimum(gmin, neg), gmin)
    n_ge_neg = _rep_sum(n_ge_neg) + tail
    above_max = vmax + jnp.maximum(jnp.abs(vmax), 1e-30) * 1e-6

    few_real = n_real < kf
    at_neg = jnp.logical_and(few_real, n_ge_neg >= kf)
    below_neg = jnp.logical_and(few_real, n_ge_neg < kf)

    done = jnp.where(at_neg, 1.0, 0.0)
    thr = jnp.where(at_neg, neg, zeros)
    n_gt = jnp.where(at_neg, n_real, zeros)
    tie = jnp.where(jnp.logical_and(at_neg, n_ge_neg > kf), 1.0, 0.0)
    lo = jnp.where(below_neg, gmin, vmin_real)
    hi = jnp.where(below_neg, neg, above_max)
    c_hi = jnp.where(below_neg, n_ge_neg, zeros)

    def bisect(st):
        done, thr, n_gt, tie, lo, hi, c_hi = st
        mid = lo + (hi - lo) * 0.5
        c = count_ge(mid)
        live = done < 0.5
        hit = jnp.logical_and(live, c == kf)
        up = jnp.logical_and(live, c > kf)
        dn = jnp.logical_and(live, c < kf)
        return (jnp.where(hit, 1.0, done), jnp.where(hit, mid, thr), n_gt, tie,
                jnp.where(up, mid, lo), jnp.where(dn, mid, hi), jnp.where(dn, c, c_hi))

    def snap(st):
        done, thr, n_gt, tie, lo, hi, c_hi = st

        def body(j, m):
            for xu in _lane_tiles(sc_ref[j]):
                m = jnp.maximum(m, jnp.where(xu < hi, xu, -inf))
            return m
        below = _rep_max(lax.fori_loop(0, nkb, body, zeros - inf))
        below = jnp.where(jnp.logical_and(tail > 0.5, neg < hi), jnp.maximum(below, neg), below)
        c = count_ge(below)
        live = done < 0.5
        fin = jnp.logical_and(live, c >= kf)
        mv = jnp.logical_and(live, c < kf)
        return (jnp.where(fin, 1.0, done), jnp.where(fin, below, thr), jnp.where(fin, c_hi, n_gt),
                jnp.where(fin, jnp.where(c > kf, 1.0, 0.0), tie),
                lo, jnp.where(mv, below, hi), jnp.where(mv, c, c_hi))

    def not_done(st):
        return _any_true(1.0 - st[0])

    st = (done, thr, n_gt, tie, lo, hi, c_hi)
    st = lax.fori_loop(0, BISECT_BLIND_PASSES, lambda _, s: bisect(s), st)
    _, st = lax.while_loop(
        lambda ps: jnp.logical_and(ps[0] < BISECT_TESTED_ROUNDS, not_done(ps[1])),
        lambda ps: (ps[0] + 1, bisect(bisect(ps[1]))), (jnp.int32(0), st))
    st = lax.while_loop(not_done, lambda s: snap(bisect(s)), st)
    done, thr, n_gt, tie, lo, hi, c_hi = st

    need = kf - n_gt
    n_stored = nkb * w

    def count_eq_upto(jb):
        def body(j, acc):
            for u, xu in enumerate(_lane_tiles(sc_ref[j])):
                idx = j * w + u * LANES + lane
                acc = acc + jnp.where(jnp.logical_and(xu == thr, idx <= jb), 1.0, 0.0)
            return acc
        return _rep_sum(lax.fori_loop(0, nkb, body, zeros))

    def jstep(pj):
        p, (jlo, jhi) = pj
        jmid = jlo + lax.shift_right_arithmetic(jhi - jlo, 1)
        ok = count_eq_upto(jmid) >= need
        return p + 1, (jnp.where(ok, jlo, jmid), jnp.where(ok, jmid, jhi))

    n_jpass = jnp.where(_any_true(tie), 16, 0)
    jlo0 = jnp.full(shape, -1, I32)
    jhi0 = jnp.zeros(shape, I32) + n_stored
    _, (_, jhi) = lax.while_loop(lambda pj: pj[0] < n_jpass, jstep, (jnp.int32(0), (jlo0, jhi0)))
    jmax = jnp.where(tie > 0.5, jhi, jnp.int32(2 ** 30))
    return thr, jmax


def _attn_prompt_kernel(qi_ref, kw_ref, qb_ref, kit_ref, kt_ref, v_ref, o_ref,
                        sc_ref, wb_ref, qc_ref, qh_ref, sa_ref, sb_ref, mxa_ref, mxb_ref, ba_ref, bb_ref,
                        pa_ref, pb_ref,
                        m_ref, acc_ref,
                        *, n_keys, n_sel):
    tq, tk = ATT_TQ, ATT_TK
    tiles = [slice(u * LANES, (u + 1) * LANES) for u in range(tk // LANES)]
    i = pl.program_id(0)
    nkb = (i * tq + tq + tk - 1) // tk
    last_blk = n_keys // tk - 1
    q_pos = i * tq + lax.broadcasted_iota(I32, (tq, LANES), 0)
    lane = lax.broadcasted_iota(I32, (tq, LANES), 1)

    @pl.when(i == 0)
    def _():
        sc_ref[...] = jnp.full(sc_ref.shape, NEG_SCORE, F32)

    kw = kw_ref[...]
    for h in range(H_I):
        hi, lo = _split_bf16(qi_ref[:, h * D_I:(h + 1) * D_I])
        qc_ref[h] = jnp.concatenate([hi, lo, hi], axis=1)
        wb_ref[h] = jnp.broadcast_to(kw[:, D_I + h:D_I + h + 1], (tq, LANES))

    def score_body(jj, _):
        for h in range(H_I):
            wbh = wb_ref[h]
            for j in (2 * jj, 2 * jj + 1):
                d = jnp.dot(qc_ref[h], kit_ref[j], preferred_element_type=F32)
                for u, us in enumerate(tiles):
                    val = wbh * jnp.maximum(d[:, us], 0.0)
                    if h > 0:
                        val = sc_ref[j, :, us] + val
                    if h == H_I - 1:
                        val = jnp.where(j * tk + u * LANES + lane <= q_pos, val, NEG_SCORE)
                    sc_ref[j, :, us] = val
        return 0

    lax.fori_loop(0, (nkb + 1) // 2, score_body, 0)

    tail = jnp.zeros((tq, LANES), I32) + (n_keys - nkb * tk)
    thr, jmax = _select_rows(sc_ref, nkb, tail, n_sel)

    m_ref[...] = jnp.full(m_ref.shape, MASK_VALUE, F32)
    acc_ref[...] = jnp.zeros(acc_ref.shape, F32)
    ones_blk = jnp.ones((tk, LANES), BF16)
    for h in range(H_B):
        qh_ref[h] = (qb_ref[:, h * HD_B:(h + 1) * HD_B] * (HD_B ** -0.5 * LOG2_E)).astype(BF16)
    first_half = lane < HD_B

    def selection_bias(blk, b_ref):
        for u, us in enumerate(tiles):
            xu = sc_ref[blk, :, us]
            idx = blk * tk + u * LANES + lane
            sel = jnp.logical_or(xu > thr, jnp.logical_and(xu == thr, idx <= jmax))
            sel = jnp.logical_and(sel, idx <= q_pos)
            b_ref[:, us] = jnp.where(sel, 0.0, MASK_VALUE)

    def masked_logits(blk, h, s_ref, mx_ref, b_ref):
        s = jnp.dot(qh_ref[h], kt_ref[blk, h * HD_B:(h + 1) * HD_B, :], preferred_element_type=F32) + b_ref[...]
        s_ref[h] = s
        mx = s[:, tiles[0]]
        for us in tiles[1:]:
            mx = jnp.maximum(mx, s[:, us])
        mx_ref[h] = _rep_max(mx)

    def reduce_block(blk, s_ref, mx_ref, nxt, b_ref, p_ref):
        rows = pl.ds(pl.multiple_of(blk * tk, tk), tk)
        for pair in range(H_B // 2):
            ps = slice(pair * LANES, (pair + 1) * LANES)
            v_ext = jnp.concatenate([v_ref[rows, ps], ones_blk], axis=1)
            for half in range(2):
                h = 2 * pair + half
                m_old = m_ref[h]
                m_new = jnp.maximum(m_old, mx_ref[h])
                alpha = jnp.exp2(m_old - m_new)
                acc_ref[h] = jnp.tile(alpha, (1, 2)) * acc_ref[h]
                for us in tiles:
                    p_ref[h, :, us] = jnp.exp2(s_ref[h, :, us] - m_new).astype(BF16)
                m_ref[h] = m_new
                masked_logits(nxt, h, s_ref, mx_ref, b_ref)
                acc_ref[h] += jnp.dot(p_ref[h], v_ext, preferred_element_type=F32)

    selection_bias(0, ba_ref)
    selection_bias(1, bb_ref)
    for h in range(H_B):
        masked_logits(0, h, sa_ref, mxa_ref, ba_ref)
        masked_logits(1, h, sb_ref, mxb_ref, bb_ref)

    def pair_body(jj, _):
        a = 2 * jj
        nxt_a = jnp.minimum(a + 2, last_blk)
        nxt_b = jnp.minimum(a + 3, last_blk)
        selection_bias(nxt_a, ba_ref)
        selection_bias(nxt_b, bb_ref)
        reduce_block(a, sa_ref, mxa_ref, nxt_a, ba_ref, pa_ref)
        reduce_block(a + 1, sb_ref, mxb_ref, nxt_b, bb_ref, pb_ref)
        return 0

    lax.fori_loop(0, (nkb + 1) // 2, pair_body, 0)

    for pair in range(H_B // 2):
        ps = slice(pair * LANES, (pair + 1) * LANES)
        a0, a1 = acc_ref[2 * pair], acc_ref[2 * pair + 1]
        o_ref[:, ps] = jnp.where(first_half, a0[:, :LANES] / a0[:, LANES:], a1[:, :LANES] / a1[:, LANES:])


def _attn_prompt(qi, kw, qb, kit_blocks, kt_blocks, v_bf, n_sel):
    s = qi.shape[0]
    nkb_total = s // ATT_TK
    row = lambda i: (i, 0)
    whole = pl.BlockSpec(memory_space=pltpu.VMEM)
    return pl.pallas_call(
        functools.partial(_attn_prompt_kernel, n_keys=s, n_sel=n_sel),
        grid=(s // ATT_TQ,),
        in_specs=[
            pl.BlockSpec((ATT_TQ, GROUP), row),
            pl.BlockSpec((ATT_TQ, LANES), row),
            pl.BlockSpec((ATT_TQ, GROUP), row),
            whole, whole, whole,
        ],
        out_specs=pl.BlockSpec((ATT_TQ, GROUP), row),
        out_shape=jax.ShapeDtypeStruct((s, GROUP), F32),
        scratch_shapes=[
            pltpu.VMEM((nkb_total, ATT_TQ, ATT_TK), F32),
            pltpu.VMEM((H_I, ATT_TQ, LANES), F32),
            pltpu.VMEM((H_I, ATT_TQ, 3 * D_I), BF16),
            pltpu.VMEM((H_B, ATT_TQ, HD_B), BF16),
            pltpu.VMEM((H_B, ATT_TQ, ATT_TK), F32),
            pltpu.VMEM((H_B, ATT_TQ, ATT_TK), F32),
            pltpu.VMEM((H_B, ATT_TQ, LANES), F32),
            pltpu.VMEM((H_B, ATT_TQ, LANES), F32),
            pltpu.VMEM((ATT_TQ, ATT_TK), F32),
            pltpu.VMEM((ATT_TQ, ATT_TK), F32),
            pltpu.VMEM((H_B, ATT_TQ, ATT_TK), BF16),
            pltpu.VMEM((H_B, ATT_TQ, ATT_TK), BF16),
            pltpu.VMEM((H_B, ATT_TQ, LANES), F32),
            pltpu.VMEM((H_B, ATT_TQ, 2 * LANES), F32),
        ],
        compiler_params=pltpu.CompilerParams(
            dimension_semantics=("arbitrary",), vmem_limit_bytes=VMEM_LIMIT_BYTES),
        name="attn_prompt",
    )(qi, kw, qb, kit_blocks, kt_blocks, v_bf)


def _attn_sample_kernel(pt_ref, qi_ref, kw_ref, qb_ref, kin_ref, kn_ref, vn_ref,
                        ckidx_hbm, ck_hbm, cv_hbm, o_ref,
                        kid_buf, k_buf, v_buf, sc_ref, m_ref, l_ref, acc_ref, sem_i, sem_k, sem_v,
                        *, n_pages, t_new, n_sel):
    b = pl.program_id(0)
    r = SUBLANES
    nrow = H_B * r
    n_chunks = n_pages // KV_CHUNK_PAGES

    def page_lanes(p):
        start = (p % KV_CHUNK_PAGES) * PAGE_SIZE
        return pl.ds(start if isinstance(start, int) else pl.multiple_of(start, PAGE_SIZE), PAGE_SIZE)

    def kidx_copy(p):
        return pltpu.make_async_copy(ckidx_hbm.at[pt_ref[b, p]],
                                     kid_buf.at[p // KV_CHUNK_PAGES, :, page_lanes(p)], sem_i)

    def kv_copies(c, slot, p):
        page = pt_ref[b, c * KV_CHUNK_PAGES + p]
        return (pltpu.make_async_copy(ck_hbm.at[page], k_buf.at[slot, :, :, page_lanes(p)], sem_k.at[slot]),
                pltpu.make_async_copy(cv_hbm.at[page], v_buf.at[slot, :, :, page_lanes(p)], sem_v.at[slot]))

    def start_chunk(c, slot):
        for p in range(KV_CHUNK_PAGES):
            ck, cv = kv_copies(c, slot, p)
            ck.start()
            cv.start()

    def wait_chunk(c, slot):
        for p in range(KV_CHUNK_PAGES):
            ck, cv = kv_copies(c, slot, p)
            ck.wait()
            cv.wait()

    def start_kidx(p, _):
        kidx_copy(p).start()
        return 0

    def wait_kidx(p, _):
        kidx_copy(p).wait()
        return 0

    lax.fori_loop(0, n_pages, start_kidx, 0)
    for c0 in range(KV_SLOTS - 1):
        start_chunk(c0, c0)

    kw = kw_ref[0]
    qi = qi_ref[0]
    q_hi, q_lo, w_rows = [], [], []
    for h in range(H_I):
        hi, lo = _split_bf16(qi[:, h * D_I:(h + 1) * D_I])
        q_hi.append(hi)
        q_lo.append(lo)
        w_rows.append(jnp.broadcast_to(kw[:, D_I + h:D_I + h + 1], (r, LANES)))
    q_hi = jnp.concatenate(q_hi, axis=0)
    q_hl = jnp.concatenate([q_hi, jnp.concatenate(q_lo, axis=0)], axis=0)
    w_rows = jnp.concatenate(w_rows, axis=0)
    nt = (((1,), (1,)), ((), ()))

    def key_scores(keys_t):
        k_hi, k_lo = _split_bf16(keys_t)
        d2 = jnp.dot(q_hl, k_hi, preferred_element_type=F32)
        d = d2[:H_I * r] + d2[H_I * r:] + jnp.dot(q_hi, k_lo, preferred_element_type=F32)
        out = []
        for u in range(keys_t.shape[1] // LANES):
            us = slice(u * LANES, (u + 1) * LANES)
            wd = w_rows * jnp.maximum(d[:, us], 0.0)
            acc = wd[0:r]
            for h in range(1, H_I):
                acc = acc + wd[h * r:(h + 1) * r]
            out.append(acc)
        return out

    lax.fori_loop(0, n_pages, wait_kidx, 0)

    def score_body(c, _):
        for u, tile in enumerate(key_scores(kid_buf[c])):
            sc_ref[c, :, u * LANES:(u + 1) * LANES] = tile
        return 0

    lax.fori_loop(0, n_chunks, score_body, 0)

    lane = lax.broadcasted_iota(I32, (r, LANES), 1)
    t_row = jnp.minimum(lax.broadcasted_iota(I32, (r, LANES), 0), t_new - 1)
    new_sc = key_scores(kin_ref[0])[0]
    new_vis = lane <= t_row
    sc_ref[n_chunks] = jnp.full((r, KV_CHUNK_PAGES * LANES), -jnp.inf, F32)
    sc_ref[n_chunks, :, 0:LANES] = jnp.where(lane >= t_new, -jnp.inf, jnp.where(new_vis, new_sc, NEG_SCORE))

    thr, jmax = _select_rows(sc_ref, n_chunks + 1, jnp.zeros((r, LANES), I32), n_sel)

    qb = qb_ref[0] * (HD_B ** -0.5)
    col_head = lax.shift_right_logical(lax.broadcasted_iota(I32, (r, GROUP), 1), HD_B.bit_length() - 1)
    q_bd = jnp.concatenate([jnp.where(col_head == h, qb, 0.0) for h in range(H_B)], axis=0).astype(BF16)

    m_ref[...] = jnp.full(m_ref.shape, MASK_VALUE, F32)
    l_ref[...] = jnp.zeros(l_ref.shape, F32)
    acc_ref[...] = jnp.zeros(acc_ref.shape, F32)

    def attend(blk, n_blk, k_t, v_t, extra_mask):
        bias = []
        for u in range(n_blk):
            x = sc_ref[blk, :, u * LANES:(u + 1) * LANES]
            idx = (blk * KV_CHUNK_PAGES + u) * LANES + lane
            sel = jnp.logical_or(x > thr, jnp.logical_and(x == thr, idx <= jmax))
            if extra_mask is not None:
                sel = jnp.logical_and(sel, extra_mask)
            bias.append(jnp.where(sel, 0.0, MASK_VALUE))
        bias = jnp.tile(jnp.concatenate(bias, axis=1), (H_B, 1))
        s = jnp.dot(q_bd, k_t.astype(BF16), preferred_element_type=F32) + bias
        m_old = m_ref[...]
        m_new = jnp.maximum(m_old, jnp.broadcast_to(jnp.max(s, axis=1, keepdims=True), m_old.shape))
        alpha = jnp.exp(m_old - m_new)
        p = jnp.exp(s - jnp.tile(m_new, (1, n_blk)))
        p_sum = p[:, :LANES]
        for u in range(1, n_blk):
            p_sum = p_sum + p[:, u * LANES:(u + 1) * LANES]
        l_ref[...] = alpha * l_ref[...] + p_sum
        pv = lax.dot_general(p.astype(BF16), v_t.astype(BF16), nt, preferred_element_type=F32)
        acc_ref[...] = jnp.tile(alpha, (1, GROUP // LANES)) * acc_ref[...] + pv
        m_ref[...] = m_new

    def chunk_body(c, _):
        slot = c % KV_SLOTS
        wait_chunk(c, slot)
        ahead = c + KV_SLOTS - 1

        @pl.when(ahead < n_chunks)
        def _():
            start_chunk(ahead, ahead % KV_SLOTS)

        n_keys_chunk = KV_CHUNK_PAGES * PAGE_SIZE
        attend(c, KV_CHUNK_PAGES, k_buf[slot].reshape(GROUP, n_keys_chunk),
               v_buf[slot].reshape(GROUP, n_keys_chunk), None)
        return 0

    lax.fori_loop(0, n_chunks, chunk_body, 0)
    attend(n_chunks, 1, kn_ref[0].reshape(GROUP, PAGE_SIZE), vn_ref[0].reshape(GROUP, PAGE_SIZE), new_vis)

    out = jnp.zeros((r, GROUP), F32)
    for h in range(H_B):
        rs = slice(h * r, (h + 1) * r)
        denom = jnp.sum(l_ref[rs, :], axis=1, keepdims=True)
        out = out + jnp.where(col_head == h, acc_ref[rs, :] / denom, 0.0)
    o_ref[0] = out


def _attn_sample(page_table, qi, kw, qb, ki_new, k_new, v_new, cache_kidx, cache_k, cache_v, t_new, n_sel):
    nb, n_pages = page_table.shape
    r = SUBLANES
    nrow = H_B * r
    per_b = lambda b, pt: (b, 0, 0)
    per_b4 = lambda b, pt: (b, 0, 0, 0)
    anyspec = pl.BlockSpec(memory_space=pl.ANY)
    chunk_keys = KV_CHUNK_PAGES * PAGE_SIZE
    n_chunks = n_pages // KV_CHUNK_PAGES
    grid_spec = pltpu.PrefetchScalarGridSpec(
        num_scalar_prefetch=1,
        grid=(nb,),
        in_specs=[
            pl.BlockSpec((1, r, GROUP), per_b),
            pl.BlockSpec((1, r, LANES), per_b),
            pl.BlockSpec((1, r, GROUP), per_b),
            pl.BlockSpec((1, D_I, PAGE_SIZE), per_b),
            pl.BlockSpec((1, H_B, HD_B, PAGE_SIZE), per_b4),
            pl.BlockSpec((1, H_B, HD_B, PAGE_SIZE), per_b4),
            anyspec, anyspec, anyspec,
        ],
        out_specs=pl.BlockSpec((1, r, GROUP), per_b),
        scratch_shapes=[
            pltpu.VMEM((n_chunks, D_I, chunk_keys), F32),
            pltpu.VMEM((KV_SLOTS, H_B, HD_B, chunk_keys), F32),
            pltpu.VMEM((KV_SLOTS, H_B, HD_B, chunk_keys), F32),
            pltpu.VMEM((n_chunks + 1, r, chunk_keys), F32),
            pltpu.VMEM((nrow, LANES), F32),
            pltpu.VMEM((nrow, LANES), F32),
            pltpu.VMEM((nrow, GROUP), F32),
            pltpu.SemaphoreType.DMA(()),
            pltpu.SemaphoreType.DMA((KV_SLOTS,)),
            pltpu.SemaphoreType.DMA((KV_SLOTS,)),
        ],
    )
    return pl.pallas_call(
        functools.partial(_attn_sample_kernel, n_pages=n_pages, t_new=t_new, n_sel=n_sel),
        grid_spec=grid_spec,
        out_shape=jax.ShapeDtypeStruct((nb, r, GROUP), F32),
        compiler_params=pltpu.CompilerParams(
            dimension_semantics=("arbitrary",), vmem_limit_bytes=VMEM_LIMIT_BYTES),
        name="attn_sample",
    )(page_table, qi, kw, qb, ki_new, k_new, v_new, cache_kidx, cache_k, cache_v)


def _out_ffn_kernel(x_ref, oa_ref, ob_ref, p1_ref, p2_ref, wo_ref, g2_ref, wu_ref, cw_ref, cb_ref, wd_ref, gf_ref,
                    y_ref, u_ref, carry_ref, x1_ref, h2_ref, *, d_ff, ff_tile, seq_rows, carry_mode):
    tm = x_ref.shape[0]
    i = pl.program_id(0)
    mixed = jnp.concatenate([oa_ref[...], ob_ref[...]], axis=1).astype(BF16)
    x1 = x_ref[...] + jnp.dot(mixed, wo_ref[...], preferred_element_type=F32)
    h2_ref[...] = _rms(x1, g2_ref[...]).astype(BF16)
    x1_ref[...] = x1

    row = lax.broadcasted_iota(I32, (tm, ff_tile), 0)
    if carry_mode:
        @pl.when(i == 0)
        def _():
            carry_ref[...] = jnp.zeros(carry_ref.shape, F32)
        t_in_seq = row
    else:
        t_in_seq = row & (seq_rows - 1)

    for c in range(d_ff // ff_tile):
        cs = slice(c * ff_tile, (c + 1) * ff_tile)
        h2 = h2_ref[...]
        u = jnp.dot(h2, wu_ref[:, cs], preferred_element_type=F32)
        v = jnp.dot(h2, wu_ref[:, d_ff + c * ff_tile:d_ff + (c + 1) * ff_tile], preferred_element_type=F32)
        if carry_mode:
            prev = carry_ref[:, cs]
            p1 = jnp.broadcast_to(prev[SUBLANES - 1:SUBLANES], (tm, ff_tile))
            p2 = jnp.where(row == 0, jnp.broadcast_to(prev[SUBLANES - 2:SUBLANES - 1], (tm, ff_tile)), p1)
            carry_ref[:, cs] = u[tm - SUBLANES:]
            u_ref[:, cs] = u[tm - SUBLANES:]
        else:
            p1 = p1_ref[:, cs]
            p2 = p2_ref[:, cs]
            u_ref[:, cs] = u
        u1 = jnp.where(t_in_seq >= 1, pltpu.roll(u, 1, 0), p1)
        u2 = jnp.where(t_in_seq >= 2, pltpu.roll(u, 2, 0), p2)
        conv = cb_ref[:, cs] + cw_ref[0:1, cs] * u2 + cw_ref[1:2, cs] * u1 + cw_ref[2:3, cs] * u
        gate = (_silu(conv) * v).astype(BF16)
        x1_ref[...] += jnp.dot(gate, wd_ref[cs, :], preferred_element_type=F32)

    y_ref[...] = _rms(x1_ref[...], gf_ref[...])


def _out_ffn(x2d, oa, ob, p1, p2, w_out, g2, w_up, conv_w, conv_b, w_down, gf, tm, seq_rows, carry_mode):
    m, d = x2d.shape
    d_ff = w_down.shape[0]
    ff_tile = 256
    row = lambda i: (i, 0)
    const = lambda i: (0, 0)
    if carry_mode:
        prev_spec = pl.BlockSpec((SUBLANES, d_ff), const)
        u_spec = pl.BlockSpec((SUBLANES, d_ff), const)
        u_shape = jax.ShapeDtypeStruct((SUBLANES, d_ff), F32)
    else:
        prev_spec = pl.BlockSpec((tm, d_ff), row)
        u_spec = pl.BlockSpec((tm, d_ff), row)
        u_shape = jax.ShapeDtypeStruct((m, d_ff), F32)
    return pl.pallas_call(
        functools.partial(_out_ffn_kernel, d_ff=d_ff, ff_tile=ff_tile, seq_rows=seq_rows, carry_mode=carry_mode),
        grid=(m // tm,),
        in_specs=[
            pl.BlockSpec((tm, d), row),
            pl.BlockSpec((tm, GROUP), row),
            pl.BlockSpec((tm, GROUP), row),
            prev_spec, prev_spec,
            pl.BlockSpec(w_out.shape, const, pipeline_mode=pl.Buffered(1)),
            pl.BlockSpec((1, d), const),
            pl.BlockSpec(w_up.shape, const, pipeline_mode=pl.Buffered(1)),
            pl.BlockSpec(conv_w.shape, const),
            pl.BlockSpec((1, d_ff), const),
            pl.BlockSpec(w_down.shape, const, pipeline_mode=pl.Buffered(1)),
            pl.BlockSpec((1, d), const),
        ],
        out_specs=[pl.BlockSpec((tm, d), row), u_spec],
        out_shape=[jax.ShapeDtypeStruct((m, d), F32), u_shape],
        scratch_shapes=[pltpu.VMEM((SUBLANES, d_ff), F32), pltpu.VMEM((tm, d), F32), pltpu.VMEM((tm, d), BF16)],
        compiler_params=pltpu.CompilerParams(
            dimension_semantics=("arbitrary",), vmem_limit_bytes=VMEM_LIMIT_BYTES),
        name="out_ffn",
    )(x2d, oa, ob, p1, p2, w_out, g2, w_up, conv_w, conv_b, w_down, gf)


def _rope_tables(pos):
    half = ROT // 2
    inv = jnp.power(ROPE_THETA, -jnp.arange(half, dtype=F32) * (2.0 / ROT))
    ang = pos.astype(F32)[:, None] * inv[None, :]
    cos, sin = jnp.cos(ang), jnp.sin(ang)
    n = pos.shape[0]
    ones = jnp.ones((n, HD_B - ROT), F32)
    zeros = jnp.zeros((n, HD_B - ROT), F32)
    zh = jnp.zeros((n, half), F32)
    c = jnp.concatenate([cos, cos, ones], axis=1)
    sa = jnp.concatenate([zh, sin, zeros], axis=1)
    sb = jnp.concatenate([-sin, zh, zeros], axis=1)
    rep = LANES // HD_B
    return jnp.tile(c, (1, rep)), jnp.tile(sa, (1, rep)), jnp.tile(sb, (1, rep))


def kernel(x_prompt, x_sample, cache_k, cache_v, cache_kidx, state_hgrn, state_conv, page_table,
           norm_mix_gain, w_in, lb_logits, hgrn_norm_gain, w_out, norm_ffn_gain, w_up, conv_w,
           conv_b, w_down, final_norm_gain):
    bp, s, d = x_prompt.shape
    nb, t_new, _ = x_sample.shape
    depth = w_in.shape[0]
    assert bp == 1 and depth == 1
    n_pages = page_table.shape[1]
    past = n_pages * PAGE_SIZE
    d_ff = w_down.shape[1]
    assert s % (2 * ATT_TK) == 0 and s % HGRN_CHUNK == 0
    assert n_pages % KV_CHUNK_PAGES == 0 and n_pages // KV_CHUNK_PAGES >= KV_SLOTS
    assert t_new <= SUBLANES and CONV_W - 1 <= t_new

    lower_bounds = jnp.cumsum(jax.nn.softmax(lb_logits.astype(F32), axis=0), axis=0)
    lb = lower_bounds[0][None, :]
    n_in = w_in.shape[2]
    n_pad = 8 * GROUP + LANES
    w_in_bf = jnp.pad(w_in[0], ((0, 0), (0, n_pad - n_in))).astype(BF16)
    w_out_bf = w_out[0].astype(BF16)
    w_up_bf = w_up[0].astype(BF16)
    w_down_bf = w_down[0].astype(BF16)
    g_mix = norm_mix_gain[0][None, :]
    g_ffn = norm_ffn_gain[0][None, :]
    g_fin = final_norm_gain[None, :]
    g_hgrn = hgrn_norm_gain[0][None, :]
    conv_b2 = conv_b[0][None, :]

    xp = x_prompt.reshape(s, d)
    cp, sap, sbp = _rope_tables(jnp.arange(s, dtype=I32))
    qa, ka, lf, ia, ga, qb, kb, vb, qi, kw = _proj_in(xp, g_mix, w_in_bf, lb, cp, sap, sbp, tm=PROJ_ROWS)

    s0 = jnp.zeros((1, H_A, DK_A, DK_A), F32)
    r3 = lambda a: a.reshape(1, s, GROUP)
    oa_p, st_p = _hgrn(r3(qa), r3(ka), r3(lf), r3(ia), r3(ga), g_hgrn, s0, HGRN_CHUNK, HGRN_SUB)

    nkb = s // ATT_TK
    ki_hi, ki_lo = _split_bf16(kw[:, :D_I])
    kit = jnp.concatenate([ki_hi, ki_hi, ki_lo], axis=1)
    kit_blocks = kit.reshape(nkb, ATT_TK, 3 * D_I).transpose(0, 2, 1)
    kt_blocks = kb.astype(BF16).reshape(nkb, ATT_TK, GROUP).transpose(0, 2, 1)
    ob_p = _attn_prompt(qi, kw, qb, kit_blocks, kt_blocks, vb.astype(BF16), min(TOPK_MAX, s // 4))

    zero_prev = jnp.zeros((SUBLANES, d_ff), F32)
    y_p, u_tail = _out_ffn(xp, oa_p.reshape(s, GROUP), ob_p, zero_prev, zero_prev, w_out_bf, g_ffn, w_up_bf,
                           conv_w[0], conv_b2, w_down_bf, g_fin, tm=PROJ_ROWS, seq_rows=s, carry_mode=True)

    ms = nb * t_new
    xs = x_sample.reshape(ms, d)
    pos_s = jnp.tile(past + jnp.arange(t_new, dtype=I32), nb)
    cs, sas, sbs = _rope_tables(pos_s)
    qa2, ka2, lf2, ia2, ga2, qb2, kb2, vb2, qi2, kw2 = _proj_in(xs, g_mix, w_in_bf, lb, cs, sas, sbs, tm=ms)

    pad_front = SUBLANES - t_new
    fp = lambda a: jnp.pad(a.reshape(nb, t_new, GROUP), ((0, 0), (pad_front, 0), (0, 0)))
    s0_s = jnp.swapaxes(state_hgrn[0], -1, -2)
    oa_s, st_s = _hgrn(fp(qa2), fp(ka2), fp(lf2), fp(ia2), fp(ga2), g_hgrn, s0_s, SUBLANES, SUBLANES)
    oa_s = oa_s[:, pad_front:, :].reshape(ms, GROUP)

    def rp(a):
        a = a.reshape(nb, t_new, a.shape[-1])
        return jnp.concatenate([a, jnp.broadcast_to(a[:, -1:], (nb, SUBLANES - t_new, a.shape[-1]))], axis=1)
    page_pad = lambda a: jnp.pad(a.reshape(nb, t_new, a.shape[-1]), ((0, 0), (0, PAGE_SIZE - t_new), (0, 0)))
    new_page = lambda a: jnp.swapaxes(page_pad(a), 1, 2)
    heads = lambda a: new_page(a).reshape(nb, H_B, HD_B, PAGE_SIZE)
    ob_s = _attn_sample(
        page_table, rp(qi2), rp(kw2), rp(qb2), new_page(kw2[:, :D_I]), heads(kb2), heads(vb2),
        jnp.transpose(cache_kidx[0], (0, 2, 1)), jnp.transpose(cache_k[0], (0, 2, 3, 1)),
        jnp.transpose(cache_v[0], (0, 2, 3, 1)),
        t_new, min(TOPK_MAX, (past + t_new) // 4))
    ob_s = ob_s[:, :t_new, :].reshape(ms, GROUP)

    sc0 = state_conv[0]
    zrow = jnp.zeros((nb, t_new - 1, d_ff), F32)
    p1 = jnp.concatenate([sc0[:, 1:2], zrow], axis=1).reshape(ms, d_ff)
    p2 = jnp.concatenate([sc0[:, 0:2], zrow[:, 1:]], axis=1).reshape(ms, d_ff)
    y_s, u_s = _out_ffn(xs, oa_s, ob_s, p1, p2, w_out_bf, g_ffn, w_up_bf, conv_w[0], conv_b2, w_down_bf, g_fin,
                        tm=ms, seq_rows=t_new, carry_mode=False)

    return (
        y_p.reshape(1, s, d),
        y_s.reshape(nb, t_new, d),
        kb.reshape(1, 1, s, H_B, HD_B),
        vb.reshape(1, 1, s, H_B, HD_B),
        kw[:, :D_I].reshape(1, 1, s, D_I),
        jnp.swapaxes(st_p, -1, -2).reshape(1, 1, H_A, DK_A, DK_A),
        u_tail[SUBLANES - (CONV_W - 1):].reshape(1, 1, CONV_W - 1, d_ff),
        kb2.reshape(1, nb, t_new, H_B, HD_B),
        vb2.reshape(1, nb, t_new, H_B, HD_B),
        kw2[:, :D_I].reshape(1, nb, t_new, D_I),
        jnp.swapaxes(st_s, -1, -2).reshape(1, nb, H_A, DK_A, DK_A),
        u_s.reshape(nb, t_new, d_ff)[:, t_new - (CONV_W - 1):].reshape(1, nb, CONV_W - 1, d_ff),
    )
```

```python
import functools

import jax
import jax.numpy as jnp
from jax import lax
from jax.experimental import pallas as pl
from jax.experimental.pallas import tpu as pltpu

F32 = jnp.float32
BF16 = jnp.bfloat16
I32 = jnp.int32

H_A = 4
DK_A = 128
H_B = 8
HD_B = 64
H_I = 8
D_I = 64
ROT = 16
ROPE_THETA = 500000.0
TOPK_MAX = 256
PAGE_SIZE = 128
CONV_W = 3
EPS = 1e-6
NEG_SCORE = -1e30
MASK_VALUE = -1e30
GROUP = 512
LOG2_E = 1.4426950408889634

LANES = 128
SUBLANES = 8
VMEM_LIMIT_BYTES = 61 * 1024 * 1024

PROJ_ROWS = 512
HGRN_CHUNK = 64
HGRN_SUB = 16
BISECT_BLIND_PASSES = 10
BISECT_TESTED_ROUNDS = 5
ATT_TQ = 128
ATT_TK = 512
KV_CHUNK_PAGES = 8
KV_SLOTS = 4


def _silu(x):
    return x * jax.nn.sigmoid(x)


def _rms(x, g):
    return x * lax.rsqrt(jnp.mean(x * x, axis=-1, keepdims=True) + EPS) * g


def _split_bf16(x):
    hi = x.astype(BF16)
    lo = (x - hi.astype(F32)).astype(BF16)
    return hi, lo


def _rope(x, c, sa, sb):
    w = x.shape[1]
    return x * c + pltpu.roll(x, ROT // 2, 1) * sa + pltpu.roll(x, w - ROT // 2, 1) * sb


def _proj_in_kernel(x_ref, g_ref, w_ref, lb_ref, c_ref, sa_ref, sb_ref,
                    qa_ref, ka_ref, lf_ref, ia_ref, ga_ref, qb_ref, kb_ref, vb_ref, qi_ref, kw_ref, h_ref):
    h_ref[...] = _rms(x_ref[...], g_ref[...]).astype(BF16)

    def grp(i, width=GROUP):
        return jnp.dot(h_ref[...], w_ref[:, i * GROUP:i * GROUP + width], preferred_element_type=F32)

    c = jnp.tile(c_ref[...], (1, GROUP // LANES))
    sa = jnp.tile(sa_ref[...], (1, GROUP // LANES))
    sb = jnp.tile(sb_ref[...], (1, GROUP // LANES))

    qa_ref[...] = _silu(grp(0))
    lb = lb_ref[...]
    fg = lb + (1.0 - lb) * jax.nn.sigmoid(grp(1))
    ka_ref[...] = 1.0 - fg
    lf_ref[...] = jnp.log(fg)
    ia_ref[...] = grp(2)
    ga_ref[...] = _silu(grp(3))
    qb_ref[...] = _rope(grp(4), c, sa, sb)
    kb_ref[...] = _rope(grp(5), c, sa, sb)
    vb_ref[...] = grp(6)
    qi_ref[...] = _rope(grp(7), c, sa, sb)
    kw = grp(8, LANES)
    lane = lax.broadcasted_iota(I32, kw.shape, 1)
    roped = _rope(kw, c_ref[...], sa_ref[...], sb_ref[...])
    kw_ref[...] = jnp.where(lane < D_I, roped, kw * (H_I ** -0.5 * D_I ** -0.5))


def _proj_in(x2d, gain, w_pad, lb, cos_t, sin_a, sin_b, tm):
    m, d = x2d.shape
    n_pad = w_pad.shape[1]
    row = lambda i: (i, 0)
    const = lambda i: (0, 0)
    big = pl.BlockSpec((tm, GROUP), row)
    out_shapes = [jax.ShapeDtypeStruct((m, GROUP), F32)] * 9 + [jax.ShapeDtypeStruct((m, LANES), F32)]
    return pl.pallas_call(
        _proj_in_kernel,
        grid=(m // tm,),
        in_specs=[
            pl.BlockSpec((tm, d), row),
            pl.BlockSpec((1, d), const),
            pl.BlockSpec((d, n_pad), const, pipeline_mode=pl.Buffered(1)),
            pl.BlockSpec((1, GROUP), const),
            pl.BlockSpec((tm, LANES), row),
            pl.BlockSpec((tm, LANES), row),
            pl.BlockSpec((tm, LANES), row),
        ],
        out_specs=[big] * 9 + [pl.BlockSpec((tm, LANES), row)],
        out_shape=out_shapes,
        scratch_shapes=[pltpu.VMEM((tm, d), BF16)],
        compiler_params=pltpu.CompilerParams(
            dimension_semantics=("arbitrary",), vmem_limit_bytes=VMEM_LIMIT_BYTES),
        name="proj_in",
    )(x2d, gain, w_pad, lb, cos_t, sin_a, sin_b)


def _shift_rows(x, d):
    if d == 0:
        return x
    return pltpu.roll(x, d, 0)


def _cumsum_rows(g):
    c = g.shape[0]
    row = lax.broadcasted_iota(I32, g.shape, 0)
    k = 1
    while k < c:
        g = g + jnp.where(row >= k, _shift_rows(g, k), 0.0)
        k *= 2
    return g


def _hgrn_kernel(q_ref, k_ref, lf_ref, v_ref, ga_ref, gain_ref, s0_ref, o_ref, sfin_ref, st_ref, *, chunk, sub):
    ci = pl.program_id(1)

    @pl.when(ci == 0)
    def _():
        st_ref[...] = s0_ref[0]

    nsb = chunk // sub
    row = lax.broadcasted_iota(I32, (chunk, DK_A), 0)
    row_in_sub = row & (sub - 1)
    gain = gain_ref[...]

    for h in range(H_A):
        sl = slice(h * DK_A, (h + 1) * DK_A)
        q = q_ref[0, :, sl]
        k = k_ref[0, :, sl]
        v = v_ref[0, :, sl]
        b = _cumsum_rows(lf_ref[0, :, sl])
        st = st_ref[h]

        refs = [jnp.zeros((1, DK_A), F32)] + [b[i * sub - 1:i * sub, :] for i in range(1, nsb)]
        ref_rows = jnp.concatenate([jnp.broadcast_to(r, (sub, DK_A)) for r in refs], axis=0)
        q_rel = q * jnp.exp(b - ref_rows)

        o = lax.dot_general((q * jnp.exp(b)).astype(BF16), st.astype(BF16),
                            (((1,), (1,)), ((), ())), preferred_element_type=F32)

        off_rows = [jnp.zeros((sub, DK_A), F32)]
        for i in range(1, nsb):
            n_prev = i * sub
            k_rel = (k[:n_prev] * jnp.exp(refs[i] - b[:n_prev])).astype(BF16)
            att = lax.dot_general(q_rel[n_prev:n_prev + sub].astype(BF16), k_rel,
                                  (((1,), (1,)), ((), ())), preferred_element_type=F32)
            off_rows.append(jnp.dot(att.astype(BF16), v[:n_prev].astype(BF16), preferred_element_type=F32))
        if nsb > 1:
            o = o + jnp.concatenate(off_rows, axis=0)

        for d in range(sub):
            valid = row_in_sub >= d
            e = jnp.exp(jnp.where(valid, b - _shift_rows(b, d), 0.0))
            w = jnp.sum(q * _shift_rows(k, d) * e, axis=1, keepdims=True)
            o = o + jnp.where(valid, w * _shift_rows(v, d), 0.0)

        b_last = b[chunk - 1:chunk, :]
        k_dec = (k * jnp.exp(b_last - b)).astype(BF16)
        st_ref[h] = st * jnp.exp(b_last) + lax.dot_general(
            v.astype(BF16), k_dec, (((0,), (0,)), ((), ())), preferred_element_type=F32)

        o_ref[0, :, sl] = _rms(o, gain) * ga_ref[0, :, sl]

    @pl.when(ci == pl.num_programs(1) - 1)
    def _():
        sfin_ref[0] = st_ref[...]


def _hgrn(q, k, lf, v, ga, gain, s0_t, chunk, sub):
    b, t, w = q.shape
    blk = pl.BlockSpec((1, chunk, w), lambda bi, ci: (bi, ci, 0))
    st_spec = pl.BlockSpec((1, H_A, DK_A, DK_A), lambda bi, ci: (bi, 0, 0, 0))
    return pl.pallas_call(
        functools.partial(_hgrn_kernel, chunk=chunk, sub=sub),
        grid=(b, t // chunk),
        in_specs=[blk, blk, blk, blk, blk, pl.BlockSpec((1, DK_A), lambda bi, ci: (0, 0)), st_spec],
        out_specs=[blk, st_spec],
        out_shape=[jax.ShapeDtypeStruct((b, t, w), F32), jax.ShapeDtypeStruct((b, H_A, DK_A, DK_A), F32)],
        scratch_shapes=[pltpu.VMEM((H_A, DK_A, DK_A), F32)],
        compiler_params=pltpu.CompilerParams(
            dimension_semantics=("arbitrary", "arbitrary"), vmem_limit_bytes=VMEM_LIMIT_BYTES),
        name="hgrn",
    )(q, k, lf, v, ga, gain, s0_t)


def _rep_sum(x):
    return jnp.broadcast_to(jnp.sum(x, axis=1, keepdims=True), x.shape)


def _rep_max(x):
    return jnp.broadcast_to(jnp.max(x, axis=1, keepdims=True), x.shape)


def _rep_min(x):
    return jnp.broadcast_to(jnp.min(x, axis=1, keepdims=True), x.shape)


def _any_true(flag):
    return jnp.max(flag) > 0.5


def _lane_tiles(x):
    return [x[:, u * LANES:(u + 1) * LANES] for u in range(x.shape[1] // LANES)]


def _select_rows(sc_ref, nkb, n_hidden_tail, n_sel):
    _, r, w = sc_ref.shape
    shape = (r, LANES)
    neg = jnp.float32(NEG_SCORE)
    inf = jnp.float32(jnp.inf)
    kf = jnp.float32(n_sel)
    tail = n_hidden_tail.astype(F32)
    zeros = jnp.zeros(shape, F32)
    lane = lax.broadcasted_iota(I32, shape, 1)

    def count_ge(c):
        def body(j, acc):
            for xu in _lane_tiles(sc_ref[j]):
                acc = acc + jnp.where(xu >= c, 1.0, 0.0)
            return acc
        if isinstance(nkb, int):
            acc = functools.reduce(lambda a, b: a + b, [body(j, zeros) for j in range(nkb)])
        else:
            acc = lax.fori_loop(0, nkb, body, zeros)
        return _rep_sum(acc) + jnp.where(neg >= c, tail, 0.0)

    assert n_sel <= 2 * LANES

    def stats_body(j, carry):
        top1, top2, n_real = carry
        for x in _lane_tiles(sc_ref[j]):
            n_real = n_real + jnp.where(x > neg, 1.0, 0.0)
            top2 = jnp.maximum(top2, jnp.minimum(top1, x))
            top1 = jnp.maximum(top1, x)
        return top1, top2, n_real

    def rare_stats_body(j, carry):
        vmin_real, n_ge_neg, gmin = carry
        for x in _lane_tiles(sc_ref[j]):
            vmin_real = jnp.minimum(vmin_real, jnp.where(x > neg, x, inf))
            n_ge_neg = n_ge_neg + jnp.where(x >= neg, 1.0, 0.0)
            gmin = jnp.minimum(gmin, jnp.where(x > -inf, x, inf))
        return vmin_real, n_ge_neg, gmin

    top1, top2, n_real = lax.fori_loop(0, nkb, stats_body, (zeros - inf, zeros - inf, zeros))
    has_tail = tail > 0.5
    vmax = _rep_max(top1)
    vmax = jnp.where(has_tail, jnp.maximum(vmax, neg), vmax)
    n_real = _rep_sum(n_real)
    lo_lanes = _rep_min(top2)
    rare = _any_true(jnp.where(jnp.logical_or(n_real < kf, lo_lanes <= neg), 1.0, 0.0))
    vmin_real, n_ge_neg, gmin = lax.fori_loop(0, jnp.where(rare, nkb, 0), rare_stats_body,
                                              (zeros + inf, zeros, zeros + inf))
    vmin_real = jnp.where(lo_lanes > neg, lo_lanes, _rep_min(vmin_real))
    gmin = _rep_min(gmin)
    gmin = jnp.where(has_tail, jnp.minimum(gmin, neg), gmin)
    n_ge_neg = _rep_sum(n_ge_neg) + tail
    above_max = vmax + jnp.maximum(jnp.abs(vmax), 1e-30) * 1e-6

    few_real = n_real < kf
    at_neg = jnp.logical_and(few_real, n_ge_neg >= kf)
    below_neg = jnp.logical_and(few_real, n_ge_neg < kf)

    done = jnp.where(at_neg, 1.0, 0.0)
    thr = jnp.where(at_neg, neg, zeros)
    n_gt = jnp.where(at_neg, n_real, zeros)
    tie = jnp.where(jnp.logical_and(at_neg, n_ge_neg > kf), 1.0, 0.0)
    lo = jnp.where(below_neg, gmin, vmin_real)
    hi = jnp.where(below_neg, neg, above_max)
    c_hi = jnp.where(below_neg, n_ge_neg, zeros)

    def bisect(st):
        done, thr, n_gt, tie, lo, hi, c_hi = st
        mid = lo + (hi - lo) * 0.5
        c = count_ge(mid)
        live = done < 0.5
        hit = jnp.logical_and(live, c == kf)
        up = jnp.logical_and(live, c > kf)
        dn = jnp.logical_and(live, c < kf)
        return (jnp.where(hit, 1.0, done), jnp.where(hit, mid, thr), n_gt, tie,
                jnp.where(up, mid, lo), jnp.where(dn, mid, hi), jnp.where(dn, c, c_hi))

    def snap(st):
        done, thr, n_gt, tie, lo, hi, c_hi = st

        def body(j, m):
            for xu in _lane_tiles(sc_ref[j]):
                m = jnp.maximum(m, jnp.where(xu < hi, xu, -inf))
            return m
        below = _rep_max(lax.fori_loop(0, nkb, body, zeros - inf))
        below = jnp.where(jnp.logical_and(tail > 0.5, neg < hi), jnp.maximum(below, neg), below)
        c = count_ge(below)
        live = done < 0.5
        fin = jnp.logical_and(live, c >= kf)
        mv = jnp.logical_and(live, c < kf)
        return (jnp.where(fin, 1.0, done), jnp.where(fin, below, thr), jnp.where(fin, c_hi, n_gt),
                jnp.where(fin, jnp.where(c > kf, 1.0, 0.0), tie),
                lo, jnp.where(mv, below, hi), jnp.where(mv, c, c_hi))

    def not_done(st):
        return _any_true(1.0 - st[0])

    st = (done, thr, n_gt, tie, lo, hi, c_hi)
    st = lax.fori_loop(0, BISECT_BLIND_PASSES, lambda _, s: bisect(s), st)
    _, st = lax.while_loop(
        lambda ps: jnp.logical_and(ps[0] < BISECT_TESTED_ROUNDS, not_done(ps[1])),
        lambda ps: (ps[0] + 1, bisect(bisect(ps[1]))), (jnp.int32(0), st))
    st = lax.while_loop(not_done, lambda s: snap(bisect(s)), st)
    done, thr, n_gt, tie, lo, hi, c_hi = st

    any_tie = _any_true(tie)
    tie_row = tie > 0.5
    n_stored = nkb * w

    @pl.when(any_tie)
    def _():
        def body(j, _):
            for u in range(w // LANES):
                us = slice(u * LANES, (u + 1) * LANES)
                xu = sc_ref[j, :, us]
                idx = (j * w + u * LANES + lane).astype(F32)
                tied = jnp.where(tie_row, -idx, inf)
                sc_ref[j, :, us] = jnp.where(xu > thr, inf, jnp.where(xu == thr, tied, -inf))
            return 0
        lax.fori_loop(0, nkb, body, 0)

    def jstep(pj):
        p, (jlo, jhi) = pj
        jmid = jlo + lax.shift_right_arithmetic(jhi - jlo, 1)
        ok = count_ge(-(jmid.astype(F32))) >= kf
        return p + 1, (jnp.where(ok, jlo, jmid), jnp.where(ok, jmid, jhi))

    n_jpass = jnp.where(any_tie, 16, 0)
    jlo0 = jnp.full(shape, -1, I32)
    jhi0 = jnp.zeros(shape, I32) + n_stored
    _, (_, jhi) = lax.while_loop(lambda pj: pj[0] < n_jpass, jstep, (jnp.int32(0), (jlo0, jhi0)))
    encoded = (zeros + jnp.max(tie)) > 0.5
    return jnp.where(encoded, jnp.where(tie_row, -(jhi.astype(F32)), zeros), thr)


def _attn_prompt_kernel(qi_ref, kw_ref, qb_ref, kit_ref, kt_ref, v_ref, o_ref,
                        sc_ref, wb_ref, qc_ref, qh_ref, sa_ref, sb_ref, mxa_ref, mxb_ref, ba_ref, bb_ref,
                        pa_ref, pb_ref,
                        m_ref, acc_ref,
                        *, n_keys, n_sel):
    tq, tk = ATT_TQ, ATT_TK
    tiles = [slice(u * LANES, (u + 1) * LANES) for u in range(tk // LANES)]
    i = pl.program_id(0)
    nkb = (i * tq + tq + tk - 1) // tk
    last_blk = n_keys // tk - 1
    q_pos = i * tq + lax.broadcasted_iota(I32, (tq, LANES), 0)
    lane = lax.broadcasted_iota(I32, (tq, LANES), 1)

    @pl.when(i == 0)
    def _():
        sc_ref[...] = jnp.full(sc_ref.shape, NEG_SCORE, F32)

    kw = kw_ref[...]
    for h in range(H_I):
        hi, lo = _split_bf16(qi_ref[:, h * D_I:(h + 1) * D_I])
        qc_ref[h] = jnp.concatenate([hi, lo, hi], axis=1)
        wb_ref[h] = jnp.broadcast_to(kw[:, D_I + h:D_I + h + 1], (tq, LANES))

    def score_body(jj, _):
        for h in range(H_I):
            wbh = wb_ref[h]
            for j in (2 * jj, 2 * jj + 1):
                d = jnp.dot(qc_ref[h], kit_ref[j], preferred_element_type=F32)
                for u, us in enumerate(tiles):
                    val = wbh * jnp.maximum(d[:, us], 0.0)
                    if h > 0:
                        val = sc_ref[j, :, us] + val
                    if h == H_I - 1:
                        val = jnp.where(j * tk + u * LANES + lane <= q_pos, val, NEG_SCORE)
                    sc_ref[j, :, us] = val
        return 0

    lax.fori_loop(0, (nkb + 1) // 2, score_body, 0)

    tail = jnp.zeros((tq, LANES), I32) + (n_keys - nkb * tk)
    cut = _select_rows(sc_ref, nkb, tail, n_sel)

    m_ref[...] = jnp.full(m_ref.shape, MASK_VALUE, F32)
    acc_ref[...] = jnp.zeros(acc_ref.shape, F32)
    ones_blk = jnp.ones((tk, LANES), BF16)
    for h in range(H_B):
        qh_ref[h] = (qb_ref[:, h * HD_B:(h + 1) * HD_B] * (HD_B ** -0.5 * LOG2_E)).astype(BF16)
    first_half = lane < HD_B

    def selection_bias(blk, b_ref):
        for u, us in enumerate(tiles):
            sel = jnp.logical_and(sc_ref[blk, :, us] >= cut, blk * tk + u * LANES + lane <= q_pos)
            b_ref[:, us] = jnp.where(sel, 0.0, MASK_VALUE)

    def masked_logits(blk, h, s_ref, mx_ref, b_ref):
        s = jnp.dot(qh_ref[h], kt_ref[blk, h * HD_B:(h + 1) * HD_B, :], preferred_element_type=F32) + b_ref[...]
        s_ref[h] = s
        mx = s[:, tiles[0]]
        for us in tiles[1:]:
            mx = jnp.maximum(mx, s[:, us])
        mx_ref[h] = _rep_max(mx)

    def reduce_block(blk, s_ref, mx_ref, nxt, b_ref, p_ref):
        rows = pl.ds(pl.multiple_of(blk * tk, tk), tk)
        for pair in range(H_B // 2):
            ps = slice(pair * LANES, (pair + 1) * LANES)
            v_ext = jnp.concatenate([v_ref[rows, ps], ones_blk], axis=1)
            for half in range(2):
                h = 2 * pair + half
                m_old = m_ref[h]
                m_new = jnp.maximum(m_old, mx_ref[h])
                alpha = jnp.exp2(m_old - m_new)
                acc_ref[h] = jnp.tile(alpha, (1, 2)) * acc_ref[h]
                for us in tiles:
                    p_ref[h, :, us] = jnp.exp2(s_ref[h, :, us] - m_new).astype(BF16)
                m_ref[h] = m_new
                masked_logits(nxt, h, s_ref, mx_ref, b_ref)
                acc_ref[h] += jnp.dot(p_ref[h], v_ext, preferred_element_type=F32)

    selection_bias(0, ba_ref)
    selection_bias(1, bb_ref)
    for h in range(H_B):
        masked_logits(0, h, sa_ref, mxa_ref, ba_ref)
        masked_logits(1, h, sb_ref, mxb_ref, bb_ref)

    def pair_body(jj, _):
        a = 2 * jj
        nxt_a = jnp.minimum(a + 2, last_blk)
        nxt_b = jnp.minimum(a + 3, last_blk)
        selection_bias(nxt_a, ba_ref)
        selection_bias(nxt_b, bb_ref)
        reduce_block(a, sa_ref, mxa_ref, nxt_a, ba_ref, pa_ref)
        reduce_block(a + 1, sb_ref, mxb_ref, nxt_b, bb_ref, pb_ref)
        return 0

    lax.fori_loop(0, (nkb + 1) // 2, pair_body, 0)

    for pair in range(H_B // 2):
        ps = slice(pair * LANES, (pair + 1) * LANES)
        a0, a1 = acc_ref[2 * pair], acc_ref[2 * pair + 1]
        o_ref[:, ps] = jnp.where(first_half, a0[:, :LANES] / a0[:, LANES:], a1[:, :LANES] / a1[:, LANES:])


def _attn_prompt(qi, kw, qb, kit_blocks, kt_blocks, v_bf, n_sel):
    s = qi.shape[0]
    nkb_total = s // ATT_TK
    row = lambda i: (i, 0)
    whole = pl.BlockSpec(memory_space=pltpu.VMEM)
    return pl.pallas_call(
        functools.partial(_attn_prompt_kernel, n_keys=s, n_sel=n_sel),
        grid=(s // ATT_TQ,),
        in_specs=[
            pl.BlockSpec((ATT_TQ, GROUP), row),
            pl.BlockSpec((ATT_TQ, LANES), row),
            pl.BlockSpec((ATT_TQ, GROUP), row),
            whole, whole, whole,
        ],
        out_specs=pl.BlockSpec((ATT_TQ, GROUP), row),
        out_shape=jax.ShapeDtypeStruct((s, GROUP), F32),
        scratch_shapes=[
            pltpu.VMEM((nkb_total, ATT_TQ, ATT_TK), F32),
            pltpu.VMEM((H_I, ATT_TQ, LANES), F32),
            pltpu.VMEM((H_I, ATT_TQ, 3 * D_I), BF16),
            pltpu.VMEM((H_B, ATT_TQ, HD_B), BF16),
            pltpu.VMEM((H_B, ATT_TQ, ATT_TK), F32),
            pltpu.VMEM((H_B, ATT_TQ, ATT_TK), F32),
            pltpu.VMEM((H_B, ATT_TQ, LANES), F32),
            pltpu.VMEM((H_B, ATT_TQ, LANES), F32),
            pltpu.VMEM((ATT_TQ, ATT_TK), F32),
            pltpu.VMEM((ATT_TQ, ATT_TK), F32),
            pltpu.VMEM((H_B, ATT_TQ, ATT_TK), BF16),
            pltpu.VMEM((H_B, ATT_TQ, ATT_TK), BF16),
            pltpu.VMEM((H_B, ATT_TQ, LANES), F32),
            pltpu.VMEM((H_B, ATT_TQ, 2 * LANES), F32),
        ],
        compiler_params=pltpu.CompilerParams(
            dimension_semantics=("arbitrary",), vmem_limit_bytes=VMEM_LIMIT_BYTES),
        name="attn_prompt",
    )(qi, kw, qb, kit_blocks, kt_blocks, v_bf)


def _attn_sample_kernel(pt_ref, qi_ref, kw_ref, qb_ref, kin_ref, kn_ref, vn_ref,
                        ckidx_hbm, ck_hbm, cv_hbm, o_ref,
                        kid_buf, k_buf, v_buf, sc_ref, m_ref, l_ref, acc_ref, sem_i, sem_k, sem_v,
                        *, n_pages, t_new, n_sel):
    b = pl.program_id(0)
    r = SUBLANES
    nrow = H_B * r
    n_chunks = n_pages // KV_CHUNK_PAGES

    def page_lanes(p):
        start = (p % KV_CHUNK_PAGES) * PAGE_SIZE
        return pl.ds(start if isinstance(start, int) else pl.multiple_of(start, PAGE_SIZE), PAGE_SIZE)

    def kidx_copy(p):
        return pltpu.make_async_copy(ckidx_hbm.at[pt_ref[b, p]],
                                     kid_buf.at[p // KV_CHUNK_PAGES, :, page_lanes(p)], sem_i)

    def kv_copies(c, slot, p):
        page = pt_ref[b, c * KV_CHUNK_PAGES + p]
        return (pltpu.make_async_copy(ck_hbm.at[page], k_buf.at[slot, :, :, page_lanes(p)], sem_k.at[slot]),
                pltpu.make_async_copy(cv_hbm.at[page], v_buf.at[slot, :, :, page_lanes(p)], sem_v.at[slot]))

    def start_chunk(c, slot):
        for p in range(KV_CHUNK_PAGES):
            ck, cv = kv_copies(c, slot, p)
            ck.start()
            cv.start()

    def wait_chunk(c, slot):
        for p in range(KV_CHUNK_PAGES):
            ck, cv = kv_copies(c, slot, p)
            ck.wait()
            cv.wait()

    def start_kidx(p, _):
        kidx_copy(p).start()
        return 0

    def wait_kidx(p, _):
        kidx_copy(p).wait()
        return 0

    lax.fori_loop(0, n_pages, start_kidx, 0)
    for c0 in range(KV_SLOTS - 1):
        start_chunk(c0, c0)

    kw = kw_ref[0]
    qi = qi_ref[0]
    q_hi, q_lo, w_rows = [], [], []
    for h in range(H_I):
        hi, lo = _split_bf16(qi[:, h * D_I:(h + 1) * D_I])
        q_hi.append(hi)
        q_lo.append(lo)
        w_rows.append(jnp.broadcast_to(kw[:, D_I + h:D_I + h + 1], (r, LANES)))
    q_hi = jnp.concatenate(q_hi, axis=0)
    q_hl = jnp.concatenate([q_hi, jnp.concatenate(q_lo, axis=0)], axis=0)
    w_rows = jnp.concatenate(w_rows, axis=0)
    nt = (((1,), (1,)), ((), ()))

    def key_scores(keys_t):
        k_hi, k_lo = _split_bf16(keys_t)
        d2 = jnp.dot(q_hl, k_hi, preferred_element_type=F32)
        d = d2[:H_I * r] + d2[H_I * r:] + jnp.dot(q_hi, k_lo, preferred_element_type=F32)
        out = []
        for u in range(keys_t.shape[1] // LANES):
            us = slice(u * LANES, (u + 1) * LANES)
            wd = w_rows * jnp.maximum(d[:, us], 0.0)
            acc = wd[0:r]
            for h in range(1, H_I):
                acc = acc + wd[h * r:(h + 1) * r]
            out.append(acc)
        return out

    lax.fori_loop(0, n_pages, wait_kidx, 0)

    def score_body(c, _):
        for u, tile in enumerate(key_scores(kid_buf[c])):
            sc_ref[c, :, u * LANES:(u + 1) * LANES] = tile
        return 0

    lax.fori_loop(0, n_chunks, score_body, 0)

    lane = lax.broadcasted_iota(I32, (r, LANES), 1)
    t_row = jnp.minimum(lax.broadcasted_iota(I32, (r, LANES), 0), t_new - 1)
    new_sc = key_scores(kin_ref[0])[0]
    new_vis = lane <= t_row
    sc_ref[n_chunks] = jnp.full((r, KV_CHUNK_PAGES * LANES), -jnp.inf, F32)
    sc_ref[n_chunks, :, 0:LANES] = jnp.where(lane >= t_new, -jnp.inf, jnp.where(new_vis, new_sc, NEG_SCORE))

    cut = _select_rows(sc_ref, n_chunks + 1, jnp.zeros((r, LANES), I32), n_sel)

    qb = qb_ref[0] * (HD_B ** -0.5)
    col_head = lax.shift_right_logical(lax.broadcasted_iota(I32, (r, GROUP), 1), HD_B.bit_length() - 1)
    q_bd = jnp.concatenate([jnp.where(col_head == h, qb, 0.0) for h in range(H_B)], axis=0).astype(BF16)

    m_ref[...] = jnp.full(m_ref.shape, MASK_VALUE, F32)
    l_ref[...] = jnp.zeros(l_ref.shape, F32)
    acc_ref[...] = jnp.zeros(acc_ref.shape, F32)

    def attend(blk, n_blk, k_t, v_t, extra_mask):
        bias = []
        for u in range(n_blk):
            sel = sc_ref[blk, :, u * LANES:(u + 1) * LANES] >= cut
            if extra_mask is not None:
                sel = jnp.logical_and(sel, extra_mask)
            bias.append(jnp.where(sel, 0.0, MASK_VALUE))
        bias = jnp.tile(jnp.concatenate(bias, axis=1), (H_B, 1))
        s = jnp.dot(q_bd, k_t.astype(BF16), preferred_element_type=F32) + bias
        m_old = m_ref[...]
        m_new = jnp.maximum(m_old, jnp.broadcast_to(jnp.max(s, axis=1, keepdims=True), m_old.shape))
        alpha = jnp.exp(m_old - m_new)
        p = jnp.exp(s - jnp.tile(m_new, (1, n_blk)))
        p_sum = p[:, :LANES]
        for u in range(1, n_blk):
            p_sum = p_sum + p[:, u * LANES:(u + 1) * LANES]
        l_ref[...] = alpha * l_ref[...] + p_sum
        pv = lax.dot_general(p.astype(BF16), v_t.astype(BF16), nt, preferred_element_type=F32)
        acc_ref[...] = jnp.tile(alpha, (1, GROUP // LANES)) * acc_ref[...] + pv
        m_ref[...] = m_new

    def chunk_body(c, _):
        slot = c % KV_SLOTS
        wait_chunk(c, slot)
        ahead = c + KV_SLOTS - 1

        @pl.when(ahead < n_chunks)
        def _():
            start_chunk(ahead, ahead % KV_SLOTS)

        n_keys_chunk = KV_CHUNK_PAGES * PAGE_SIZE
        attend(c, KV_CHUNK_PAGES, k_buf[slot].reshape(GROUP, n_keys_chunk),
               v_buf[slot].reshape(GROUP, n_keys_chunk), None)
        return 0

    lax.fori_loop(0, n_chunks, chunk_body, 0)
    attend(n_chunks, 1, kn_ref[0].reshape(GROUP, PAGE_SIZE), vn_ref[0].reshape(GROUP, PAGE_SIZE), new_vis)

    out = jnp.zeros((r, GROUP), F32)
    for h in range(H_B):
        rs = slice(h * r, (h + 1) * r)
        denom = jnp.sum(l_ref[rs, :], axis=1, keepdims=True)
        out = out + jnp.where(col_head == h, acc_ref[rs, :] / denom, 0.0)
    o_ref[0] = out


def _attn_sample(page_table, qi, kw, qb, ki_new, k_new, v_new, cache_kidx, cache_k, cache_v, t_new, n_sel):
    nb, n_pages = page_table.shape
    r = SUBLANES
    nrow = H_B * r
    per_b = lambda b, pt: (b, 0, 0)
    per_b4 = lambda b, pt: (b, 0, 0, 0)
    anyspec = pl.BlockSpec(memory_space=pl.ANY)
    chunk_keys = KV_CHUNK_PAGES * PAGE_SIZE
    n_chunks = n_pages // KV_CHUNK_PAGES
    grid_spec = pltpu.PrefetchScalarGridSpec(
        num_scalar_prefetch=1,
        grid=(nb,),
        in_specs=[
            pl.BlockSpec((1, r, GROUP), per_b),
            pl.BlockSpec((1, r, LANES), per_b),
            pl.BlockSpec((1, r, GROUP), per_b),
            pl.BlockSpec((1, D_I, PAGE_SIZE), per_b),
            pl.BlockSpec((1, H_B, HD_B, PAGE_SIZE), per_b4),
            pl.BlockSpec((1, H_B, HD_B, PAGE_SIZE), per_b4),
            anyspec, anyspec, anyspec,
        ],
        out_specs=pl.BlockSpec((1, r, GROUP), per_b),
        scratch_shapes=[
            pltpu.VMEM((n_chunks, D_I, chunk_keys), F32),
            pltpu.VMEM((KV_SLOTS, H_B, HD_B, chunk_keys), F32),
            pltpu.VMEM((KV_SLOTS, H_B, HD_B, chunk_keys), F32),
            pltpu.VMEM((n_chunks + 1, r, chunk_keys), F32),
            pltpu.VMEM((nrow, LANES), F32),
            pltpu.VMEM((nrow, LANES), F32),
            pltpu.VMEM((nrow, GROUP), F32),
            pltpu.SemaphoreType.DMA(()),
            pltpu.SemaphoreType.DMA((KV_SLOTS,)),
            pltpu.SemaphoreType.DMA((KV_SLOTS,)),
        ],
    )
    return pl.pallas_call(
        functools.partial(_attn_sample_kernel, n_pages=n_pages, t_new=t_new, n_sel=n_sel),
        grid_spec=grid_spec,
        out_shape=jax.ShapeDtypeStruct((nb, r, GROUP), F32),
        compiler_params=pltpu.CompilerParams(
            dimension_semantics=("arbitrary",), vmem_limit_bytes=VMEM_LIMIT_BYTES),
        name="attn_sample",
    )(page_table, qi, kw, qb, ki_new, k_new, v_new, cache_kidx, cache_k, cache_v)


def _out_ffn_kernel(x_ref, oa_ref, ob_ref, p1_ref, p2_ref, wo_ref, g2_ref, wu_ref, cw_ref, cb_ref, wd_ref, gf_ref,
                    y_ref, u_ref, carry_ref, x1_ref, h2_ref, *, d_ff, ff_tile, seq_rows, carry_mode):
    tm = x_ref.shape[0]
    i = pl.program_id(0)
    mixed = jnp.concatenate([oa_ref[...], ob_ref[...]], axis=1).astype(BF16)
    x1 = x_ref[...] + jnp.dot(mixed, wo_ref[...], preferred_element_type=F32)
    h2_ref[...] = _rms(x1, g2_ref[...]).astype(BF16)
    x1_ref[...] = x1

    row = lax.broadcasted_iota(I32, (tm, ff_tile), 0)
    if carry_mode:
        @pl.when(i == 0)
        def _():
            carry_ref[...] = jnp.zeros(carry_ref.shape, F32)
        t_in_seq = row
    else:
        t_in_seq = row & (seq_rows - 1)

    for c in range(d_ff // ff_tile):
        cs = slice(c * ff_tile, (c + 1) * ff_tile)
        h2 = h2_ref[...]
        u = jnp.dot(h2, wu_ref[:, cs], preferred_element_type=F32)
        v = jnp.dot(h2, wu_ref[:, d_ff + c * ff_tile:d_ff + (c + 1) * ff_tile], preferred_element_type=F32)
        if carry_mode:
            prev = carry_ref[:, cs]
            p1 = jnp.broadcast_to(prev[SUBLANES - 1:SUBLANES], (tm, ff_tile))
            p2 = jnp.where(row == 0, jnp.broadcast_to(prev[SUBLANES - 2:SUBLANES - 1], (tm, ff_tile)), p1)
            carry_ref[:, cs] = u[tm - SUBLANES:]
            u_ref[:, cs] = u[tm - SUBLANES:]
        else:
            p1 = p1_ref[:, cs]
            p2 = p2_ref[:, cs]
            u_ref[:, cs] = u
        u1 = jnp.where(t_in_seq >= 1, pltpu.roll(u, 1, 0), p1)
        u2 = jnp.where(t_in_seq >= 2, pltpu.roll(u, 2, 0), p2)
        conv = cb_ref[:, cs] + cw_ref[0:1, cs] * u2 + cw_ref[1:2, cs] * u1 + cw_ref[2:3, cs] * u
        gate = (_silu(conv) * v).astype(BF16)
        x1_ref[...] += jnp.dot(gate, wd_ref[cs, :], preferred_element_type=F32)

    y_ref[...] = _rms(x1_ref[...], gf_ref[...])


def _out_ffn(x2d, oa, ob, p1, p2, w_out, g2, w_up, conv_w, conv_b, w_down, gf, tm, seq_rows, carry_mode):
    m, d = x2d.shape
    d_ff = w_down.shape[0]
    ff_tile = 256
    row = lambda i: (i, 0)
    const = lambda i: (0, 0)
    if carry_mode:
        prev_spec = pl.BlockSpec((SUBLANES, d_ff), const)
        u_spec = pl.BlockSpec((SUBLANES, d_ff), const)
        u_shape = jax.ShapeDtypeStruct((SUBLANES, d_ff), F32)
    else:
        prev_spec = pl.BlockSpec((tm, d_ff), row)
        u_spec = pl.BlockSpec((tm, d_ff), row)
        u_shape = jax.ShapeDtypeStruct((m, d_ff), F32)
    return pl.pallas_call(
        functools.partial(_out_ffn_kernel, d_ff=d_ff, ff_tile=ff_tile, seq_rows=seq_rows, carry_mode=carry_mode),
        grid=(m // tm,),
        in_specs=[
            pl.BlockSpec((tm, d), row),
            pl.BlockSpec((tm, GROUP), row),
            pl.BlockSpec((tm, GROUP), row),
            prev_spec, prev_spec,
            pl.BlockSpec(w_out.shape, const, pipeline_mode=pl.Buffered(1)),
            pl.BlockSpec((1, d), const),
            pl.BlockSpec(w_up.shape, const, pipeline_mode=pl.Buffered(1)),
            pl.BlockSpec(conv_w.shape, const),
            pl.BlockSpec((1, d_ff), const),
            pl.BlockSpec(w_down.shape, const, pipeline_mode=pl.Buffered(1)),
            pl.BlockSpec((1, d), const),
        ],
        out_specs=[pl.BlockSpec((tm, d), row), u_spec],
        out_shape=[jax.ShapeDtypeStruct((m, d), F32), u_shape],
        scratch_shapes=[pltpu.VMEM((SUBLANES, d_ff), F32), pltpu.VMEM((tm, d), F32), pltpu.VMEM((tm, d), BF16)],
        compiler_params=pltpu.CompilerParams(
            dimension_semantics=("arbitrary",), vmem_limit_bytes=VMEM_LIMIT_BYTES),
        name="out_ffn",
    )(x2d, oa, ob, p1, p2, w_out, g2, w_up, conv_w, conv_b, w_down, gf)


def _rope_tables(pos):
    half = ROT // 2
    inv = jnp.power(ROPE_THETA, -jnp.arange(half, dtype=F32) * (2.0 / ROT))
    ang = pos.astype(F32)[:, None] * inv[None, :]
    cos, sin = jnp.cos(ang), jnp.sin(ang)
    n = pos.shape[0]
    ones = jnp.ones((n, HD_B - ROT), F32)
    zeros = jnp.zeros((n, HD_B - ROT), F32)
    zh = jnp.zeros((n, half), F32)
    c = jnp.concatenate([cos, cos, ones], axis=1)
    sa = jnp.concatenate([zh, sin, zeros], axis=1)
    sb = jnp.concatenate([-sin, zh, zeros], axis=1)
    rep = LANES // HD_B
    return jnp.tile(c, (1, rep)), jnp.tile(sa, (1, rep)), jnp.tile(sb, (1, rep))


def kernel(x_prompt, x_sample, cache_k, cache_v, cache_kidx, state_hgrn, state_conv, page_table,
           norm_mix_gain, w_in, lb_logits, hgrn_norm_gain, w_out, norm_ffn_gain, w_up, conv_w,
           conv_b, w_down, final_norm_gain):
    bp, s, d = x_prompt.shape
    nb, t_new, _ = x_sample.shape
    depth = w_in.shape[0]
    assert bp == 1 and depth == 1
    n_pages = page_table.shape[1]
    past = n_pages * PAGE_SIZE
    d_ff = w_down.shape[1]
    assert s % (2 * ATT_TK) == 0 and s % HGRN_CHUNK == 0
    assert n_pages % KV_CHUNK_PAGES == 0 and n_pages // KV_CHUNK_PAGES >= KV_SLOTS
    assert t_new <= SUBLANES and CONV_W - 1 <= t_new

    lower_bounds = jnp.cumsum(jax.nn.softmax(lb_logits.astype(F32), axis=0), axis=0)
    lb = lower_bounds[0][None, :]
    n_in = w_in.shape[2]
    n_pad = 8 * GROUP + LANES
    w_in_bf = jnp.pad(w_in[0], ((0, 0), (0, n_pad - n_in))).astype(BF16)
    w_out_bf = w_out[0].astype(BF16)
    w_up_bf = w_up[0].astype(BF16)
    w_down_bf = w_down[0].astype(BF16)
    g_mix = norm_mix_gain[0][None, :]
    g_ffn = norm_ffn_gain[0][None, :]
    g_fin = final_norm_gain[None, :]
    g_hgrn = hgrn_norm_gain[0][None, :]
    conv_b2 = conv_b[0][None, :]

    xp = x_prompt.reshape(s, d)
    cp, sap, sbp = _rope_tables(jnp.arange(s, dtype=I32))
    qa, ka, lf, ia, ga, qb, kb, vb, qi, kw = _proj_in(xp, g_mix, w_in_bf, lb, cp, sap, sbp, tm=PROJ_ROWS)

    s0 = jnp.zeros((1, H_A, DK_A, DK_A), F32)
    r3 = lambda a: a.reshape(1, s, GROUP)
    oa_p, st_p = _hgrn(r3(qa), r3(ka), r3(lf), r3(ia), r3(ga), g_hgrn, s0, HGRN_CHUNK, HGRN_SUB)

    nkb = s // ATT_TK
    ki_hi, ki_lo = _split_bf16(kw[:, :D_I])
    kit = jnp.concatenate([ki_hi, ki_hi, ki_lo], axis=1)
    kit_blocks = kit.reshape(nkb, ATT_TK, 3 * D_I).transpose(0, 2, 1)
    kt_blocks = kb.astype(BF16).reshape(nkb, ATT_TK, GROUP).transpose(0, 2, 1)
    ob_p = _attn_prompt(qi, kw, qb, kit_blocks, kt_blocks, vb.astype(BF16), min(TOPK_MAX, s // 4))

    zero_prev = jnp.zeros((SUBLANES, d_ff), F32)
    y_p, u_tail = _out_ffn(xp, oa_p.reshape(s, GROUP), ob_p, zero_prev, zero_prev, w_out_bf, g_ffn, w_up_bf,
                           conv_w[0], conv_b2, w_down_bf, g_fin, tm=PROJ_ROWS, seq_rows=s, carry_mode=True)

    ms = nb * t_new
    xs = x_sample.reshape(ms, d)
    pos_s = jnp.tile(past + jnp.arange(t_new, dtype=I32), nb)
    cs, sas, sbs = _rope_tables(pos_s)
    qa2, ka2, lf2, ia2, ga2, qb2, kb2, vb2, qi2, kw2 = _proj_in(xs, g_mix, w_in_bf, lb, cs, sas, sbs, tm=ms)

    pad_front = SUBLANES - t_new
    fp = lambda a: jnp.pad(a.reshape(nb, t_new, GROUP), ((0, 0), (pad_front, 0), (0, 0)))
    s0_s = jnp.swapaxes(state_hgrn[0], -1, -2)
    oa_s, st_s = _hgrn(fp(qa2), fp(ka2), fp(lf2), fp(ia2), fp(ga2), g_hgrn, s0_s, SUBLANES, SUBLANES)
    oa_s = oa_s[:, pad_front:, :].reshape(ms, GROUP)

    def rp(a):
        a = a.reshape(nb, t_new, a.shape[-1])
        return jnp.concatenate([a, jnp.broadcast_to(a[:, -1:], (nb, SUBLANES - t_new, a.shape[-1]))], axis=1)
    page_pad = lambda a: jnp.pad(a.reshape(nb, t_new, a.shape[-1]), ((0, 0), (0, PAGE_SIZE - t_new), (0, 0)))
    new_page = lambda a: jnp.swapaxes(page_pad(a), 1, 2)
    heads = lambda a: new_page(a).reshape(nb, H_B, HD_B, PAGE_SIZE)
    ob_s = _attn_sample(
        page_table, rp(qi2), rp(kw2), rp(qb2), new_page(kw2[:, :D_I]), heads(kb2), heads(vb2),
        jnp.transpose(cache_kidx[0], (0, 2, 1)), jnp.transpose(cache_k[0], (0, 2, 3, 1)),
        jnp.transpose(cache_v[0], (0, 2, 3, 1)),
        t_new, min(TOPK_MAX, (past + t_new) // 4))
    ob_s = ob_s[:, :t_new, :].reshape(ms, GROUP)

    sc0 = state_conv[0]
    zrow = jnp.zeros((nb, t_new - 1, d_ff), F32)
    p1 = jnp.concatenate([sc0[:, 1:2], zrow], axis=1).reshape(ms, d_ff)
    p2 = jnp.concatenate([sc0[:, 0:2], zrow[:, 1:]], axis=1).reshape(ms, d_ff)
    y_s, u_s = _out_ffn(xs, oa_s, ob_s, p1, p2, w_out_bf, g_ffn, w_up_bf, conv_w[0], conv_b2, w_down_bf, g_fin,
                        tm=ms, seq_rows=t_new, carry_mode=False)

    return (
        y_p.reshape(1, s, d),
        y_s.reshape(nb, t_new, d),
        kb.reshape(1, 1, s, H_B, HD_B),
        vb.reshape(1, 1, s, H_B, HD_B),
        kw[:, :D_I].reshape(1, 1, s, D_I),
        jnp.swapaxes(st_p, -1, -2).reshape(1, 1, H_A, DK_A, DK_A),
        u_tail[SUBLANES - (CONV_W - 1):].reshape(1, 1, CONV_W - 1, d_ff),
        kb2.reshape(1, nb, t_new, H_B, HD_B),
        vb2.reshape(1, nb, t_new, H_B, HD_B),
        kw2[:, :D_I].reshape(1, nb, t_new, D_I),
        jnp.swapaxes(st_s, -1, -2).reshape(1, nb, H_A, DK_A, DK_A),
        u_s.reshape(nb, t_new, d_ff)[:, t_new - (CONV_W - 1):].reshape(1, nb, CONV_W - 1, d_ff),
    )
```

```python
import functools

import jax
import jax.numpy as jnp
from jax import lax
from jax.experimental import pallas as pl
from jax.experimental.pallas import tpu as pltpu

F32 = jnp.float32
BF16 = jnp.bfloat16
I32 = jnp.int32

H_A = 4
DK_A = 128
H_B = 8
HD_B = 64
H_I = 8
D_I = 64
ROT = 16
ROPE_THETA = 500000.0
TOPK_MAX = 256
PAGE_SIZE = 128
CONV_W = 3
EPS = 1e-6
NEG_SCORE = -1e30
MASK_VALUE = -1e30
GROUP = 512
LOG2_E = 1.4426950408889634

LANES = 128
SUBLANES = 8
VMEM_LIMIT_BYTES = 61 * 1024 * 1024

PROJ_ROWS = 512
HGRN_CHUNK = 64
HGRN_SUB = 16
BISECT_BLIND_PASSES = 10
BISECT_TESTED_ROUNDS = 5
ATT_TQ = 128
ATT_TK = 512
KV_CHUNK_PAGES = 8
KV_SLOTS = 4


def _silu(x):
    return x * jax.nn.sigmoid(x)


def _rms(x, g):
    return x * lax.rsqrt(jnp.mean(x * x, axis=-1, keepdims=True) + EPS) * g


def _split_bf16(x):
    hi = x.astype(BF16)
    lo = (x - hi.astype(F32)).astype(BF16)
    return hi, lo


def _rope(x, c, sa, sb):
    w = x.shape[1]
    return x * c + pltpu.roll(x, ROT // 2, 1) * sa + pltpu.roll(x, w - ROT // 2, 1) * sb


def _proj_in_kernel(x_ref, g_ref, w_ref, lb_ref, c_ref, sa_ref, sb_ref,
                    qa_ref, ka_ref, lf_ref, ia_ref, ga_ref, qb_ref, kb_ref, vb_ref, qi_ref, kw_ref, h_ref):
    h_ref[...] = _rms(x_ref[...], g_ref[...]).astype(BF16)

    def grp(i, width=GROUP):
        return jnp.dot(h_ref[...], w_ref[:, i * GROUP:i * GROUP + width], preferred_element_type=F32)

    c = jnp.tile(c_ref[...], (1, GROUP // LANES))
    sa = jnp.tile(sa_ref[...], (1, GROUP // LANES))
    sb = jnp.tile(sb_ref[...], (1, GROUP // LANES))

    qa_ref[...] = _silu(grp(0))
    lb = lb_ref[...]
    fg = lb + (1.0 - lb) * jax.nn.sigmoid(grp(1))
    ka_ref[...] = 1.0 - fg
    lf_ref[...] = jnp.log(fg)
    ia_ref[...] = grp(2)
    ga_ref[...] = _silu(grp(3))
    qb_ref[...] = _rope(grp(4), c, sa, sb)
    kb_ref[...] = _rope(grp(5), c, sa, sb)
    vb_ref[...] = grp(6)
    qi_ref[...] = _rope(grp(7), c, sa, sb)
    kw = grp(8, LANES)
    lane = lax.broadcasted_iota(I32, kw.shape, 1)
    roped = _rope(kw, c_ref[...], sa_ref[...], sb_ref[...])
    kw_ref[...] = jnp.where(lane < D_I, roped, kw * (H_I ** -0.5 * D_I ** -0.5))


def _proj_in(x2d, gain, w_pad, lb, cos_t, sin_a, sin_b, tm):
    m, d = x2d.shape
    n_pad = w_pad.shape[1]
    row = lambda i: (i, 0)
    const = lambda i: (0, 0)
    big = pl.BlockSpec((tm, GROUP), row)
    out_shapes = [jax.ShapeDtypeStruct((m, GROUP), F32)] * 9 + [jax.ShapeDtypeStruct((m, LANES), F32)]
    return pl.pallas_call(
        _proj_in_kernel,
        grid=(m // tm,),
        in_specs=[
            pl.BlockSpec((tm, d), row),
            pl.BlockSpec((1, d), const),
            pl.BlockSpec((d, n_pad), const, pipeline_mode=pl.Buffered(1)),
            pl.BlockSpec((1, GROUP), const),
            pl.BlockSpec((tm, LANES), row),
            pl.BlockSpec((tm, LANES), row),
            pl.BlockSpec((tm, LANES), row),
        ],
        out_specs=[big] * 9 + [pl.BlockSpec((tm, LANES), row)],
        out_shape=out_shapes,
        scratch_shapes=[pltpu.VMEM((tm, d), BF16)],
        compiler_params=pltpu.CompilerParams(
            dimension_semantics=("arbitrary",), vmem_limit_bytes=VMEM_LIMIT_BYTES),
        name="proj_in",
    )(x2d, gain, w_pad, lb, cos_t, sin_a, sin_b)


def _shift_rows(x, d):
    if d == 0:
        return x
    return pltpu.roll(x, d, 0)


def _cumsum_rows(g):
    c = g.shape[0]
    row = lax.broadcasted_iota(I32, g.shape, 0)
    k = 1
    while k < c:
        g = g + jnp.where(row >= k, _shift_rows(g, k), 0.0)
        k *= 2
    return g


def _hgrn_kernel(q_ref, k_ref, lf_ref, v_ref, ga_ref, gain_ref, s0_ref, o_ref, sfin_ref, st_ref, *, chunk, sub):
    ci = pl.program_id(1)

    @pl.when(ci == 0)
    def _():
        st_ref[...] = s0_ref[0]

    nsb = chunk // sub
    row = lax.broadcasted_iota(I32, (chunk, DK_A), 0)
    row_in_sub = row & (sub - 1)
    gain = gain_ref[...]

    for h in range(H_A):
        sl = slice(h * DK_A, (h + 1) * DK_A)
        q = q_ref[0, :, sl]
        k = k_ref[0, :, sl]
        v = v_ref[0, :, sl]
        b = _cumsum_rows(lf_ref[0, :, sl])
        st = st_ref[h]

        refs = [jnp.zeros((1, DK_A), F32)] + [b[i * sub - 1:i * sub, :] for i in range(1, nsb)]
        ref_rows = jnp.concatenate([jnp.broadcast_to(r, (sub, DK_A)) for r in refs], axis=0)
        q_rel = q * jnp.exp(b - ref_rows)

        o = lax.dot_general((q * jnp.exp(b)).astype(BF16), st.astype(BF16),
                            (((1,), (1,)), ((), ())), preferred_element_type=F32)

        off_rows = [jnp.zeros((sub, DK_A), F32)]
        for i in range(1, nsb):
            n_prev = i * sub
            k_rel = (k[:n_prev] * jnp.exp(refs[i] - b[:n_prev])).astype(BF16)
            att = lax.dot_general(q_rel[n_prev:n_prev + sub].astype(BF16), k_rel,
                                  (((1,), (1,)), ((), ())), preferred_element_type=F32)
            off_rows.append(jnp.dot(att.astype(BF16), v[:n_prev].astype(BF16), preferred_element_type=F32))
        if nsb > 1:
            o = o + jnp.concatenate(off_rows, axis=0)

        for d in range(sub):
            valid = row_in_sub >= d
            e = jnp.exp(jnp.where(valid, b - _shift_rows(b, d), 0.0))
            w = jnp.sum(q * _shift_rows(k, d) * e, axis=1, keepdims=True)
            o = o + jnp.where(valid, w * _shift_rows(v, d), 0.0)

        b_last = b[chunk - 1:chunk, :]
        k_dec = (k * jnp.exp(b_last - b)).astype(BF16)
        st_ref[h] = st * jnp.exp(b_last) + lax.dot_general(
            v.astype(BF16), k_dec, (((0,), (0,)), ((), ())), preferred_element_type=F32)

        o_ref[0, :, sl] = _rms(o, gain) * ga_ref[0, :, sl]

    @pl.when(ci == pl.num_programs(1) - 1)
    def _():
        sfin_ref[0] = st_ref[...]


def _hgrn(q, k, lf, v, ga, gain, s0_t, chunk, sub):
    b, t, w = q.shape
    blk = pl.BlockSpec((1, chunk, w), lambda bi, ci: (bi, ci, 0))
    st_spec = pl.BlockSpec((1, H_A, DK_A, DK_A), lambda bi, ci: (bi, 0, 0, 0))
    return pl.pallas_call(
        functools.partial(_hgrn_kernel, chunk=chunk, sub=sub),
        grid=(b, t // chunk),
        in_specs=[blk, blk, blk, blk, blk, pl.BlockSpec((1, DK_A), lambda bi, ci: (0, 0)), st_spec],
        out_specs=[blk, st_spec],
        out_shape=[jax.ShapeDtypeStruct((b, t, w), F32), jax.ShapeDtypeStruct((b, H_A, DK_A, DK_A), F32)],
        scratch_shapes=[pltpu.VMEM((H_A, DK_A, DK_A), F32)],
        compiler_params=pltpu.CompilerParams(
            dimension_semantics=("arbitrary", "arbitrary"), vmem_limit_bytes=VMEM_LIMIT_BYTES),
        name="hgrn",
    )(q, k, lf, v, ga, gain, s0_t)


def _rep_sum(x):
    return jnp.broadcast_to(jnp.sum(x, axis=1, keepdims=True), x.shape)


def _rep_max(x):
    return jnp.broadcast_to(jnp.max(x, axis=1, keepdims=True), x.shape)


def _rep_min(x):
    return jnp.broadcast_to(jnp.min(x, axis=1, keepdims=True), x.shape)


def _any_true(flag):
    return jnp.max(flag) > 0.5


def _lane_tiles(x):
    return [x[:, u * LANES:(u + 1) * LANES] for u in range(x.shape[1] // LANES)]


def _select_rows(sc_ref, nkb, n_hidden_tail, n_sel):
    _, r, w = sc_ref.shape
    shape = (r, LANES)
    neg = jnp.float32(NEG_SCORE)
    inf = jnp.float32(jnp.inf)
    kf = jnp.float32(n_sel)
    tail = n_hidden_tail.astype(F32)
    zeros = jnp.zeros(shape, F32)
    lane = lax.broadcasted_iota(I32, shape, 1)

    def count_ge(c):
        def body(j, acc):
            for xu in _lane_tiles(sc_ref[j]):
                acc = acc + jnp.where(xu >= c, 1.0, 0.0)
            return acc
        if isinstance(nkb, int):
            acc = functools.reduce(lambda a, b: a + b, [body(j, zeros) for j in range(nkb)])
        else:
            acc = lax.fori_loop(0, nkb, body, zeros)
        return _rep_sum(acc) + jnp.where(neg >= c, tail, 0.0)

    assert n_sel <= 2 * LANES

    def stats_body(j, carry):
        top1, top2, n_real = carry
        for x in _lane_tiles(sc_ref[j]):
            n_real = n_real + jnp.where(x > neg, 1.0, 0.0)
            top2 = jnp.maximum(top2, jnp.minimum(top1, x))
            top1 = jnp.maximum(top1, x)
        return top1, top2, n_real

    def rare_stats_body(j, carry):
        vmin_real, n_ge_neg, gmin = carry
        for x in _lane_tiles(sc_ref[j]):
            vmin_real = jnp.minimum(vmin_real, jnp.where(x > neg, x, inf))
            n_ge_neg = n_ge_neg + jnp.where(x >= neg, 1.0, 0.0)
            gmin = jnp.minimum(gmin, jnp.where(x > -inf, x, inf))
        return vmin_real, n_ge_neg, gmin

    top1, top2, n_real = lax.fori_loop(0, nkb, stats_body, (zeros - inf, zeros - inf, zeros))
    has_tail = tail > 0.5
    vmax = _rep_max(top1)
    vmax = jnp.where(has_tail, jnp.maximum(vmax, neg), vmax)
    n_real = _rep_sum(n_real)
    lo_lanes = _rep_min(top2)
    rare = _any_true(jnp.where(jnp.logical_or(n_real < kf, lo_lanes <= neg), 1.0, 0.0))
    vmin_real, n_ge_neg, gmin = lax.fori_loop(0, jnp.where(rare, nkb, 0), rare_stats_body,
                                              (zeros + inf, zeros, zeros + inf))
    vmin_real = jnp.where(lo_lanes > neg, lo_lanes, _rep_min(vmin_real))
    gmin = _rep_min(gmin)
    gmin = jnp.where(has_tail, jnp.minimum(gmin, neg), gmin)
    n_ge_neg = _rep_sum(n_ge_neg) + tail
    above_max = vmax + jnp.maximum(jnp.abs(vmax), 1e-30) * 1e-6

    few_real = n_real < kf
    at_neg = jnp.logical_and(few_real, n_ge_neg >= kf)
    below_neg = jnp.logical_and(few_real, n_ge_neg < kf)

    done = jnp.where(at_neg, 1.0, 0.0)
    thr = jnp.where(at_neg, neg, zeros)
    n_gt = jnp.where(at_neg, n_real, zeros)
    tie = jnp.where(jnp.logical_and(at_neg, n_ge_neg > kf), 1.0, 0.0)
    lo = jnp.where(below_neg, gmin, vmin_real)
    hi = jnp.where(below_neg, neg, above_max)
    c_hi = jnp.where(below_neg, n_ge_neg, zeros)

    def bisect(st):
        done, thr, n_gt, tie, lo, hi, c_hi = st
        mid = lo + (hi - lo) * 0.5
        c = count_ge(mid)
        live = done < 0.5
        hit = jnp.logical_and(live, c == kf)
        up = jnp.logical_and(live, c > kf)
        dn = jnp.logical_and(live, c < kf)
        return (jnp.where(hit, 1.0, done), jnp.where(hit, mid, thr), n_gt, tie,
                jnp.where(up, mid, lo), jnp.where(dn, mid, hi), jnp.where(dn, c, c_hi))

    def snap(st):
        done, thr, n_gt, tie, lo, hi, c_hi = st

        def body(j, m):
            for xu in _lane_tiles(sc_ref[j]):
                m = jnp.maximum(m, jnp.where(xu < hi, xu, -inf))
            return m
        below = _rep_max(lax.fori_loop(0, nkb, body, zeros - inf))
        below = jnp.where(jnp.logical_and(tail > 0.5, neg < hi), jnp.maximum(below, neg), below)
        c = count_ge(below)
        live = done < 0.5
        fin = jnp.logical_and(live, c >= kf)
        mv = jnp.logical_and(live, c < kf)
        return (jnp.where(fin, 1.0, done), jnp.where(fin, below, thr), jnp.where(fin, c_hi, n_gt),
                jnp.where(fin, jnp.where(c > kf, 1.0, 0.0), tie),
                lo, jnp.where(mv, below, hi), jnp.where(mv, c, c_hi))

    def not_done(st):
        return _any_true(1.0 - st[0])

    st = (done, thr, n_gt, tie, lo, hi, c_hi)
    st = lax.fori_loop(0, BISECT_BLIND_PASSES, lambda _, s: bisect(s), st)
    _, st = lax.while_loop(
        lambda ps: jnp.logical_and(ps[0] < BISECT_TESTED_ROUNDS, not_done(ps[1])),
        lambda ps: (ps[0] + 1, bisect(bisect(ps[1]))), (jnp.int32(0), st))
    st = lax.while_loop(not_done, lambda s: snap(bisect(s)), st)
    done, thr, n_gt, tie, lo, hi, c_hi = st

    any_tie = _any_true(tie)
    tie_row = tie > 0.5
    n_stored = nkb * w

    @pl.when(any_tie)
    def _():
        def body(j, _):
            for u in range(w // LANES):
                us = slice(u * LANES, (u + 1) * LANES)
                xu = sc_ref[j, :, us]
                idx = (j * w + u * LANES + lane).astype(F32)
                tied = jnp.where(tie_row, -idx, inf)
                sc_ref[j, :, us] = jnp.where(xu > thr, inf, jnp.where(xu == thr, tied, -inf))
            return 0
        lax.fori_loop(0, nkb, body, 0)

    def jstep(pj):
        p, (jlo, jhi) = pj
        jmid = jlo + lax.shift_right_arithmetic(jhi - jlo, 1)
        ok = count_ge(-(jmid.astype(F32))) >= kf
        return p + 1, (jnp.where(ok, jlo, jmid), jnp.where(ok, jmid, jhi))

    n_jpass = jnp.where(any_tie, 16, 0)
    jlo0 = jnp.full(shape, -1, I32)
    jhi0 = jnp.zeros(shape, I32) + n_stored
    _, (_, jhi) = lax.while_loop(lambda pj: pj[0] < n_jpass, jstep, (jnp.int32(0), (jlo0, jhi0)))
    encoded = (zeros + jnp.max(tie)) > 0.5
    return jnp.where(encoded, jnp.where(tie_row, -(jhi.astype(F32)), zeros), thr)


def _attn_prompt_kernel(qi_ref, kw_ref, qb_ref, kit_ref, kt_ref, v_ref, o_ref,
                        sc_ref, wb_ref, qc_ref, qh_ref, sa_ref, sb_ref, mxa_ref, mxb_ref, ba_ref, bb_ref,
                        pa_ref, pb_ref,
                        m_ref, acc_ref,
                        *, n_keys, n_sel):
    tq, tk = ATT_TQ, ATT_TK
    tiles = [slice(u * LANES, (u + 1) * LANES) for u in range(tk // LANES)]
    i = pl.program_id(0)
    nkb = (i * tq + tq + tk - 1) // tk
    last_blk = n_keys // tk - 1
    q_pos = i * tq + lax.broadcasted_iota(I32, (tq, LANES), 0)
    lane = lax.broadcasted_iota(I32, (tq, LANES), 1)

    @pl.when(i == 0)
    def _():
        sc_ref[...] = jnp.full(sc_ref.shape, NEG_SCORE, F32)

    kw = kw_ref[...]
    for h in range(H_I):
        hi, lo = _split_bf16(qi_ref[:, h * D_I:(h + 1) * D_I])
        qc_ref[h] = jnp.concatenate([hi, lo, hi], axis=1)
        wb_ref[h] = jnp.broadcast_to(kw[:, D_I + h:D_I + h + 1], (tq, LANES))

    def score_body(jj, _):
        for h in range(H_I):
            wbh = wb_ref[h]
            for j in (2 * jj, 2 * jj + 1):
                d = jnp.dot(qc_ref[h], kit_ref[j], preferred_element_type=F32)
                for u, us in enumerate(tiles):
                    val = wbh * jnp.maximum(d[:, us], 0.0)
                    if h > 0:
                        val = sc_ref[j, :, us] + val
                    if h == H_I - 1:
                        val = jnp.where(j * tk + u * LANES + lane <= q_pos, val, NEG_SCORE)
                    sc_ref[j, :, us] = val
        return 0

    lax.fori_loop(0, (nkb + 1) // 2, score_body, 0)

    tail = jnp.zeros((tq, LANES), I32) + (n_keys - nkb * tk)
    cut = _select_rows(sc_ref, nkb, tail, n_sel)

    m_ref[...] = jnp.full(m_ref.shape, MASK_VALUE, F32)
    acc_ref[...] = jnp.zeros(acc_ref.shape, F32)
    ones_blk = jnp.ones((tk, LANES), BF16)
    for h in range(H_B):
        qh_ref[h] = (qb_ref[:, h * HD_B:(h + 1) * HD_B] * (HD_B ** -0.5 * LOG2_E)).astype(BF16)
    first_half = lane < HD_B

    def selection_bias(blk, b_ref):
        for u, us in enumerate(tiles):
            sel = jnp.logical_and(sc_ref[blk, :, us] >= cut, blk * tk + u * LANES + lane <= q_pos)
            b_ref[:, us] = jnp.where(sel, 0.0, MASK_VALUE)

    def masked_logits(blk, h, s_ref, mx_ref, b_ref):
        s = jnp.dot(qh_ref[h], kt_ref[blk, h * HD_B:(h + 1) * HD_B, :], preferred_element_type=F32) + b_ref[...]
        s_ref[h] = s
        mx = s[:, tiles[0]]
        for us in tiles[1:]:
            mx = jnp.maximum(mx, s[:, us])
        mx_ref[h] = _rep_max(mx)

    def reduce_block(blk, s_ref, mx_ref, nxt, b_ref, p_ref):
        rows = pl.ds(pl.multiple_of(blk * tk, tk), tk)
        for pair in range(H_B // 2):
            ps = slice(pair * LANES, (pair + 1) * LANES)
            v_ext = jnp.concatenate([v_ref[rows, ps], ones_blk], axis=1)
            for half in range(2):
                h = 2 * pair + half
                m_old = m_ref[h]
                m_new = jnp.maximum(m_old, mx_ref[h])
                alpha = jnp.exp2(m_old - m_new)
                acc_ref[h] = jnp.tile(alpha, (1, 2)) * acc_ref[h]
                for us in tiles:
                    p_ref[h, :, us] = jnp.exp2(s_ref[h, :, us] - m_new).astype(BF16)
                m_ref[h] = m_new
                masked_logits(nxt, h, s_ref, mx_ref, b_ref)
                acc_ref[h] += jnp.dot(p_ref[h], v_ext, preferred_element_type=F32)

    selection_bias(0, ba_ref)
    selection_bias(1, bb_ref)
    for h in range(H_B):
        masked_logits(0, h, sa_ref, mxa_ref, ba_ref)
        masked_logits(1, h, sb_ref, mxb_ref, bb_ref)

    def pair_body(jj, _):
        a = 2 * jj
        nxt_a = jnp.minimum(a + 2, last_blk)
        nxt_b = jnp.minimum(a + 3, last_blk)
        selection_bias(nxt_a, ba_ref)
        selection_bias(nxt_b, bb_ref)
        reduce_block(a, sa_ref, mxa_ref, nxt_a, ba_ref, pa_ref)
        reduce_block(a + 1, sb_ref, mxb_ref, nxt_b, bb_ref, pb_ref)
        return 0

    lax.fori_loop(0, (nkb + 1) // 2, pair_body, 0)

    for pair in range(H_B // 2):
        ps = slice(pair * LANES, (pair + 1) * LANES)
        a0, a1 = acc_ref[2 * pair], acc_ref[2 * pair + 1]
        o_ref[:, ps] = jnp.where(first_half, a0[:, :LANES] / a0[:, LANES:], a1[:, :LANES] / a1[:, LANES:])


def _attn_prompt(qi, kw, qb, kit_blocks, kt_blocks, v_bf, n_sel):
    s = qi.shape[0]
    nkb_total = s // ATT_TK
    row = lambda i: (i, 0)
    whole = pl.BlockSpec(memory_space=pltpu.VMEM)
    return pl.pallas_call(
        functools.partial(_attn_prompt_kernel, n_keys=s, n_sel=n_sel),
        grid=(s // ATT_TQ,),
        in_specs=[
            pl.BlockSpec((ATT_TQ, GROUP), row),
            pl.BlockSpec((ATT_TQ, LANES), row),
            pl.BlockSpec((ATT_TQ, GROUP), row),
            whole, whole, whole,
        ],
        out_specs=pl.BlockSpec((ATT_TQ, GROUP), row),
        out_shape=jax.ShapeDtypeStruct((s, GROUP), F32),
        scratch_shapes=[
            pltpu.VMEM((nkb_total, ATT_TQ, ATT_TK), F32),
            pltpu.VMEM((H_I, ATT_TQ, LANES), F32),
            pltpu.VMEM((H_I, ATT_TQ, 3 * D_I), BF16),
            pltpu.VMEM((H_B, ATT_TQ, HD_B), BF16),
            pltpu.VMEM((H_B, ATT_TQ, ATT_TK), F32),
            pltpu.VMEM((H_B, ATT_TQ, ATT_TK), F32),
            pltpu.VMEM((H_B, ATT_TQ, LANES), F32),
            pltpu.VMEM((H_B, ATT_TQ, LANES), F32),
            pltpu.VMEM((ATT_TQ, ATT_TK), F32),
            pltpu.VMEM((ATT_TQ, ATT_TK), F32),
            pltpu.VMEM((H_B, ATT_TQ, ATT_TK), BF16),
            pltpu.VMEM((H_B, ATT_TQ, ATT_TK), BF16),
            pltpu.VMEM((H_B, ATT_TQ, LANES), F32),
            pltpu.VMEM((H_B, ATT_TQ, 2 * LANES), F32),
        ],
        compiler_params=pltpu.CompilerParams(
            dimension_semantics=("arbitrary",), vmem_limit_bytes=VMEM_LIMIT_BYTES),
        name="attn_prompt",
    )(qi, kw, qb, kit_blocks, kt_blocks, v_bf)


def _attn_sample_kernel(pt_ref, qi_ref, kw_ref, qb_ref, kin_ref, kn_ref, vn_ref,
                        ckidx_hbm, ck_hbm, cv_hbm, o_ref,
                        kid_buf, k_buf, v_buf, sc_ref, m_ref, l_ref, acc_ref, sem_i, sem_k, sem_v,
                        *, n_pages, t_new, n_sel):
    b = pl.program_id(0)
    r = SUBLANES
    nrow = H_B * r
    n_chunks = n_pages // KV_CHUNK_PAGES

    def page_lanes(p):
        start = (p % KV_CHUNK_PAGES) * PAGE_SIZE
        return pl.ds(start if isinstance(start, int) else pl.multiple_of(start, PAGE_SIZE), PAGE_SIZE)

    def kidx_copy(seq, p):
        slot = seq % 2
        return pltpu.make_async_copy(ckidx_hbm.at[pt_ref[seq, p]],
                                     kid_buf.at[slot, p // KV_CHUNK_PAGES, :, page_lanes(p)], sem_i.at[slot])

    def kv_copies(c, slot, p):
        page = pt_ref[b, c * KV_CHUNK_PAGES + p]
        return (pltpu.make_async_copy(ck_hbm.at[page], k_buf.at[slot, :, :, page_lanes(p)], sem_k.at[slot]),
                pltpu.make_async_copy(cv_hbm.at[page], v_buf.at[slot, :, :, page_lanes(p)], sem_v.at[slot]))

    def start_chunk(c, slot):
        for p in range(KV_CHUNK_PAGES):
            ck, cv = kv_copies(c, slot, p)
            ck.start()
            cv.start()

    def wait_chunk(c, slot):
        for p in range(KV_CHUNK_PAGES):
            ck, cv = kv_copies(c, slot, p)
            ck.wait()
            cv.wait()

    def start_kidx(seq):
        def body(p, _):
            kidx_copy(seq, p).start()
            return 0
        lax.fori_loop(0, n_pages, body, 0)

    def wait_kidx(p, _):
        kidx_copy(b, p).wait()
        return 0

    @pl.when(b == 0)
    def _():
        start_kidx(b)

    for c0 in range(KV_SLOTS - 1):
        start_chunk(c0, c0)

    @pl.when(b + 1 < pl.num_programs(0))
    def _():
        start_kidx(b + 1)

    kw = kw_ref[0]
    qi = qi_ref[0]
    q_hi, q_lo, w_rows = [], [], []
    for h in range(H_I):
        hi, lo = _split_bf16(qi[:, h * D_I:(h + 1) * D_I])
        q_hi.append(hi)
        q_lo.append(lo)
        w_rows.append(jnp.broadcast_to(kw[:, D_I + h:D_I + h + 1], (r, LANES)))
    q_hi = jnp.concatenate(q_hi, axis=0)
    q_hl = jnp.concatenate([q_hi, jnp.concatenate(q_lo, axis=0)], axis=0)
    w_rows = jnp.concatenate(w_rows, axis=0)
    nt = (((1,), (1,)), ((), ()))

    def key_scores(keys_t):
        k_hi, k_lo = _split_bf16(keys_t)
        d2 = jnp.dot(q_hl, k_hi, preferred_element_type=F32)
        d = d2[:H_I * r] + d2[H_I * r:] + jnp.dot(q_hi, k_lo, preferred_element_type=F32)
        out = []
        for u in range(keys_t.shape[1] // LANES):
            us = slice(u * LANES, (u + 1) * LANES)
            wd = w_rows * jnp.maximum(d[:, us], 0.0)
            acc = wd[0:r]
            for h in range(1, H_I):
                acc = acc + wd[h * r:(h + 1) * r]
            out.append(acc)
        return out

    lax.fori_loop(0, n_pages, wait_kidx, 0)

    def score_body(c, _):
        for u, tile in enumerate(key_scores(kid_buf[b % 2, c])):
            sc_ref[c, :, u * LANES:(u + 1) * LANES] = tile
        return 0

    lax.fori_loop(0, n_chunks, score_body, 0)

    lane = lax.broadcasted_iota(I32, (r, LANES), 1)
    t_row = jnp.minimum(lax.broadcasted_iota(I32, (r, LANES), 0), t_new - 1)
    new_sc = key_scores(kin_ref[0])[0]
    new_vis = lane <= t_row
    sc_ref[n_chunks] = jnp.full((r, KV_CHUNK_PAGES * LANES), -jnp.inf, F32)
    sc_ref[n_chunks, :, 0:LANES] = jnp.where(lane >= t_new, -jnp.inf, jnp.where(new_vis, new_sc, NEG_SCORE))

    cut = _select_rows(sc_ref, n_chunks + 1, jnp.zeros((r, LANES), I32), n_sel)

    qb = qb_ref[0] * (HD_B ** -0.5)
    col_head = lax.shift_right_logical(lax.broadcasted_iota(I32, (r, GROUP), 1), HD_B.bit_length() - 1)
    q_bd = jnp.concatenate([jnp.where(col_head == h, qb, 0.0) for h in range(H_B)], axis=0).astype(BF16)

    m_ref[...] = jnp.full(m_ref.shape, MASK_VALUE, F32)
    l_ref[...] = jnp.zeros(l_ref.shape, F32)
    acc_ref[...] = jnp.zeros(acc_ref.shape, F32)

    def attend(blk, n_blk, k_t, v_t, extra_mask):
        bias = []
        for u in range(n_blk):
            sel = sc_ref[blk, :, u * LANES:(u + 1) * LANES] >= cut
            if extra_mask is not None:
                sel = jnp.logical_and(sel, extra_mask)
            bias.append(jnp.where(sel, 0.0, MASK_VALUE))
        bias = jnp.tile(jnp.concatenate(bias, axis=1), (H_B, 1))
        s = jnp.dot(q_bd, k_t.astype(BF16), preferred_element_type=F32) + bias
        m_old = m_ref[...]
        m_new = jnp.maximum(m_old, jnp.broadcast_to(jnp.max(s, axis=1, keepdims=True), m_old.shape))
        alpha = jnp.exp(m_old - m_new)
        p = jnp.exp(s - jnp.tile(m_new, (1, n_blk)))
        p_sum = p[:, :LANES]
        for u in range(1, n_blk):
            p_sum = p_sum + p[:, u * LANES:(u + 1) * LANES]
        l_ref[...] = alpha * l_ref[...] + p_sum
        pv = lax.dot_general(p.astype(BF16), v_t.astype(BF16), nt, preferred_element_type=F32)
        acc_ref[...] = jnp.tile(alpha, (1, GROUP // LANES)) * acc_ref[...] + pv
        m_ref[...] = m_new

    def chunk_body(c, _):
        slot = c % KV_SLOTS
        wait_chunk(c, slot)
        ahead = c + KV_SLOTS - 1

        @pl.when(ahead < n_chunks)
        def _():
            start_chunk(ahead, ahead % KV_SLOTS)

        n_keys_chunk = KV_CHUNK_PAGES * PAGE_SIZE
        attend(c, KV_CHUNK_PAGES, k_buf[slot].reshape(GROUP, n_keys_chunk),
               v_buf[slot].reshape(GROUP, n_keys_chunk), None)
        return 0

    lax.fori_loop(0, n_chunks, chunk_body, 0)
    attend(n_chunks, 1, kn_ref[0].reshape(GROUP, PAGE_SIZE), vn_ref[0].reshape(GROUP, PAGE_SIZE), new_vis)

    out = jnp.zeros((r, GROUP), F32)
    for h in range(H_B):
        rs = slice(h * r, (h + 1) * r)
        denom = jnp.sum(l_ref[rs, :], axis=1, keepdims=True)
        out = out + jnp.where(col_head == h, acc_ref[rs, :] / denom, 0.0)
    o_ref[0] = out


def _attn_sample(page_table, qi, kw, qb, ki_new, k_new, v_new, cache_kidx, cache_k, cache_v, t_new, n_sel):
    nb, n_pages = page_table.shape
    r = SUBLANES
    nrow = H_B * r
    per_b = lambda b, pt: (b, 0, 0)
    per_b4 = lambda b, pt: (b, 0, 0, 0)
    anyspec = pl.BlockSpec(memory_space=pl.ANY)
    chunk_keys = KV_CHUNK_PAGES * PAGE_SIZE
    n_chunks = n_pages // KV_CHUNK_PAGES
    grid_spec = pltpu.PrefetchScalarGridSpec(
        num_scalar_prefetch=1,
        grid=(nb,),
        in_specs=[
            pl.BlockSpec((1, r, GROUP), per_b),
            pl.BlockSpec((1, r, LANES), per_b),
            pl.BlockSpec((1, r, GROUP), per_b),
            pl.BlockSpec((1, D_I, PAGE_SIZE), per_b),
            pl.BlockSpec((1, H_B, HD_B, PAGE_SIZE), per_b4),
            pl.BlockSpec((1, H_B, HD_B, PAGE_SIZE), per_b4),
            anyspec, anyspec, anyspec,
        ],
        out_specs=pl.BlockSpec((1, r, GROUP), per_b),
        scratch_shapes=[
            pltpu.VMEM((2, n_chunks, D_I, chunk_keys), F32),
            pltpu.VMEM((KV_SLOTS, H_B, HD_B, chunk_keys), F32),
            pltpu.VMEM((KV_SLOTS, H_B, HD_B, chunk_keys), F32),
            pltpu.VMEM((n_chunks + 1, r, chunk_keys), F32),
            pltpu.VMEM((nrow, LANES), F32),
            pltpu.VMEM((nrow, LANES), F32),
            pltpu.VMEM((nrow, GROUP), F32),
            pltpu.SemaphoreType.DMA((2,)),
            pltpu.SemaphoreType.DMA((KV_SLOTS,)),
            pltpu.SemaphoreType.DMA((KV_SLOTS,)),
        ],
    )
    return pl.pallas_call(
        functools.partial(_attn_sample_kernel, n_pages=n_pages, t_new=t_new, n_sel=n_sel),
        grid_spec=grid_spec,
        out_shape=jax.ShapeDtypeStruct((nb, r, GROUP), F32),
        compiler_params=pltpu.CompilerParams(
            dimension_semantics=("arbitrary",), vmem_limit_bytes=VMEM_LIMIT_BYTES),
        name="attn_sample",
    )(page_table, qi, kw, qb, ki_new, k_new, v_new, cache_kidx, cache_k, cache_v)


def _out_ffn_kernel(x_ref, oa_ref, ob_ref, p1_ref, p2_ref, wo_ref, g2_ref, wu_ref, cw_ref, cb_ref, wd_ref, gf_ref,
                    y_ref, u_ref, carry_ref, x1_ref, h2_ref, *, d_ff, ff_tile, seq_rows, carry_mode):
    tm = x_ref.shape[0]
    i = pl.program_id(0)
    mixed = jnp.concatenate([oa_ref[...], ob_ref[...]], axis=1).astype(BF16)
    x1 = x_ref[...] + jnp.dot(mixed, wo_ref[...], preferred_element_type=F32)
    h2_ref[...] = _rms(x1, g2_ref[...]).astype(BF16)
    x1_ref[...] = x1

    row = lax.broadcasted_iota(I32, (tm, ff_tile), 0)
    if carry_mode:
        @pl.when(i == 0)
        def _():
            carry_ref[...] = jnp.zeros(carry_ref.shape, F32)
        t_in_seq = row
    else:
        t_in_seq = row & (seq_rows - 1)

    for c in range(d_ff // ff_tile):
        cs = slice(c * ff_tile, (c + 1) * ff_tile)
        h2 = h2_ref[...]
        u = jnp.dot(h2, wu_ref[:, cs], preferred_element_type=F32)
        v = jnp.dot(h2, wu_ref[:, d_ff + c * ff_tile:d_ff + (c + 1) * ff_tile], preferred_element_type=F32)
        if carry_mode:
            prev = carry_ref[:, cs]
            p1 = jnp.broadcast_to(prev[SUBLANES - 1:SUBLANES], (tm, ff_tile))
            p2 = jnp.where(row == 0, jnp.broadcast_to(prev[SUBLANES - 2:SUBLANES - 1], (tm, ff_tile)), p1)
            carry_ref[:, cs] = u[tm - SUBLANES:]
            u_ref[:, cs] = u[tm - SUBLANES:]
        else:
            p1 = p1_ref[:, cs]
            p2 = p2_ref[:, cs]
            u_ref[:, cs] = u
        u1 = jnp.where(t_in_seq >= 1, pltpu.roll(u, 1, 0), p1)
        u2 = jnp.where(t_in_seq >= 2, pltpu.roll(u, 2, 0), p2)
        conv = cb_ref[:, cs] + cw_ref[0:1, cs] * u2 + cw_ref[1:2, cs] * u1 + cw_ref[2:3, cs] * u
        gate = (_silu(conv) * v).astype(BF16)
        x1_ref[...] += jnp.dot(gate, wd_ref[cs, :], preferred_element_type=F32)

    y_ref[...] = _rms(x1_ref[...], gf_ref[...])


def _out_ffn(x2d, oa, ob, p1, p2, w_out, g2, w_up, conv_w, conv_b, w_down, gf, tm, seq_rows, carry_mode):
    m, d = x2d.shape
    d_ff = w_down.shape[0]
    ff_tile = 256
    row = lambda i: (i, 0)
    const = lambda i: (0, 0)
    if carry_mode:
        prev_spec = pl.BlockSpec((SUBLANES, d_ff), const)
        u_spec = pl.BlockSpec((SUBLANES, d_ff), const)
        u_shape = jax.ShapeDtypeStruct((SUBLANES, d_ff), F32)
    else:
        prev_spec = pl.BlockSpec((tm, d_ff), row)
        u_spec = pl.BlockSpec((tm, d_ff), row)
        u_shape = jax.ShapeDtypeStruct((m, d_ff), F32)
    return pl.pallas_call(
        functools.partial(_out_ffn_kernel, d_ff=d_ff, ff_tile=ff_tile, seq_rows=seq_rows, carry_mode=carry_mode),
        grid=(m // tm,),
        in_specs=[
            pl.BlockSpec((tm, d), row),
            pl.BlockSpec((tm, GROUP), row),
            pl.BlockSpec((tm, GROUP), row),
            prev_spec, prev_spec,
            pl.BlockSpec(w_out.shape, const, pipeline_mode=pl.Buffered(1)),
            pl.BlockSpec((1, d), const),
            pl.BlockSpec(w_up.shape, const, pipeline_mode=pl.Buffered(1)),
            pl.BlockSpec(conv_w.shape, const),
            pl.BlockSpec((1, d_ff), const),
            pl.BlockSpec(w_down.shape, const, pipeline_mode=pl.Buffered(1)),
            pl.BlockSpec((1, d), const),
        ],
        out_specs=[pl.BlockSpec((tm, d), row), u_spec],
        out_shape=[jax.ShapeDtypeStruct((m, d), F32), u_shape],
        scratch_shapes=[pltpu.VMEM((SUBLANES, d_ff), F32), pltpu.VMEM((tm, d), F32), pltpu.VMEM((tm, d), BF16)],
        compiler_params=pltpu.CompilerParams(
            dimension_semantics=("arbitrary",), vmem_limit_bytes=VMEM_LIMIT_BYTES),
        name="out_ffn",
    )(x2d, oa, ob, p1, p2, w_out, g2, w_up, conv_w, conv_b, w_down, gf)


def _rope_tables(pos):
    half = ROT // 2
    inv = jnp.power(ROPE_THETA, -jnp.arange(half, dtype=F32) * (2.0 / ROT))
    ang = pos.astype(F32)[:, None] * inv[None, :]
    cos, sin = jnp.cos(ang), jnp.sin(ang)
    n = pos.shape[0]
    ones = jnp.ones((n, HD_B - ROT), F32)
    zeros = jnp.zeros((n, HD_B - ROT), F32)
    zh = jnp.zeros((n, half), F32)
    c = jnp.concatenate([cos, cos, ones], axis=1)
    sa = jnp.concatenate([zh, sin, zeros], axis=1)
    sb = jnp.concatenate([-sin, zh, zeros], axis=1)
    rep = LANES // HD_B
    return jnp.tile(c, (1, rep)), jnp.tile(sa, (1, rep)), jnp.tile(sb, (1, rep))


def kernel(x_prompt, x_sample, cache_k, cache_v, cache_kidx, state_hgrn, state_conv, page_table,
           norm_mix_gain, w_in, lb_logits, hgrn_norm_gain, w_out, norm_ffn_gain, w_up, conv_w,
           conv_b, w_down, final_norm_gain):
    bp, s, d = x_prompt.shape
    nb, t_new, _ = x_sample.shape
    depth = w_in.shape[0]
    assert bp == 1 and depth == 1
    n_pages = page_table.shape[1]
    past = n_pages * PAGE_SIZE
    d_ff = w_down.shape[1]
    assert s % (2 * ATT_TK) == 0 and s % HGRN_CHUNK == 0
    assert n_pages % KV_CHUNK_PAGES == 0 and n_pages // KV_CHUNK_PAGES >= KV_SLOTS
    assert t_new <= SUBLANES and CONV_W - 1 <= t_new

    lower_bounds = jnp.cumsum(jax.nn.softmax(lb_logits.astype(F32), axis=0), axis=0)
    lb = lower_bounds[0][None, :]
    n_in = w_in.shape[2]
    n_pad = 8 * GROUP + LANES
    w_in_bf = jnp.pad(w_in[0], ((0, 0), (0, n_pad - n_in))).astype(BF16)
    w_out_bf = w_out[0].astype(BF16)
    w_up_bf = w_up[0].astype(BF16)
    w_down_bf = w_down[0].astype(BF16)
    g_mix = norm_mix_gain[0][None, :]
    g_ffn = norm_ffn_gain[0][None, :]
    g_fin = final_norm_gain[None, :]
    g_hgrn = hgrn_norm_gain[0][None, :]
    conv_b2 = conv_b[0][None, :]

    xp = x_prompt.reshape(s, d)
    cp, sap, sbp = _rope_tables(jnp.arange(s, dtype=I32))
    qa, ka, lf, ia, ga, qb, kb, vb, qi, kw = _proj_in(xp, g_mix, w_in_bf, lb, cp, sap, sbp, tm=PROJ_ROWS)

    s0 = jnp.zeros((1, H_A, DK_A, DK_A), F32)
    r3 = lambda a: a.reshape(1, s, GROUP)
    oa_p, st_p = _hgrn(r3(qa), r3(ka), r3(lf), r3(ia), r3(ga), g_hgrn, s0, HGRN_CHUNK, HGRN_SUB)

    nkb = s // ATT_TK
    ki_hi, ki_lo = _split_bf16(kw[:, :D_I])
    kit = jnp.concatenate([ki_hi, ki_hi, ki_lo], axis=1)
    kit_blocks = kit.reshape(nkb, ATT_TK, 3 * D_I).transpose(0, 2, 1)
    kt_blocks = kb.astype(BF16).reshape(nkb, ATT_TK, GROUP).transpose(0, 2, 1)
    ob_p = _attn_prompt(qi, kw, qb, kit_blocks, kt_blocks, vb.astype(BF16), min(TOPK_MAX, s // 4))

    zero_prev = jnp.zeros((SUBLANES, d_ff), F32)
    y_p, u_tail = _out_ffn(xp, oa_p.reshape(s, GROUP), ob_p, zero_prev, zero_prev, w_out_bf, g_ffn, w_up_bf,
                           conv_w[0], conv_b2, w_down_bf, g_fin, tm=PROJ_ROWS, seq_rows=s, carry_mode=True)

    ms = nb * t_new
    xs = x_sample.reshape(ms, d)
    pos_s = jnp.tile(past + jnp.arange(t_new, dtype=I32), nb)
    cs, sas, sbs = _rope_tables(pos_s)
    qa2, ka2, lf2, ia2, ga2, qb2, kb2, vb2, qi2, kw2 = _proj_in(xs, g_mix, w_in_bf, lb, cs, sas, sbs, tm=ms)

    pad_front = SUBLANES - t_new
    fp = lambda a: jnp.pad(a.reshape(nb, t_new, GROUP), ((0, 0), (pad_front, 0), (0, 0)))
    s0_s = jnp.swapaxes(state_hgrn[0], -1, -2)
    oa_s, st_s = _hgrn(fp(qa2), fp(ka2), fp(lf2), fp(ia2), fp(ga2), g_hgrn, s0_s, SUBLANES, SUBLANES)
    oa_s = oa_s[:, pad_front:, :].reshape(ms, GROUP)

    def rp(a):
        a = a.reshape(nb, t_new, a.shape[-1])
        return jnp.concatenate([a, jnp.broadcast_to(a[:, -1:], (nb, SUBLANES - t_new, a.shape[-1]))], axis=1)
    page_pad = lambda a: jnp.pad(a.reshape(nb, t_new, a.shape[-1]), ((0, 0), (0, PAGE_SIZE - t_new), (0, 0)))
    new_page = lambda a: jnp.swapaxes(page_pad(a), 1, 2)
    heads = lambda a: new_page(a).reshape(nb, H_B, HD_B, PAGE_SIZE)
    ob_s = _attn_sample(
        page_table, rp(qi2), rp(kw2), rp(qb2), new_page(kw2[:, :D_I]), heads(kb2), heads(vb2),
        jnp.transpose(cache_kidx[0], (0, 2, 1)), jnp.transpose(cache_k[0], (0, 2, 3, 1)),
        jnp.transpose(cache_v[0], (0, 2, 3, 1)),
        t_new, min(TOPK_MAX, (past + t_new) // 4))
    ob_s = ob_s[:, :t_new, :].reshape(ms, GROUP)

    sc0 = state_conv[0]
    zrow = jnp.zeros((nb, t_new - 1, d_ff), F32)
    p1 = jnp.concatenate([sc0[:, 1:2], zrow], axis=1).reshape(ms, d_ff)
    p2 = jnp.concatenate([sc0[:, 0:2], zrow[:, 1:]], axis=1).reshape(ms, d_ff)
    y_s, u_s = _out_ffn(xs, oa_s, ob_s, p1, p2, w_out_bf, g_ffn, w_up_bf, conv_w[0], conv_b2, w_down_bf, g_fin,
                        tm=ms, seq_rows=t_new, carry_mode=False)

    return (
        y_p.reshape(1, s, d),
        y_s.reshape(nb, t_new, d),
        kb.reshape(1, 1, s, H_B, HD_B),
        vb.reshape(1, 1, s, H_B, HD_B),
        kw[:, :D_I].reshape(1, 1, s, D_I),
        jnp.swapaxes(st_p, -1, -2).reshape(1, 1, H_A, DK_A, DK_A),
        u_tail[SUBLANES - (CONV_W - 1):].reshape(1, 1, CONV_W - 1, d_ff),
        kb2.reshape(1, nb, t_new, H_B, HD_B),
        vb2.reshape(1, nb, t_new, H_B, HD_B),
        kw2[:, :D_I].reshape(1, nb, t_new, D_I),
        jnp.swapaxes(st_s, -1, -2).reshape(1, nb, H_A, DK_A, DK_A),
        u_s.reshape(nb, t_new, d_ff)[:, t_new - (CONV_W - 1):].reshape(1, nb, CONV_W - 1, d_ff),
    )
```
